```python
import math
import jax, jax.numpy as jnp
from jax import lax
import numpy as np

D_MODEL = 1024
BATCH = 4
SEQ = 8192
DEPTH = 2
DEC_BATCH = 128
DEC_SEQ = 8
PAST_LEN = 16384
PAGE_SIZE = 128

N_META = 16
EPS = 1e-6
POOL_GROUPS = 4
POOL_GROUP_DIM = D_MODEL // 8
POOL_WIDTH = POOL_GROUPS * POOL_GROUP_DIM
POOL_WINDOWS = (2, 4, 8, 16)
POOL_BUF = max(POOL_WINDOWS) - 1
DN_HEADS = 4
DN_DK = 128
DN_DV = 128
DN_QK = DN_HEADS * DN_DK
DN_VW = DN_HEADS * DN_DV
CONV_W = 4
CONV_CH = 2 * DN_QK + DN_VW
DN_CHUNK = 64
SWA_HEADS = 8
SWA_KV_HEADS = 2
SWA_GROUP = SWA_HEADS // SWA_KV_HEADS
SWA_HD = 64
SWA_WIDTH = SWA_HEADS * SWA_HD
SWA_KV_WIDTH = SWA_KV_HEADS * SWA_HD
SWA_WINDOW = 128
SWA_BLOCK = 128
ROPE_THETA = 10000.0
D_FF = 4 * D_MODEL
IN_SPLITS = (POOL_WIDTH, DN_QK, DN_QK, DN_VW, DN_VW, DN_HEADS, DN_HEADS,
             SWA_WIDTH, SWA_KV_WIDTH, SWA_KV_WIDTH, D_MODEL, D_MODEL, D_MODEL)
D_IN = sum(IN_SPLITS)

kernel_name = 'hybrid_pool_delta_swa_decode_step'


def rmsnorm(x, w):
    xf = x.astype(jnp.float32)
    y = xf * lax.rsqrt(jnp.mean(xf * xf, axis=-1, keepdims=True) + EPS)
    return (y * w.astype(jnp.float32)).astype(x.dtype)


def l2norm(x):
    xf = x.astype(jnp.float32)
    return (xf * lax.rsqrt(jnp.sum(xf * xf, axis=-1, keepdims=True) + EPS)).astype(x.dtype)


def rope(x, pos):
    half = x.shape[-1] // 2
    inv = ROPE_THETA ** (-jnp.arange(half, dtype=jnp.float32) / half)
    ang = pos.astype(jnp.float32)[:, None] * inv[None, :]
    cos = jnp.cos(ang)[None, :, None, :]
    sin = jnp.sin(ang)[None, :, None, :]
    xf = x.astype(jnp.float32)
    x1, x2 = xf[..., :half], xf[..., half:]
    return jnp.concatenate([x1 * cos - x2 * sin, x2 * cos + x1 * sin], axis=-1).astype(x.dtype)


def pool_mix(u, buf, pos0, w_pool, s_pool):
    B, T, _ = u.shape
    ext = jnp.concatenate([buf.astype(u.dtype), u], axis=1).astype(jnp.float32)
    c = jnp.cumsum(jnp.pad(ext, ((0, 0), (1, 0), (0, 0))), axis=1)
    end = c[:, POOL_BUF + 1:]
    pos = pos0 + jnp.arange(T)
    means = []
    for g, w in enumerate(POOL_WINDOWS):
        sl = slice(g * POOL_GROUP_DIM, (g + 1) * POOL_GROUP_DIM)
        start = c[:, POOL_BUF + 1 - w:POOL_BUF + 1 - w + T, sl]
        cnt = jnp.minimum(w, pos + 1).astype(jnp.float32)[None, :, None]
        means.append((end[..., sl] - start) / cnt)
    d = (jnp.concatenate(means, axis=-1) - ext[:, POOL_BUF:]).astype(u.dtype)
    d = d.reshape(B, T, POOL_GROUPS, POOL_GROUP_DIM)
    y = jnp.einsum('btgc,gcd->btgd', d, w_pool).reshape(B, T, POOL_WIDTH)
    return y * s_pool, ext[:, -POOL_BUF:].astype(u.dtype)


def short_conv(u, buf, w_conv):
    T = u.shape[1]
    ext = jnp.concatenate([buf.astype(u.dtype), u], axis=1)
    y = ext[:, 0:T] * w_conv[0]
    for j in range(1, CONV_W):
        y = y + ext[:, j:j + T] * w_conv[j]
    return jax.nn.silu(y), ext[:, T:]


def gated_delta(q, k, v, g, beta, S0, chunk):
    B, T, H, _ = q.shape
    DV = v.shape[-1]
    N = T // chunk
    f32 = jnp.float32

    def blk(x):
        return x.astype(f32).reshape(B, N, chunk, H, -1).transpose(1, 0, 3, 2, 4)

    qc, kc, vc = blk(q), blk(k), blk(v)
    gc = g.astype(f32).reshape(B, N, chunk, H).transpose(1, 0, 3, 2)
    bc = beta.astype(f32).reshape(B, N, chunk, H).transpose(1, 0, 3, 2)
    G = jnp.cumsum(gc, axis=-1)
    tri = jnp.tril(jnp.ones((chunk, chunk), bool))
    strict = jnp.tril(jnp.ones((chunk, chunk), bool), -1)
    decay = jnp.exp(jnp.where(tri, G[..., :, None] - G[..., None, :], -jnp.inf))
    kk = jnp.einsum('nbhid,nbhjd->nbhij', kc, kc)
    M = jnp.where(strict, kk * decay * bc[..., :, None], 0.0)
    eye = jnp.eye(chunk, dtype=f32)
    Tinv = lax.linalg.triangular_solve(eye + M, jnp.broadcast_to(eye, M.shape),
                                       left_side=True, lower=True, unit_diagonal=True)
    u = Tinv @ (vc * bc[..., None])
    w = Tinv @ (kc * (bc * jnp.exp(G))[..., None])
    qk = jnp.einsum('nbhid,nbhjd->nbhij', qc, kc) * decay
    qg = qc * jnp.exp(G)[..., None]
    kg = kc * jnp.exp(G[..., -1:] - G)[..., None]
    gl = jnp.exp(G[..., -1])

    def step(S, xs):
        qk_i, qg_i, kg_i, u_i, w_i, gl_i = xs
        dlt = u_i - w_i @ S
        o = qg_i @ S + qk_i @ dlt
        S = S * gl_i[..., None, None] + jnp.swapaxes(kg_i, -1, -2) @ dlt
        return S, o

    S, o = lax.scan(step, S0.astype(f32), (qk, qg, kg, u, w, gl))
    o = o.transpose(1, 0, 3, 2, 4).reshape(B, T, H, DV)
    return o.astype(v.dtype), S


def deltanet(q, k, v, z, b, a, conv_buf, S0, conv_w, a_log, dt_bias, onorm_w, segments):
    B, T, _ = q.shape
    qkv, conv_new = short_conv(jnp.concatenate([q, k, v], axis=-1), conv_buf, conv_w)
    q, k, v = jnp.split(qkv, [DN_QK, 2 * DN_QK], axis=-1)
    q = l2norm(q.reshape(B, T, DN_HEADS, DN_DK)) * (DN_DK ** -0.5)
    k = l2norm(k.reshape(B, T, DN_HEADS, DN_DK))
    v = v.reshape(B, T, DN_HEADS, DN_DV)
    beta = jax.nn.sigmoid(b.astype(jnp.float32))
    g = -jnp.exp(a_log.astype(jnp.float32)) * jax.nn.softplus(a.astype(jnp.float32) + dt_bias.astype(jnp.float32))
    S = S0
    outs = []
    start = 0
    for length, chunk in segments:
        sl = slice(start, start + length)
        o, S = gated_delta(q[:, sl], k[:, sl], v[:, sl], g[:, sl], beta[:, sl], S, chunk)
        outs.append(o)
        start += length
    o = jnp.concatenate(outs, axis=1)
    o = rmsnorm(o, onorm_w) * jax.nn.silu(z.reshape(B, T, DN_HEADS, DN_DV))
    return o.reshape(B, T, DN_VW), conv_new, S.astype(S0.dtype)


def sink_probs(s, mask, sink):
    s = jnp.where(mask, s, -jnp.inf)
    m = jnp.maximum(jnp.max(s, axis=-1, keepdims=True), sink)
    e = jnp.exp(s - m)
    return e / (jnp.sum(e, axis=-1, keepdims=True) + jnp.exp(sink - m))


def swa_prompt(q, k, v, sinks):
    B, L = q.shape[:2]
    P = (-L) % SWA_BLOCK
    nb = (L + P) // SWA_BLOCK
    f32 = jnp.float32
    qb = jnp.pad(q, ((0, 0), (P, 0), (0, 0), (0, 0))).reshape(B, nb, SWA_BLOCK, SWA_KV_HEADS, SWA_GROUP, SWA_HD)
    kp = jnp.pad(k, ((0, 0), (P + SWA_BLOCK, 0), (0, 0), (0, 0))).reshape(B, nb + 1, SWA_BLOCK, SWA_KV_HEADS, SWA_HD)
    vp = jnp.pad(v, ((0, 0), (P + SWA_BLOCK, 0), (0, 0), (0, 0))).reshape(B, nb + 1, SWA_BLOCK, SWA_KV_HEADS, SWA_HD)
    kw = jnp.concatenate([kp[:, :-1], kp[:, 1:]], axis=2)
    vw = jnp.concatenate([vp[:, :-1], vp[:, 1:]], axis=2)
    s = jnp.einsum('bnqkgd,bnjkd->bnkgqj', qb.astype(f32), kw.astype(f32)) * (SWA_HD ** -0.5)
    r = jnp.arange(SWA_BLOCK)[:, None]
    j = jnp.arange(2 * SWA_BLOCK)[None, :]
    diff = r - j + SWA_BLOCK
    kpos = jnp.arange(nb)[:, None, None] * SWA_BLOCK + j[None] - SWA_BLOCK - P
    mask = (diff >= 0) & (diff < SWA_WINDOW) & (kpos >= 0)
    sink = sinks.astype(f32).reshape(SWA_KV_HEADS, SWA_GROUP)[None, None, :, :, None, None]
    prob = sink_probs(s, mask[None, :, None, None], sink)
    o = jnp.einsum('bnkgqj,bnjkd->bnqkgd', prob, vw.astype(f32)).reshape(B, nb * SWA_BLOCK, SWA_WIDTH)
    return o[:, P:].astype(q.dtype)


def swa_sample(q, k, v, k_buf, v_buf, sinks):
    B, T = q.shape[:2]
    W = k_buf.shape[1]
    f32 = jnp.float32
    ke = jnp.concatenate([k_buf.astype(k.dtype), k], axis=1)
    ve = jnp.concatenate([v_buf.astype(v.dtype), v], axis=1)
    qg = q.reshape(B, T, SWA_KV_HEADS, SWA_GROUP, SWA_HD)
    s = jnp.einsum('btkgd,bjkd->bkgtj', qg.astype(f32), ke.astype(f32)) * (SWA_HD ** -0.5)
    diff = (jnp.arange(T)[:, None] + W) - jnp.arange(W + T)[None, :]
    mask = (diff >= 0) & (diff < SWA_WINDOW)
    sink = sinks.astype(f32).reshape(SWA_KV_HEADS, SWA_GROUP)[None, :, :, None, None]
    prob = sink_probs(s, mask, sink)
    o = jnp.einsum('bkgtj,bjkd->btkgd', prob, ve.astype(f32)).reshape(B, T, SWA_WIDTH)
    return o.astype(q.dtype), ke[:, -W:], ve[:, -W:]


def mixer(xn, p, pos0, pool_buf, conv_buf, S0, k_buf, v_buf, is_prompt):
    B, T, _ = xn.shape
    h = xn @ p['w_in']
    (u_a, q_b, k_b, v_b, z_b, b_b, a_b, q_c, k_c, v_c, g_a, g_b, g_c) = jnp.split(
        h, np.cumsum(IN_SPLITS)[:-1].tolist(), axis=-1)
    o_a, pool_new = pool_mix(u_a, pool_buf, pos0, p['pool_w'], p['pool_scale'])
    segments = ((N_META, N_META), (T - N_META, DN_CHUNK)) if is_prompt else ((T, T),)
    o_b, conv_new, S_new = deltanet(q_b, k_b, v_b, z_b, b_b, a_b, conv_buf, S0, p['dn_conv_w'],
                                    p['dn_a_log'], p['dn_dt_bias'], p['dn_onorm_w'], segments)
    pos = pos0 + jnp.arange(T)
    qh = rope(q_c.reshape(B, T, SWA_HEADS, SWA_HD), pos)
    kh = rope(k_c.reshape(B, T, SWA_KV_HEADS, SWA_HD), pos)
    vh = v_c.reshape(B, T, SWA_KV_HEADS, SWA_HD)
    if is_prompt:
        o_c = swa_prompt(qh, kh, vh, p['swa_sinks'])
        k_new, v_new = kh[:, -SWA_WINDOW:], vh[:, -SWA_WINDOW:]
    else:
        o_c, k_new, v_new = swa_sample(qh, kh, vh, k_buf, v_buf, p['swa_sinks'])
    m = (jax.nn.sigmoid(g_a) * (o_a @ p['proj_a'])
         + jax.nn.sigmoid(g_b) * (o_b @ p['proj_b'])
         + jax.nn.sigmoid(g_c) * (o_c @ p['proj_c']))
    return m @ p['w_out'], (pool_new, conv_new, S_new, k_new, v_new)


def block(x, p, pos0, pool_buf, conv_buf, S0, k_buf, v_buf, is_prompt):
    mix, new = mixer(rmsnorm(x, p['norm1_w']), p, pos0, pool_buf, conv_buf, S0, k_buf, v_buf, is_prompt)
    h = x + mix
    f = jnp.square(jax.nn.relu(rmsnorm(h, p['norm2_w']) @ p['w_up'])) @ p['w_down']
    return h + f, new


def setup_inputs(seed: int = 0) -> dict:
    key = jax.random.key(seed)
    ks = list(jax.random.split(key, 32))

    def nrm(i, shape, scale):
        return jax.random.normal(ks[i], shape, jnp.float32) * scale

    swa_buf = min(SWA_WINDOW, PAST_LEN)
    dt = jnp.exp(jax.random.uniform(ks[10], (DEPTH, DN_HEADS), jnp.float32, math.log(1e-3), math.log(1e-1)))
    return {
        'x_prompt': nrm(0, (BATCH, SEQ, D_MODEL), 1.0),
        'x_sample': nrm(1, (DEC_BATCH, DEC_SEQ, D_MODEL), 1.0),
        'state_pool': nrm(2, (DEPTH, DEC_BATCH, POOL_BUF, POOL_WIDTH), 1.0),
        'state_conv': nrm(3, (DEPTH, DEC_BATCH, CONV_W - 1, CONV_CH), 1.0),
        'state_delta': nrm(4, (DEPTH, DEC_BATCH, DN_HEADS, DN_DK, DN_DV), DN_DK ** -0.5),
        'cache_swa_k': nrm(5, (DEPTH, DEC_BATCH, swa_buf, SWA_KV_HEADS, SWA_HD), 1.0),
        'cache_swa_v': nrm(6, (DEPTH, DEC_BATCH, swa_buf, SWA_KV_HEADS, SWA_HD), 1.0),
        'meta_tokens': nrm(7, (N_META, D_MODEL), 1.0),
        'norm1_w': 1.0 + nrm(8, (DEPTH, D_MODEL), 0.05),
        'w_in': nrm(9, (DEPTH, D_MODEL, D_IN), D_MODEL ** -0.5),
        'pool_w': nrm(11, (DEPTH, POOL_GROUPS, POOL_GROUP_DIM, POOL_GROUP_DIM), POOL_GROUP_DIM ** -0.5),
        'pool_scale': 1.0 + nrm(12, (DEPTH, POOL_WIDTH), 0.1),
        'dn_conv_w': nrm(13, (DEPTH, CONV_W, CONV_CH), CONV_W ** -0.5),
        'dn_a_log': jnp.log(jax.random.uniform(ks[14], (DEPTH, DN_HEADS), jnp.float32, 1.0, 16.0)),
        'dn_dt_bias': dt + jnp.log(-jnp.expm1(-dt)),
        'dn_onorm_w': 1.0 + nrm(15, (DEPTH, DN_DV), 0.05),
        'swa_sinks': nrm(16, (DEPTH, SWA_HEADS), 0.5),
        'proj_a': nrm(17, (DEPTH, POOL_WIDTH, D_MODEL), POOL_WIDTH ** -0.5),
        'proj_b': nrm(18, (DEPTH, DN_VW, D_MODEL), DN_VW ** -0.5),
        'proj_c': nrm(19, (DEPTH, SWA_WIDTH, D_MODEL), SWA_WIDTH ** -0.5),
        'w_out': nrm(20, (DEPTH, D_MODEL, D_MODEL), D_MODEL ** -0.5),
        'norm2_w': 1.0 + nrm(21, (DEPTH, D_MODEL), 0.05),
        'w_up': nrm(22, (DEPTH, D_MODEL, D_FF), D_MODEL ** -0.5),
        'w_down': nrm(23, (DEPTH, D_FF, D_MODEL), D_FF ** -0.5),
        'final_norm_w': 1.0 + nrm(24, (D_MODEL,), 0.05),
    }


def reference(x_prompt, x_sample, state_pool, state_conv, state_delta, cache_swa_k, cache_swa_v,
              meta_tokens, norm1_w, w_in, pool_w, pool_scale, dn_conv_w, dn_a_log, dn_dt_bias,
              dn_onorm_w, swa_sinks, proj_a, proj_b, proj_c, w_out, norm2_w, w_up, w_down, final_norm_w):
    Bp = x_prompt.shape[0]
    xp = jnp.concatenate([jnp.broadcast_to(meta_tokens[None].astype(x_prompt.dtype), (Bp, N_META, D_MODEL)),
                          x_prompt], axis=1)
    xs = x_sample
    new_p = []
    new_s = []
    for l in range(DEPTH):
        p = dict(norm1_w=norm1_w[l], w_in=w_in[l], pool_w=pool_w[l], pool_scale=pool_scale[l],
                 dn_conv_w=dn_conv_w[l], dn_a_log=dn_a_log[l], dn_dt_bias=dn_dt_bias[l],
                 dn_onorm_w=dn_onorm_w[l], swa_sinks=swa_sinks[l], proj_a=proj_a[l], proj_b=proj_b[l],
                 proj_c=proj_c[l], w_out=w_out[l], norm2_w=norm2_w[l], w_up=w_up[l], w_down=w_down[l])
        xp, st_p = block(xp, p, 0,
                         jnp.zeros((Bp, POOL_BUF, POOL_WIDTH), xp.dtype),
                         jnp.zeros((Bp, CONV_W - 1, CONV_CH), xp.dtype),
                         jnp.zeros((Bp, DN_HEADS, DN_DK, DN_DV), xp.dtype),
                         None, None, True)
        xs, st_s = block(xs, p, PAST_LEN, state_pool[l], state_conv[l], state_delta[l],
                         cache_swa_k[l], cache_swa_v[l], False)
        new_p.append(st_p)
        new_s.append(st_s)
    y_prompt = rmsnorm(xp, final_norm_w)[:, N_META:]
    y_sample = rmsnorm(xs, final_norm_w)
    pool_p = jnp.stack([st[0] for st in new_p])
    conv_p = jnp.stack([st[1] for st in new_p])
    delta_p = jnp.stack([st[2] for st in new_p])
    k_p = jnp.stack([st[3] for st in new_p])
    v_p = jnp.stack([st[4] for st in new_p])
    pool_s = jnp.stack([st[0] for st in new_s])
    conv_s = jnp.stack([st[1] for st in new_s])
    delta_s = jnp.stack([st[2] for st in new_s])
    k_s = jnp.stack([st[3] for st in new_s])
    v_s = jnp.stack([st[4] for st in new_s])
    return (y_prompt, y_sample, pool_p, conv_p, delta_p, k_p, v_p, pool_s, conv_s, delta_s, k_s, v_s)
```

```python
import functools
import math

import jax
import jax.numpy as jnp
from jax import lax
from jax.experimental import pallas as pl
from jax.experimental.pallas import tpu as pltpu

F32 = jnp.float32
BF16 = jnp.bfloat16

D_MODEL = 1024
N_META = 16
EPS = 1e-6
POOL_GROUPS = 4
POOL_GROUP_DIM = 128
POOL_WIDTH = 512
POOL_WINDOWS = (2, 4, 8, 16)
POOL_BUF = 15
POOL_HALO = 16
DN_HEADS = 4
DN_DK = 128
DN_DV = 128
DN_QK = 512
DN_VW = 512
CONV_W = 4
CONV_CH = 1536
CONV_HALO = 8
DN_CHUNK = 64
SWA_HEADS = 8
SWA_KV_HEADS = 2
SWA_GROUP = 4
SWA_HD = 64
SWA_WIDTH = 512
SWA_KV_WIDTH = 128
SWA_WINDOW = 128
ROPE_THETA = 10000.0
D_FF = 4096
PAST_LEN = 16384

G_OFF = 0
QKV_OFF = 3072
U_OFF = 4608
Z_OFF = 5120
QC_OFF = 5632
KVC_OFF = 6144
BA_OFF = 6400
H_WIDTH = 6656
IN_TN = 1664

VMEM_LIMIT = 56 * 1024 * 1024
NEG_BIG = -1e30


def _cparams(sem):
    return pltpu.CompilerParams(dimension_semantics=sem, vmem_limit_bytes=VMEM_LIMIT)


def _rms(x, w):
    return x * lax.rsqrt(jnp.mean(x * x, axis=-1, keepdims=True) + EPS) * w


def _inproj_kernel(x_ref, nw_ref, w_ref, o_ref, xn_ref):
    @pl.when(pl.program_id(1) == 0)
    def _():
        xn_ref[...] = _rms(x_ref[...], nw_ref[...]).astype(BF16)

    o_ref[...] = jnp.dot(xn_ref[...], w_ref[...], preferred_element_type=F32)


def _inproj(x, nw, w):
    m = x.shape[0]
    tm = min(m, 1024)
    assert m % tm == 0
    return pl.pallas_call(
        _inproj_kernel,
        grid=(m // tm, H_WIDTH // IN_TN),
        in_specs=[
            pl.BlockSpec((tm, D_MODEL), lambda i, j: (i, 0)),
            pl.BlockSpec((1, D_MODEL), lambda i, j: (0, 0)),
            pl.BlockSpec((D_MODEL, IN_TN), lambda i, j: (0, j)),
        ],
        out_specs=pl.BlockSpec((tm, IN_TN), lambda i, j: (i, j)),
        out_shape=jax.ShapeDtypeStruct((m, H_WIDTH), F32),
        scratch_shapes=[pltpu.VMEM((tm, D_MODEL), BF16)],
        compiler_params=_cparams(("parallel", "arbitrary")),
        name="inproj",
    )(x, nw, w)


def _pool_kernel(u_ref, init_ref, pw_ref, ps_ref, o_ref, ext_ref, *, pos0, tt):
    t = pl.program_id(1)
    bb = u_ref.shape[0]

    @pl.when(t == 0)
    def _():
        ext_ref[:, 0:POOL_HALO, :] = init_ref[...]

    @pl.when(t > 0)
    def _():
        ext_ref[:, 0:POOL_HALO, :] = ext_ref[:, tt:tt + POOL_HALO, :]

    ext_ref[:, POOL_HALO:POOL_HALO + tt, :] = u_ref[...]

    pos = pos0 + t * tt + lax.broadcasted_iota(jnp.int32, (1, tt, 1), 1)
    for g, w in enumerate(POOL_WINDOWS):
        cs = slice(g * POOL_GROUP_DIM, (g + 1) * POOL_GROUP_DIM)
        x = ext_ref[:, POOL_HALO:POOL_HALO + tt, cs]
        s = x
        for k in range(1, w):
            s = s + ext_ref[:, POOL_HALO - k:POOL_HALO - k + tt, cs]
        cnt = jnp.minimum(w, pos + 1).astype(F32)
        d = (s / cnt - x).reshape(bb * tt, POOL_GROUP_DIM)
        y = jnp.dot(d.astype(BF16), pw_ref[g], preferred_element_type=F32)
        o_ref[:, :, cs] = (y * ps_ref[:, cs]).reshape(bb, tt, POOL_GROUP_DIM)


def _pool(h3, init, pw, ps, pos0, bb, tt):
    b, t, _ = h3.shape
    return pl.pallas_call(
        functools.partial(_pool_kernel, pos0=pos0, tt=tt),
        grid=(b // bb, t // tt),
        in_specs=[
            pl.BlockSpec((bb, tt, POOL_WIDTH), lambda i, j: (i, j, U_OFF // POOL_WIDTH)),
            pl.BlockSpec((bb, POOL_HALO, POOL_WIDTH), lambda i, j: (i, 0, 0)),
            pl.BlockSpec((POOL_GROUPS, POOL_GROUP_DIM, POOL_GROUP_DIM), lambda i, j: (0, 0, 0)),
            pl.BlockSpec((1, POOL_WIDTH), lambda i, j: (0, 0)),
        ],
        out_specs=pl.BlockSpec((bb, tt, POOL_WIDTH), lambda i, j: (i, j, 0)),
        out_shape=jax.ShapeDtypeStruct((b, t, POOL_WIDTH), F32),
        scratch_shapes=[pltpu.VMEM((bb, POOL_HALO + tt, POOL_WIDTH), F32)],
        compiler_params=_cparams(("parallel", "arbitrary")),
        name="pool",
    )(h3, init, pw, ps)


def _softplus(x):
    return jnp.maximum(x, 0.0) + jnp.log1p(jnp.exp(-jnp.abs(x)))


def _dot_nt(a, b, **kw):
    return lax.dot_general(a, b, (((1,), (1,)), ((), ())), preferred_element_type=F32, **kw)


def _dot_tn(a, b):
    return lax.dot_general(a, b, (((0,), (0,)), ((), ())), preferred_element_type=F32)


def _delta_kernel(qkv_ref, z_ref, ba_ref, cinit_ref, s0_ref, cw_ref, alog_ref, dtb_ref, onw_ref,
                  o_ref, sfin_ref, ext_ref, s_ref, *, rows, chunk):
    t = pl.program_id(1)
    nchunks = rows // chunk
    mm = BF16 if chunk >= 16 else F32

    @pl.when(t == 0)
    def _():
        ext_ref[0:CONV_HALO, :] = cinit_ref[0]
        s_ref[...] = s0_ref[0]

    @pl.when(t > 0)
    def _():
        ext_ref[0:CONV_HALO, :] = ext_ref[rows:rows + CONV_HALO, :]

    ext_ref[CONV_HALO:CONV_HALO + rows, :] = qkv_ref[0]

    base = CONV_HALO - (CONV_W - 1)
    y = ext_ref[base:base + rows, :] * cw_ref[0:1, :]
    for j in range(1, CONV_W):
        y = y + ext_ref[base + j:base + j + rows, :] * cw_ref[j:j + 1, :]
    y = y * jax.nn.sigmoid(y)

    bav = ba_ref[0]
    beta_full = jax.nn.sigmoid(bav)
    g_full = -jnp.exp(alog_ref[...]) * _softplus(bav + dtb_ref[...])

    row = lax.broadcasted_iota(jnp.int32, (rows, rows), 0)
    col = lax.broadcasted_iota(jnp.int32, (rows, rows), 1)
    shift = int(math.log2(chunk))
    same = (row >> shift) == (col >> shift)
    tri = same & (row >= col)
    strict = same & (row > col)
    gcum = jnp.dot(tri.astype(F32), g_full, preferred_element_type=F32, precision=lax.Precision.HIGHEST)
    lane = lax.broadcasted_iota(jnp.int32, (rows, 128), 1)
    ones = jnp.ones((rows, 128), F32)
    zv = z_ref[0]

    for h in range(DN_HEADS):
        hs = slice(h * DN_DK, (h + 1) * DN_DK)
        g_only = jnp.where(lane == DN_HEADS + h, gcum, 0.0)
        gcol = jnp.sum(g_only, axis=1, keepdims=True)
        grow = _dot_nt(ones, g_only, precision=lax.Precision.HIGHEST)
        bcol = jnp.sum(jnp.where(lane == h, beta_full, 0.0), axis=1, keepdims=True)
        decay = jnp.where(tri, jnp.exp(jnp.where(tri, gcol - grow, 0.0)), 0.0)

        qh = y[:, hs]
        kh = y[:, DN_QK + h * DN_DK:DN_QK + (h + 1) * DN_DK]
        vh = y[:, 2 * DN_QK + h * DN_DV:2 * DN_QK + (h + 1) * DN_DV]
        qh = qh * lax.rsqrt(jnp.sum(qh * qh, axis=-1, keepdims=True) + EPS) * (DN_DK ** -0.5)
        kh = kh * lax.rsqrt(jnp.sum(kh * kh, axis=-1, keepdims=True) + EPS)
        kb = kh.astype(mm)
        kk = _dot_nt(kb, kb)
        qk = _dot_nt(qh.astype(mm), kb)

        x = jnp.where(strict, -(kk * decay * bcol), 0.0)
        p = x
        q = x
        for _ in range(shift - 1):
            qb = q.astype(mm)
            q = jnp.dot(qb, qb, preferred_element_type=F32)
            p = p + q + jnp.dot(p.astype(mm), q.astype(mm), preferred_element_type=F32)

        eg = jnp.exp(gcol)
        rhs = jnp.concatenate([vh * bcol, kh * (bcol * eg)], axis=1)
        uw = rhs + jnp.dot(p.astype(mm), rhs.astype(mm), preferred_element_type=F32)
        u = uw[:, :DN_DV]
        w = uw[:, DN_DV:]
        qkd = qk * decay
        qg = qh * eg

        outs = []
        s = s_ref[h]
        for c in range(nchunks):
            rs = slice(c * chunk, (c + 1) * chunk)
            glast = gcol[(c + 1) * chunk - 1:(c + 1) * chunk, :]
            sb = s.astype(mm)
            dlt = u[rs] - jnp.dot(w[rs].astype(mm), sb, preferred_element_type=F32)
            db = dlt.astype(mm)
            o = (jnp.dot(qg[rs].astype(mm), sb, preferred_element_type=F32)
                 + jnp.dot(qkd[rs, rs].astype(mm), db, preferred_element_type=F32))
            kg = kh[rs] * jnp.exp(glast - gcol[rs])
            s = s * jnp.exp(glast) + _dot_tn(kg.astype(mm), db)
            outs.append(o)
        s_ref[h] = s
        o = outs[0] if nchunks == 1 else jnp.concatenate(outs, axis=0)
        zh = zv[:, hs]
        o_ref[0, :, hs] = _rms(o, onw_ref[...]) * (zh * jax.nn.sigmoid(zh))

    sfin_ref[0] = s_ref[...]


def _delta(h3, cinit, s0, cw, alog, dtb, onw, rows, chunk):
    b, t, _ = h3.shape
    return pl.pallas_call(
        functools.partial(_delta_kernel, rows=rows, chunk=chunk),
        grid=(b, t // rows),
        in_specs=[
            pl.BlockSpec((1, rows, CONV_CH), lambda i, j: (i, j, QKV_OFF // CONV_CH)),
            pl.BlockSpec((1, rows, DN_VW), lambda i, j: (i, j, Z_OFF // DN_VW)),
            pl.BlockSpec((1, rows, 128), lambda i, j: (i, j, BA_OFF // 128)),
            pl.BlockSpec((1, CONV_HALO, CONV_CH), lambda i, j: (i, 0, 0)),
            pl.BlockSpec((1, DN_HEADS, DN_DK, DN_DV), lambda i, j: (i, 0, 0, 0)),
            pl.BlockSpec((CONV_W, CONV_CH), lambda i, j: (0, 0)),
            pl.BlockSpec((1, 128), lambda i, j: (0, 0)),
            pl.BlockSpec((1, 128), lambda i, j: (0, 0)),
            pl.BlockSpec((1, DN_DV), lambda i, j: (0, 0)),
        ],
        out_specs=[
            pl.BlockSpec((1, rows, DN_VW), lambda i, j: (i, j, 0)),
            pl.BlockSpec((1, DN_HEADS, DN_DK, DN_DV), lambda i, j: (i, 0, 0, 0)),
        ],
        out_shape=[
            jax.ShapeDtypeStruct((b, t, DN_VW), F32),
            jax.ShapeDtypeStruct((b, DN_HEADS, DN_DK, DN_DV), F32),
        ],
        scratch_shapes=[
            pltpu.VMEM((CONV_HALO + rows, CONV_CH), F32),
            pltpu.VMEM((DN_HEADS, DN_DK, DN_DV), F32),
        ],
        compiler_params=_cparams(("parallel", "arbitrary")),
        name="delta",
    )(h3, h3, h3, cinit, s0, cw, alog, dtb, onw)


def _rope(x, cos, sin):
    width = x.shape[-1]
    reps = width // cos.shape[-1]
    if reps > 1:
        cos = jnp.concatenate([cos] * reps, axis=1)
        sin = jnp.concatenate([sin] * reps, axis=1)
    lane = lax.broadcasted_iota(jnp.int32, x.shape, 1)
    first_half = (lane & (SWA_HD - 1)) < (SWA_HD // 2)
    other = jnp.where(first_half, pltpu.roll(x, width - SWA_HD // 2, 1), pltpu.roll(x, SWA_HD // 2, 1))
    return x * cos + other * sin


def _attend(q, blocks, sinks_ref, o_ref):
    for h in range(SWA_HEADS):
        g = h // SWA_GROUP
        gs = slice(g * SWA_HD, (g + 1) * SWA_HD)
        sink = sinks_ref[h]
        qh = q[:, h * SWA_HD:(h + 1) * SWA_HD].astype(BF16)
        scores = []
        m = None
        for k, _, mask in blocks:
            s = jnp.where(mask, _dot_nt(qh, k[:, gs].astype(BF16)), NEG_BIG)
            scores.append(s)
            bm = jnp.max(s, axis=1, keepdims=True)
            m = bm if m is None else jnp.maximum(m, bm)
        m = jnp.maximum(m, sink)
        den = jnp.exp(sink - m)
        acc = None
        for s, (_, v, _) in zip(scores, blocks):
            e = jnp.exp(s - m)
            den = den + jnp.sum(e, axis=1, keepdims=True)
            pv = jnp.dot(e.astype(BF16), v[:, gs].astype(BF16), preferred_element_type=F32)
            acc = pv if acc is None else acc + pv
        o_ref[0, :, h * SWA_HD:(h + 1) * SWA_HD] = acc / den


def _swa_prompt_kernel(sinks_ref, q_ref, kv_ref, kvp_ref, meta_ref, cos_ref, sin_ref, cosp_ref, sinp_ref,
                       o_ref, klast_ref, *, pos0):
    n = pl.program_id(1)
    blk = SWA_WINDOW
    cos = cos_ref[...]
    sin = sin_ref[...]
    q = _rope(q_ref[0], cos, sin) * (SWA_HD ** -0.5)
    kv = kv_ref[0]
    k = _rope(kv[:, :SWA_KV_WIDTH], cos, sin)
    v = kv[:, SWA_KV_WIDTH:]
    klast_ref[0] = k
    kvp = kvp_ref[0]
    kp = _rope(kvp[:, :SWA_KV_WIDTH], cosp_ref[...], sinp_ref[...])
    vp = kvp[:, SWA_KV_WIDTH:]
    first = n == 0
    kp = jnp.where(first, meta_ref[:, :SWA_KV_WIDTH], kp)
    vp = jnp.where(first, meta_ref[:, SWA_KV_WIDTH:], vp)
    r = lax.broadcasted_iota(jnp.int32, (blk, blk), 0)
    j = lax.broadcasted_iota(jnp.int32, (blk, blk), 1)
    kpos_prev = pos0 + (n - 1) * blk + j
    mask_prev = (j > r) & (kpos_prev >= 0)
    mask_own = j <= r
    _attend(q, [(kp, vp, mask_prev), (k, v, mask_own)], sinks_ref, o_ref)


def _swa_prompt(h3, meta_kv, cos, sin, sinks, pos0):
    b, t, _ = h3.shape
    blk = SWA_WINDOW
    prev = lambda i, n: (i, jnp.maximum(n - 1, 0), KVC_OFF // 256)
    return pl.pallas_call(
        functools.partial(_swa_prompt_kernel, pos0=pos0),
        grid=(b, t // blk),
        in_specs=[
            pl.BlockSpec(memory_space=pltpu.SMEM),
            pl.BlockSpec((1, blk, SWA_WIDTH), lambda i, n: (i, n, QC_OFF // SWA_WIDTH)),
            pl.BlockSpec((1, blk, 256), lambda i, n: (i, n, KVC_OFF // 256)),
            pl.BlockSpec((1, blk, 256), prev),
            pl.BlockSpec((blk, 256), lambda i, n: (0, 0)),
            pl.BlockSpec((blk, 128), lambda i, n: (n, 0)),
            pl.BlockSpec((blk, 128), lambda i, n: (n, 0)),
            pl.BlockSpec((blk, 128), lambda i, n: (jnp.maximum(n - 1, 0), 0)),
            pl.BlockSpec((blk, 128), lambda i, n: (jnp.maximum(n - 1, 0), 0)),
        ],
        out_specs=[
            pl.BlockSpec((1, blk, SWA_WIDTH), lambda i, n: (i, n, 0)),
            pl.BlockSpec((1, blk, SWA_KV_WIDTH), lambda i, n: (i, 0, 0)),
        ],
        out_shape=[
            jax.ShapeDtypeStruct((b, t, SWA_WIDTH), F32),
            jax.ShapeDtypeStruct((b, blk, SWA_KV_WIDTH), F32),
        ],
        compiler_params=_cparams(("parallel", "arbitrary")),
        name="swa_prompt",
    )(sinks, h3, h3, h3, meta_kv, cos, sin, cos, sin)


def _swa_step_kernel(sinks_ref, q_ref, kv_ref, *rest, t_new, window):
    if window:
        kc_ref, vc_ref, cos_ref, sin_ref, o_ref, knew_ref = rest
    else:
        cos_ref, sin_ref, o_ref, knew_ref = rest
    cos = cos_ref[...]
    sin = sin_ref[...]
    q = _rope(q_ref[0], cos, sin) * (SWA_HD ** -0.5)
    kv = kv_ref[0]
    k = _rope(kv[:, :SWA_KV_WIDTH], cos, sin)
    v = kv[:, SWA_KV_WIDTH:]
    knew_ref[0] = k
    r = lax.broadcasted_iota(jnp.int32, (t_new, t_new), 0)
    j = lax.broadcasted_iota(jnp.int32, (t_new, t_new), 1)
    blocks = [(k, v, j <= r)]
    if window:
        rc = lax.broadcasted_iota(jnp.int32, (t_new, window), 0)
        jc = lax.broadcasted_iota(jnp.int32, (t_new, window), 1)
        diff = rc + window - jc
        blocks.insert(0, (kc_ref[0], vc_ref[0], (diff >= 0) & (diff < SWA_WINDOW)))
    _attend(q, blocks, sinks_ref, o_ref)


def _swa_step(h3, kc, vc, cos, sin, sinks):
    b, t, _ = h3.shape
    window = 0 if kc is None else kc.shape[1]
    in_specs = [
        pl.BlockSpec(memory_space=pltpu.SMEM),
        pl.BlockSpec((1, t, SWA_WIDTH), lambda i: (i, 0, QC_OFF // SWA_WIDTH)),
        pl.BlockSpec((1, t, 256), lambda i: (i, 0, KVC_OFF // 256)),
    ]
    args = [sinks, h3, h3]
    if window:
        in_specs += [pl.BlockSpec((1, window, SWA_KV_WIDTH), lambda i: (i, 0, 0))] * 2
        args += [kc, vc]
    in_specs += [pl.BlockSpec((t, 128), lambda i: (0, 0))] * 2
    args += [cos, sin]
    return pl.pallas_call(
        functools.partial(_swa_step_kernel, t_new=t, window=window),
        grid=(b,),
        in_specs=in_specs,
        out_specs=[
            pl.BlockSpec((1, t, SWA_WIDTH), lambda i: (i, 0, 0)),
            pl.BlockSpec((1, t, SWA_KV_WIDTH), lambda i: (i, 0, 0)),
        ],
        out_shape=[
            jax.ShapeDtypeStruct((b, t, SWA_WIDTH), F32),
            jax.ShapeDtypeStruct((b, t, SWA_KV_WIDTH), F32),
        ],
        compiler_params=_cparams(("parallel",)),
        name="swa_step",
    )(*args)


def _merge_kernel(oa_ref, ob_ref, oc_ref, ga_ref, gb_ref, gc_ref, x_ref, pa_ref, pb_ref, pc_ref, wo_ref, o_ref):
    def branch(o_r, g_r, p_r):
        return jax.nn.sigmoid(g_r[...]) * jnp.dot(o_r[...].astype(BF16), p_r[...], preferred_element_type=F32)

    m = branch(oa_ref, ga_ref, pa_ref) + branch(ob_ref, gb_ref, pb_ref) + branch(oc_ref, gc_ref, pc_ref)
    o_ref[...] = x_ref[...] + jnp.dot(m.astype(BF16), wo_ref[...], preferred_element_type=F32)


def _merge(oa, ob, oc, h, x, pa, pb, pc, wo):
    m = x.shape[0]
    tm = min(m, 512)
    assert m % tm == 0
    row =lambda w: pl.BlockSpec((tm, w), lambda i: (i, 0))
    gate = lambda c: pl.BlockSpec((tm, D_MODEL), lambda i: (i, c))
    full = lambda a: pl.BlockSpec(a.shape, lambda i: (0, 0))
    return pl.pallas_call(
        _merge_kernel,
        grid=(m // tm,),
        in_specs=[row(512), row(512), row(512), gate(0), gate(1), gate(2), row(D_MODEL),
                  full(pa), full(pb), full(pc), full(wo)],
        out_specs=row(D_MODEL),
        out_shape=jax.ShapeDtypeStruct((m, D_MODEL), F32),
        compiler_params=_cparams(("parallel",)),
        name="merge",
    )(oa, ob, oc, h, h, h, x, pa, pb, pc, wo)


def _mlp_kernel(x_ref, nw_ref, wu_ref, wd_ref, fnw_ref, o_ref, xn_ref, acc_ref, *, final):
    f = pl.program_id(1)

    @pl.when(f == 0)
    def _():
        x = x_ref[...]
        xn_ref[...] = _rms(x, nw_ref[...]).astype(BF16)
        acc_ref[...] = x

    a = jnp.maximum(jnp.dot(xn_ref[...], wu_ref[...], preferred_element_type=F32), 0.0)
    acc_ref[...] += jnp.dot((a * a).astype(BF16), wd_ref[...], preferred_element_type=F32)

    @pl.when(f == pl.num_programs(1) - 1)
    def _():
        y = acc_ref[...]
        o_ref[...] = _rms(y, fnw_ref[...]) if final else y


def _mlp(x, nw, wu, wd, fnw, final):
    m = x.shape[0]
    tm = min(m, 1024)
    assert m % tm == 0
    tf = 1024
    return pl.pallas_call(
        functools.partial(_mlp_kernel, final=final),
        grid=(m // tm, D_FF // tf),
        in_specs=[
            pl.BlockSpec((tm, D_MODEL), lambda i, f: (i, 0)),
            pl.BlockSpec((1, D_MODEL), lambda i, f: (0, 0)),
            pl.BlockSpec((D_MODEL, tf), lambda i, f: (0, f)),
            pl.BlockSpec((tf, D_MODEL), lambda i, f: (f, 0)),
            pl.BlockSpec((1, D_MODEL), lambda i, f: (0, 0)),
        ],
        out_specs=pl.BlockSpec((tm, D_MODEL), lambda i, f: (i, 0)),
        out_shape=jax.ShapeDtypeStruct((m, D_MODEL), F32),
        scratch_shapes=[pltpu.VMEM((tm, D_MODEL), BF16), pltpu.VMEM((tm, D_MODEL), F32)],
        compiler_params=_cparams(("parallel", "arbitrary")),
        name="mlp",
    )(x, nw, wu, wd, fnw)


def _rope_tables(pos0, t):
    half = SWA_HD // 2
    inv = ROPE_THETA ** (-jnp.arange(half, dtype=F32) / half)
    ang = (pos0 + jnp.arange(t)).astype(F32)[:, None] * inv[None, :]
    cos = jnp.cos(ang)
    sin = jnp.sin(ang)
    return jnp.tile(cos, (1, 4)), jnp.tile(jnp.concatenate([-sin, sin], axis=1), (1, 2))


def _prep_layer(l, norm1_w, w_in, pool_w, pool_scale, dn_conv_w, dn_a_log, dn_dt_bias, dn_onorm_w, swa_sinks,
                proj_a, proj_b, proj_c, w_out, norm2_w, w_up, w_down):
    w = w_in[l]
    w_perm = jnp.concatenate(
        [w[:, 3336:6408], w[:, 512:2048], w[:, 0:512], w[:, 2048:2560], w[:, 2568:3080], w[:, 3080:3336],
         w[:, 2560:2568], jnp.zeros((D_MODEL, H_WIDTH - BA_OFF - 2 * DN_HEADS), w.dtype)], axis=1).astype(BF16)
    lane_pad = lambda v: jnp.zeros((1, 128), F32).at[0, DN_HEADS:2 * DN_HEADS].set(v.astype(F32))
    return dict(
        norm1=norm1_w[l][None].astype(F32), w_in=w_perm,
        pool_w=pool_w[l].astype(BF16), pool_scale=pool_scale[l][None].astype(F32),
        conv_w=dn_conv_w[l].astype(F32), alog=lane_pad(dn_a_log[l]), dtb=lane_pad(dn_dt_bias[l]),
        onw=dn_onorm_w[l][None].astype(F32), sinks=swa_sinks[l].astype(F32),
        pa=proj_a[l].astype(BF16), pb=proj_b[l].astype(BF16), pc=proj_c[l].astype(BF16),
        wo=w_out[l].astype(BF16), norm2=norm2_w[l][None].astype(F32),
        wu=w_up[l].astype(BF16), wd=w_down[l].astype(BF16))


def _group_step(x3, p, fnw, final, pos0, pool_init, conv_init, s0, kind, swa_extra, pool_tiles, delta_tiles):
    b, t, _ = x3.shape
    x2 = x3.reshape(b * t, D_MODEL)
    h = _inproj(x2, p['norm1'], p['w_in'])
    h3 = h.reshape(b, t, H_WIDTH)
    o_a = _pool(h3, pool_init, p['pool_w'], p['pool_scale'], pos0, *pool_tiles)
    o_b, s_new = _delta(h3, conv_init, s0, p['conv_w'], p['alog'], p['dtb'], p['onw'], *delta_tiles)
    cos, sin = _rope_tables(pos0, t)
    if kind == 'prompt':
        o_c, k_rot = _swa_prompt(h3, swa_extra, cos, sin, p['sinks'], pos0)
    else:
        kc, vc = swa_extra
        o_c, k_rot = _swa_step(h3, kc, vc, cos, sin, p['sinks'])
    m = b * t
    h1 = _merge(o_a.reshape(m, 512), o_b.reshape(m, 512), o_c.reshape(m, 512), h, x2,
                p['pa'], p['pb'], p['pc'], p['wo'])
    out = _mlp(h1, p['norm2'], p['wu'], p['wd'], fnw, final)
    return out.reshape(b, t, D_MODEL), h3, s_new, k_rot


def kernel(x_prompt, x_sample, state_pool, state_conv, state_delta, cache_swa_k, cache_swa_v, meta_tokens, norm1_w, w_in, pool_w, pool_scale, dn_conv_w, dn_a_log, dn_dt_bias, dn_onorm_w, swa_sinks, proj_a, proj_b, proj_c, w_out, norm2_w, w_up, w_down, final_norm_w):
    depth = w_in.shape[0]
    bp, tp, _ = x_prompt.shape
    bs, ts, _ = x_sample.shape
    past = cache_swa_k.shape[2]
    fnw = final_norm_w[None].astype(F32)

    xm = meta_tokens[None].astype(F32)
    xp = x_prompt
    xs = x_sample
    outs = {k: [] for k in ('pool_p', 'conv_p', 'delta_p', 'k_p', 'v_p', 'pool_s', 'conv_s', 'delta_s', 'k_s', 'v_s')}
    for l in range(depth):
        p = _prep_layer(l, norm1_w, w_in, pool_w, pool_scale, dn_conv_w, dn_a_log, dn_dt_bias, dn_onorm_w,
                        swa_sinks, proj_a, proj_b, proj_c, w_out, norm2_w, w_up, w_down)
        final = l == depth - 1

        xm, hm, s_m, k_m = _group_step(
            xm, p, fnw, final, 0,
            jnp.zeros((1, POOL_HALO, POOL_WIDTH), F32), jnp.zeros((1, CONV_HALO, CONV_CH), F32),
            jnp.zeros((1, DN_HEADS, DN_DK, DN_DV), F32), 'step', (None, None),
            (1, N_META), (N_META, N_META))

        meta_kv = jnp.zeros((SWA_WINDOW, 256), F32)
        meta_kv = meta_kv.at[SWA_WINDOW - N_META:, :SWA_KV_WIDTH].set(k_m[0])
        meta_kv = meta_kv.at[SWA_WINDOW - N_META:, SWA_KV_WIDTH:].set(hm[0, :, KVC_OFF + SWA_KV_WIDTH:KVC_OFF + 256])
        xp, hp, s_p, k_p = _group_step(
            xp, p, fnw, final, N_META,
            jnp.broadcast_to(hm[:, :, U_OFF:U_OFF + POOL_WIDTH], (bp, POOL_HALO, POOL_WIDTH)),
            jnp.broadcast_to(hm[:, N_META - CONV_HALO:, QKV_OFF:QKV_OFF + CONV_CH], (bp, CONV_HALO, CONV_CH)),
            jnp.broadcast_to(s_m, (bp, DN_HEADS, DN_DK, DN_DV)), 'prompt', meta_kv,
            (1, 512), (2 * DN_CHUNK, DN_CHUNK))
        outs['pool_p'].append(hp[:, tp - POOL_BUF:, U_OFF:U_OFF + POOL_WIDTH])
        outs['conv_p'].append(hp[:, tp - (CONV_W - 1):, QKV_OFF:QKV_OFF + CONV_CH])
        outs['delta_p'].append(s_p)
        outs['k_p'].append(k_p.reshape(bp, SWA_WINDOW, SWA_KV_HEADS, SWA_HD))
        outs['v_p'].append(hp[:, tp - SWA_WINDOW:, KVC_OFF + SWA_KV_WIDTH:KVC_OFF + 256]
                           .reshape(bp, SWA_WINDOW, SWA_KV_HEADS, SWA_HD))

        kc = cache_swa_k[l].reshape(bs, past, SWA_KV_WIDTH)
        vc = cache_swa_v[l].reshape(bs, past, SWA_KV_WIDTH)
        xs, hs, s_s, k_s = _group_step(
            xs, p, fnw, final, PAST_LEN,
            jnp.concatenate([jnp.zeros((bs, POOL_HALO - POOL_BUF, POOL_WIDTH), F32), state_pool[l]], axis=1),
            jnp.concatenate([jnp.zeros((bs, CONV_HALO - (CONV_W - 1), CONV_CH), F32), state_conv[l]], axis=1),
            state_delta[l], 'step', (kc, vc),
            (bs, ts), (ts, ts))
        outs['pool_s'].append(jnp.concatenate([state_pool[l], hs[:, :, U_OFF:U_OFF + POOL_WIDTH]], axis=1)[:, -POOL_BUF:])
        outs['conv_s'].append(jnp.concatenate([state_conv[l], hs[:, :, QKV_OFF:QKV_OFF + CONV_CH]], axis=1)[:, -(CONV_W - 1):])
        outs['delta_s'].append(s_s)
        v_new = hs[:, :, KVC_OFF + SWA_KV_WIDTH:KVC_OFF + 256]
        outs['k_s'].append(jnp.concatenate([kc, k_s], axis=1)[:, -past:].reshape(bs, past, SWA_KV_HEADS, SWA_HD))
        outs['v_s'].append(jnp.concatenate([vc, v_new], axis=1)[:, -past:].reshape(bs, past, SWA_KV_HEADS, SWA_HD))

    st = {k: jnp.stack(v) for k, v in outs.items()}
    return (xp, xs, st['pool_p'], st['conv_p'], st['delta_p'], st['k_p'], st['v_p'],
            st['pool_s'], st['conv_s'], st['delta_s'], st['k_s'], st['v_s'])
```

```python
import functools
import math

import jax
import jax.numpy as jnp
from jax import lax
from jax.experimental import pallas as pl
from jax.experimental.pallas import tpu as pltpu

F32 = jnp.float32
BF16 = jnp.bfloat16

D_MODEL = 1024
N_META = 16
EPS = 1e-6
POOL_GROUPS = 4
POOL_GROUP_DIM = 128
POOL_WIDTH = 512
POOL_WINDOWS = (2, 4, 8, 16)
POOL_BUF = 15
POOL_HALO = 16
DN_HEADS = 4
DN_DK = 128
DN_DV = 128
DN_QK = 512
DN_VW = 512
CONV_W = 4
CONV_CH = 1536
CONV_HALO = 8
DN_CHUNK = 64
SWA_HEADS = 8
SWA_KV_HEADS = 2
SWA_GROUP = 4
SWA_HD = 64
SWA_WIDTH = 512
SWA_KV_WIDTH = 128
SWA_WINDOW = 128
ROPE_THETA = 10000.0
D_FF = 4096
PAST_LEN = 16384

G_OFF = 0
QKV_OFF = 3072
U_OFF = 4608
Z_OFF = 5120
QC_OFF = 5632
KVC_OFF = 6144
BA_OFF = 6400
H_WIDTH = 6656
IN_TN = 1664

VMEM_LIMIT = 56 * 1024 * 1024
NEG_BIG = -1e30


def _cparams(sem):
    return pltpu.CompilerParams(dimension_semantics=sem, vmem_limit_bytes=VMEM_LIMIT)


def _rms(x, w):
    return x * lax.rsqrt(jnp.mean(x * x, axis=-1, keepdims=True) + EPS) * w


def _inproj_kernel(x_ref, nw_ref, w_ref, o_ref, xn_ref):
    @pl.when(pl.program_id(1) == 0)
    def _():
        xn_ref[...] = _rms(x_ref[...], nw_ref[...]).astype(BF16)

    o_ref[...] = jnp.dot(xn_ref[...], w_ref[...], preferred_element_type=F32)


def _inproj(x, nw, w):
    m = x.shape[0]
    tm = min(m, 1024)
    assert m % tm == 0
    return pl.pallas_call(
        _inproj_kernel,
        grid=(m // tm, H_WIDTH // IN_TN),
        in_specs=[
            pl.BlockSpec((tm, D_MODEL), lambda i, j: (i, 0)),
            pl.BlockSpec((1, D_MODEL), lambda i, j: (0, 0)),
            pl.BlockSpec((D_MODEL, IN_TN), lambda i, j: (0, j)),
        ],
        out_specs=pl.BlockSpec((tm, IN_TN), lambda i, j: (i, j)),
        out_shape=jax.ShapeDtypeStruct((m, H_WIDTH), F32),
        scratch_shapes=[pltpu.VMEM((tm, D_MODEL), BF16)],
        compiler_params=_cparams(("parallel", "arbitrary")),
        name="inproj",
    )(x, nw, w)


def _pool_kernel(u_ref, init_ref, pw_ref, ps_ref, o_ref, ext_ref, *, pos0, tt):
    t = pl.program_id(1)
    bb = u_ref.shape[0]

    @pl.when(t == 0)
    def _():
        ext_ref[:, 0:POOL_HALO, :] = init_ref[...]

    @pl.when(t > 0)
    def _():
        ext_ref[:, 0:POOL_HALO, :] = ext_ref[:, tt:tt + POOL_HALO, :]

    ext_ref[:, POOL_HALO:POOL_HALO + tt, :] = u_ref[...]

    pos = pos0 + t * tt + lax.broadcasted_iota(jnp.int32, (1, tt, 1), 1)
    for g, w in enumerate(POOL_WINDOWS):
        cs = slice(g * POOL_GROUP_DIM, (g + 1) * POOL_GROUP_DIM)
        x = ext_ref[:, POOL_HALO:POOL_HALO + tt, cs]
        s = x
        for k in range(1, w):
            s = s + ext_ref[:, POOL_HALO - k:POOL_HALO - k + tt, cs]
        cnt = jnp.minimum(w, pos + 1).astype(F32)
        d = (s / cnt - x).reshape(bb * tt, POOL_GROUP_DIM)
        y = jnp.dot(d.astype(BF16), pw_ref[g], preferred_element_type=F32)
        o_ref[:, :, cs] = (y * ps_ref[:, cs]).reshape(bb, tt, POOL_GROUP_DIM)


def _pool(h3, init, pw, ps, pos0, bb, tt):
    b, t, _ = h3.shape
    return pl.pallas_call(
        functools.partial(_pool_kernel, pos0=pos0, tt=tt),
        grid=(b // bb, t // tt),
        in_specs=[
            pl.BlockSpec((bb, tt, POOL_WIDTH), lambda i, j: (i, j, U_OFF // POOL_WIDTH)),
            pl.BlockSpec((bb, POOL_HALO, POOL_WIDTH), lambda i, j: (i, 0, 0)),
            pl.BlockSpec((POOL_GROUPS, POOL_GROUP_DIM, POOL_GROUP_DIM), lambda i, j: (0, 0, 0)),
            pl.BlockSpec((1, POOL_WIDTH), lambda i, j: (0, 0)),
        ],
        out_specs=pl.BlockSpec((bb, tt, POOL_WIDTH), lambda i, j: (i, j, 0)),
        out_shape=jax.ShapeDtypeStruct((b, t, POOL_WIDTH), F32),
        scratch_shapes=[pltpu.VMEM((bb, POOL_HALO + tt, POOL_WIDTH), F32)],
        compiler_params=_cparams(("parallel", "arbitrary")),
        name="pool",
    )(h3, init, pw, ps)


def _softplus(x):
    return jnp.maximum(x, 0.0) + jnp.log1p(jnp.exp(-jnp.abs(x)))


def _dot_nt(a, b, **kw):
    return lax.dot_general(a, b, (((1,), (1,)), ((), ())), preferred_element_type=F32, **kw)


def _dot_tn(a, b):
    return lax.dot_general(a, b, (((0,), (0,)), ((), ())), preferred_element_type=F32)


def _dot(a, b):
    return jnp.dot(a, b, preferred_element_type=F32)


def _delta_kernel(qkv_ref, z_ref, ba_ref, cinit_ref, s0_ref, cw_ref, alog_ref, dtb_ref, onw_ref,
                  o_ref, s_ref, ext_ref, *, bb, tb, chunk):
    t = pl.program_id(1)
    rows = bb * tb
    cpb = tb // chunk
    heads = range(DN_HEADS)
    mm = BF16 if rows >= 16 else F32
    mc = BF16 if chunk >= 16 else F32

    @pl.when(t == 0)
    def _():
        ext_ref[:, 0:CONV_HALO, :] = cinit_ref[...]
        s_ref[...] = s0_ref[...]

    @pl.when(t > 0)
    def _():
        ext_ref[:, 0:CONV_HALO, :] = ext_ref[:, tb:tb + CONV_HALO, :]

    ext_ref[:, CONV_HALO:CONV_HALO + tb, :] = qkv_ref[...]

    base = CONV_HALO - (CONV_W - 1)
    y = ext_ref[:, base:base + tb, :] * cw_ref[0:1, :]
    for j in range(1, CONV_W):
        y = y + ext_ref[:, base + j:base + j + tb, :] * cw_ref[j:j + 1, :]
    y = (y * jax.nn.sigmoid(y)).reshape(rows, CONV_CH)

    bav = ba_ref[...].reshape(rows, 128)
    zv = z_ref[...].reshape(rows, DN_VW)
    beta_full = jax.nn.sigmoid(bav)
    g_full = -jnp.exp(alog_ref[...]) * _softplus(bav + dtb_ref[...])

    row = lax.broadcasted_iota(jnp.int32, (rows, rows), 0)
    col = lax.broadcasted_iota(jnp.int32, (rows, rows), 1)
    shift = int(math.log2(chunk))
    same = (row >> shift) == (col >> shift)
    tri = same & (row >= col)
    strict = same & (row > col)

    gcum = jnp.dot(tri.astype(F32), g_full, preferred_element_type=F32, precision=lax.Precision.HIGHEST)

    lane = lax.broadcasted_iota(jnp.int32, (rows, 128), 1)
    gcol = [jnp.sum(jnp.where(lane == DN_HEADS + h, gcum, 0.0), axis=1, keepdims=True) for h in heads]
    bcol = [jnp.sum(jnp.where(lane == h, beta_full, 0.0), axis=1, keepdims=True) for h in heads]
    if rows % 128 == 0:
        gcum_t = gcum.T
        grow = [gcum_t[DN_HEADS + h:DN_HEADS + h + 1, :] for h in heads]
    else:
        ones = jnp.ones((rows, 128), F32)
        grow = [_dot_nt(ones, jnp.where(lane == DN_HEADS + h, gcum, 0.0), precision=lax.Precision.HIGHEST)
                for h in heads]
    decay = [jnp.where(tri, jnp.exp(jnp.where(tri, gcol[h] - grow[h], 0.0)), 0.0) for h in heads]
    eg = [jnp.exp(gcol[h]) for h in heads]

    def l2n(v):
        return v * lax.rsqrt(jnp.sum(v * v, axis=-1, keepdims=True) + EPS)

    qn = [l2n(y[:, h * DN_DK:(h + 1) * DN_DK]) * (DN_DK ** -0.5) for h in heads]
    kn = [l2n(y[:, DN_QK + h * DN_DK:DN_QK + (h + 1) * DN_DK]) for h in heads]
    vh = [y[:, 2 * DN_QK + h * DN_DV:2 * DN_QK + (h + 1) * DN_DV] for h in heads]
    kb = [kn[h].astype(mm) for h in heads]
    kk = [_dot_nt(kb[h], kb[h]) for h in heads]
    qk = [_dot_nt(qn[h].astype(mm), kb[h]) for h in heads]

    m = [jnp.where(strict, kk[h] * decay[h] * bcol[h], 0.0) for h in heads]
    bshift = min(shift, 4)
    p = [jnp.where((row >> bshift) == (col >> bshift), -m[h], 0.0) for h in heads]
    xp = p
    for _ in range(bshift - 1):
        xb = [xp[h].astype(mm) for h in heads]
        xp = [_dot(xb[h], xb[h]) for h in heads]
        p = [p[h] + xp[h] + _dot(p[h].astype(mm), xp[h].astype(mm)) for h in heads]
    for s in range(bshift, shift):
        lower = ((row >> (s + 1)) == (col >> (s + 1))) & ((row >> s) > (col >> s))
        c_blk = [jnp.where(lower, m[h], 0.0) for h in heads]
        pb = [p[h].astype(mm) for h in heads]
        a = [c_blk[h] + _dot(pb[h], c_blk[h].astype(mm)) for h in heads]
        p = [p[h] - a[h] - _dot(a[h].astype(mm), pb[h]) for h in heads]
    rhs = [jnp.concatenate([vh[h] * bcol[h], kn[h] * (bcol[h] * eg[h])], axis=1) for h in heads]
    uw = [rhs[h] + _dot(p[h].astype(mm), rhs[h].astype(mm)) for h in heads]
    qkd = [(qk[h] * decay[h]).astype(mm) for h in heads]
    qg = [qn[h] * eg[h] for h in heads]

    pairs = [(b, h) for b in range(bb) for h in heads]
    state = {bh: s_ref[bh[0], bh[1]] for bh in pairs}
    dlt = {h: [] for h in heads}
    oq = {h: [] for h in heads}
    for c in range(cpb):
        def rs(b):
            return slice(b * tb + c * chunk, b * tb + (c + 1) * chunk)
        prod = {}
        for b, h in pairs:
            lhs = jnp.concatenate([uw[h][rs(b), DN_DV:], qg[h][rs(b)]], axis=0).astype(mc)
            prod[b, h] = _dot(lhs, state[b, h].astype(mc))
        for b, h in pairs:
            r = rs(b)
            d = uw[h][r, :DN_DV] - prod[b, h][:chunk]
            glast = gcol[h][r.stop - 1:r.stop, :]
            kg = kn[h][r] * jnp.exp(glast - gcol[h][r])
            state[b, h] = state[b, h] * jnp.exp(glast) + _dot_tn(kg.astype(mc), d.astype(mc))
            dlt[h].append((r.start, d))
            oq[h].append((r.start, prod[b, h][chunk:]))
    for b, h in pairs:
        s_ref[b, h] = state[b, h]

    def stack(parts):
        parts = [v for _, v in sorted(parts, key=lambda sv: sv[0])]
        return parts[0] if len(parts) == 1 else jnp.concatenate(parts, axis=0)

    for h in heads:
        hs = slice(h * DN_DV, (h + 1) * DN_DV)
        o = stack(oq[h]) + _dot(qkd[h], stack(dlt[h]).astype(mm))
        zh = zv[:, hs]
        o_ref[:, :, hs] = (_rms(o, onw_ref[...]) * (zh * jax.nn.sigmoid(zh))).reshape(bb, tb, DN_DV)


def _delta(h3, cinit, s0, cw, alog, dtb, onw, bb, tb, chunk):
    b, t, _ = h3.shape
    assert b % bb == 0 and t % tb == 0 and tb % chunk == 0
    return pl.pallas_call(
        functools.partial(_delta_kernel, bb=bb, tb=tb, chunk=chunk),
        grid=(b // bb, t // tb),
        in_specs=[
            pl.BlockSpec((bb, tb, CONV_CH), lambda i, j: (i, j, QKV_OFF // CONV_CH)),
            pl.BlockSpec((bb, tb, DN_VW), lambda i, j: (i, j, Z_OFF // DN_VW)),
            pl.BlockSpec((bb, tb, 128), lambda i, j: (i, j, BA_OFF // 128)),
            pl.BlockSpec((bb, CONV_HALO, CONV_CH), lambda i, j: (i, 0, 0)),
            pl.BlockSpec((bb, DN_HEADS, DN_DK, DN_DV), lambda i, j: (i, 0, 0, 0)),
            pl.BlockSpec((CONV_W, CONV_CH), lambda i, j: (0, 0)),
            pl.BlockSpec((1, 128), lambda i, j: (0, 0)),
            pl.BlockSpec((1, 128), lambda i, j: (0, 0)),
            pl.BlockSpec((1, DN_DV), lambda i, j: (0, 0)),
        ],
        out_specs=[
            pl.BlockSpec((bb, tb, DN_VW), lambda i, j: (i, j, 0)),
            pl.BlockSpec((bb, DN_HEADS, DN_DK, DN_DV), lambda i, j: (i, 0, 0, 0)),
        ],
        out_shape=[
            jax.ShapeDtypeStruct((b, t, DN_VW), F32),
            jax.ShapeDtypeStruct((b, DN_HEADS, DN_DK, DN_DV), F32),
        ],
        scratch_shapes=[pltpu.VMEM((bb, CONV_HALO + tb, CONV_CH), F32)],
        compiler_params=_cparams(("parallel", "arbitrary")),
        name="delta",
    )(h3, h3, h3, cinit, s0, cw, alog, dtb, onw)


def _rope(x, cos, sin):
    width = x.shape[-1]
    reps = width // cos.shape[-1]
    if reps > 1:
        cos = jnp.concatenate([cos] * reps, axis=1)
        sin = jnp.concatenate([sin] * reps, axis=1)
    lane = lax.broadcasted_iota(jnp.int32, x.shape, 1)
    first_half = (lane & (SWA_HD - 1)) < (SWA_HD // 2)
    other = jnp.where(first_half, pltpu.roll(x, width - SWA_HD // 2, 1), pltpu.roll(x, SWA_HD // 2, 1))
    return x * cos + other * sin


def _attend(q, blocks, sinks_ref, o_ref):
    for h in range(SWA_HEADS):
        g = h // SWA_GROUP
        gs = slice(g * SWA_HD, (g + 1) * SWA_HD)
        sink = sinks_ref[h]
        qh = q[:, h * SWA_HD:(h + 1) * SWA_HD].astype(BF16)
        scores = []
        m = None
        for k, _, mask in blocks:
            s = jnp.where(mask, _dot_nt(qh, k[:, gs].astype(BF16)), NEG_BIG)
            scores.append(s)
            bm = jnp.max(s, axis=1, keepdims=True)
            m = bm if m is None else jnp.maximum(m, bm)
        m = jnp.maximum(m, sink)
        den = jnp.exp(sink - m)
        acc = None
        for s, (_, v, _) in zip(scores, blocks):
            e = jnp.exp(s - m)
            den = den + jnp.sum(e, axis=1, keepdims=True)
            pv = jnp.dot(e.astype(BF16), v[:, gs].astype(BF16), preferred_element_type=F32)
            acc = pv if acc is None else acc + pv
        o_ref[0, :, h * SWA_HD:(h + 1) * SWA_HD] = acc / den


def _swa_prompt_kernel(sinks_ref, q_ref, kv_ref, kvp_ref, meta_ref, cos_ref, sin_ref, cosp_ref, sinp_ref,
                       o_ref, klast_ref, *, pos0):
    n = pl.program_id(1)
    blk = SWA_WINDOW
    cos = cos_ref[...]
    sin = sin_ref[...]
    q = _rope(q_ref[0], cos, sin) * (SWA_HD ** -0.5)
    kv = kv_ref[0]
    k = _rope(kv[:, :SWA_KV_WIDTH], cos, sin)
    v = kv[:, SWA_KV_WIDTH:]
    klast_ref[0] = k
    kvp = kvp_ref[0]
    kp = _rope(kvp[:, :SWA_KV_WIDTH], cosp_ref[...], sinp_ref[...])
    vp = kvp[:, SWA_KV_WIDTH:]
    first = n == 0
    kp = jnp.where(first, meta_ref[:, :SWA_KV_WIDTH], kp)
    vp = jnp.where(first, meta_ref[:, SWA_KV_WIDTH:], vp)
    r = lax.broadcasted_iota(jnp.int32, (blk, blk), 0)
    j = lax.broadcasted_iota(jnp.int32, (blk, blk), 1)
    kpos_prev = pos0 + (n - 1) * blk + j
    mask_prev = (j > r) & (kpos_prev >= 0)
    mask_own = j <= r
    _attend(q, [(kp, vp, mask_prev), (k, v, mask_own)], sinks_ref, o_ref)


def _swa_prompt(h3, meta_kv, cos, sin, sinks, pos0):
    b, t, _ = h3.shape
    blk = SWA_WINDOW
    prev = lambda i, n: (i, jnp.maximum(n - 1, 0), KVC_OFF // 256)
    return pl.pallas_call(
        functools.partial(_swa_prompt_kernel, pos0=pos0),
        grid=(b, t // blk),
        in_specs=[
            pl.BlockSpec(memory_space=pltpu.SMEM),
            pl.BlockSpec((1, blk, SWA_WIDTH), lambda i, n: (i, n, QC_OFF // SWA_WIDTH)),
            pl.BlockSpec((1, blk, 256), lambda i, n: (i, n, KVC_OFF // 256)),
            pl.BlockSpec((1, blk, 256), prev),
            pl.BlockSpec((blk, 256), lambda i, n: (0, 0)),
            pl.BlockSpec((blk, 128), lambda i, n: (n, 0)),
            pl.BlockSpec((blk, 128), lambda i, n: (n, 0)),
            pl.BlockSpec((blk, 128), lambda i, n: (jnp.maximum(n - 1, 0), 0)),
            pl.BlockSpec((blk, 128), lambda i, n: (jnp.maximum(n - 1, 0), 0)),
        ],
        out_specs=[
            pl.BlockSpec((1, blk, SWA_WIDTH), lambda i, n: (i, n, 0)),
            pl.BlockSpec((1, blk, SWA_KV_WIDTH), lambda i, n: (i, 0, 0)),
        ],
        out_shape=[
            jax.ShapeDtypeStruct((b, t, SWA_WIDTH), F32),
            jax.ShapeDtypeStruct((b, blk, SWA_KV_WIDTH), F32),
        ],
        compiler_params=_cparams(("parallel", "arbitrary")),
        name="swa_prompt",
    )(sinks, h3, h3, h3, meta_kv, cos, sin, cos, sin)


def _swa_step_kernel(sinks_ref, q_ref, kv_ref, *rest, t_new, window):
    if window:
        kc_ref, vc_ref, cos_ref, sin_ref, o_ref, knew_ref = rest
    else:
        cos_ref, sin_ref, o_ref, knew_ref = rest
    cos = cos_ref[...]
    sin = sin_ref[...]
    q = _rope(q_ref[0], cos, sin) * (SWA_HD ** -0.5)
    kv = kv_ref[0]
    k = _rope(kv[:, :SWA_KV_WIDTH], cos, sin)
    v = kv[:, SWA_KV_WIDTH:]
    knew_ref[0] = k
    r = lax.broadcasted_iota(jnp.int32, (t_new, t_new), 0)
    j = lax.broadcasted_iota(jnp.int32, (t_new, t_new), 1)
    blocks = [(k, v, j <= r)]
    if window:
        rc = lax.broadcasted_iota(jnp.int32, (t_new, window), 0)
        jc = lax.broadcasted_iota(jnp.int32, (t_new, window), 1)
        diff = rc + window - jc
        blocks.insert(0, (kc_ref[0], vc_ref[0], (diff >= 0) & (diff < SWA_WINDOW)))
    _attend(q, blocks, sinks_ref, o_ref)


def _swa_step(h3, kc, vc, cos, sin, sinks):
    b, t, _ = h3.shape
    window = 0 if kc is None else kc.shape[1]
    in_specs = [
        pl.BlockSpec(memory_space=pltpu.SMEM),
        pl.BlockSpec((1, t, SWA_WIDTH), lambda i: (i, 0, QC_OFF // SWA_WIDTH)),
        pl.BlockSpec((1, t, 256), lambda i: (i, 0, KVC_OFF // 256)),
    ]
    args = [sinks, h3, h3]
    if window:
        in_specs += [pl.BlockSpec((1, window, SWA_KV_WIDTH), lambda i: (i, 0, 0))] * 2
        args += [kc, vc]
    in_specs += [pl.BlockSpec((t, 128), lambda i: (0, 0))] * 2
    args += [cos, sin]
    return pl.pallas_call(
        functools.partial(_swa_step_kernel, t_new=t, window=window),
        grid=(b,),
        in_specs=in_specs,
        out_specs=[
            pl.BlockSpec((1, t, SWA_WIDTH), lambda i: (i, 0, 0)),
            pl.BlockSpec((1, t, SWA_KV_WIDTH), lambda i: (i, 0, 0)),
        ],
        out_shape=[
            jax.ShapeDtypeStruct((b, t, SWA_WIDTH), F32),
            jax.ShapeDtypeStruct((b, t, SWA_KV_WIDTH), F32),
        ],
        compiler_params=_cparams(("parallel",)),
        name="swa_step",
    )(*args)


def _merge_kernel(oa_ref, ob_ref, oc_ref, ga_ref, gb_ref, gc_ref, x_ref, pa_ref, pb_ref, pc_ref, wo_ref, o_ref):
    def branch(o_r, g_r, p_r):
        return jax.nn.sigmoid(g_r[...]) * jnp.dot(o_r[...].astype(BF16), p_r[...], preferred_element_type=F32)

    m = branch(oa_ref, ga_ref, pa_ref) + branch(ob_ref, gb_ref, pb_ref) + branch(oc_ref, gc_ref, pc_ref)
    o_ref[...] = x_ref[...] + jnp.dot(m.astype(BF16), wo_ref[...], preferred_element_type=F32)


def _merge(oa, ob, oc, h, x, pa, pb, pc, wo):
    m = x.shape[0]
    tm = min(m, 512)
    assert m % tm == 0
    row =lambda w: pl.BlockSpec((tm, w), lambda i: (i, 0))
    gate = lambda c: pl.BlockSpec((tm, D_MODEL), lambda i: (i, c))
    full = lambda a: pl.BlockSpec(a.shape, lambda i: (0, 0))
    return pl.pallas_call(
        _merge_kernel,
        grid=(m // tm,),
        in_specs=[row(512), row(512), row(512), gate(0), gate(1), gate(2), row(D_MODEL),
                  full(pa), full(pb), full(pc), full(wo)],
        out_specs=row(D_MODEL),
        out_shape=jax.ShapeDtypeStruct((m, D_MODEL), F32),
        compiler_params=_cparams(("parallel",)),
        name="merge",
    )(oa, ob, oc, h, h, h, x, pa, pb, pc, wo)


def _mlp_kernel(x_ref, nw_ref, wu_ref, wd_ref, fnw_ref, o_ref, xn_ref, acc_ref, *, final):
    f = pl.program_id(1)

    @pl.when(f == 0)
    def _():
        x = x_ref[...]
        xn_ref[...] = _rms(x, nw_ref[...]).astype(BF16)
        acc_ref[...] = x

    a = jnp.maximum(jnp.dot(xn_ref[...], wu_ref[...], preferred_element_type=F32), 0.0)
    acc_ref[...] += jnp.dot((a * a).astype(BF16), wd_ref[...], preferred_element_type=F32)

    @pl.when(f == pl.num_programs(1) - 1)
    def _():
        y = acc_ref[...]
        o_ref[...] = _rms(y, fnw_ref[...]) if final else y


def _mlp(x, nw, wu, wd, fnw, final):
    m = x.shape[0]
    tm = min(m, 1024)
    assert m % tm == 0
    tf = 1024
    return pl.pallas_call(
        functools.partial(_mlp_kernel, final=final),
        grid=(m // tm, D_FF // tf),
        in_specs=[
            pl.BlockSpec((tm, D_MODEL), lambda i, f: (i, 0)),
            pl.BlockSpec((1, D_MODEL), lambda i, f: (0, 0)),
            pl.BlockSpec((D_MODEL, tf), lambda i, f: (0, f)),
            pl.BlockSpec((tf, D_MODEL), lambda i, f: (f, 0)),
            pl.BlockSpec((1, D_MODEL), lambda i, f: (0, 0)),
        ],
        out_specs=pl.BlockSpec((tm, D_MODEL), lambda i, f: (i, 0)),
        out_shape=jax.ShapeDtypeStruct((m, D_MODEL), F32),
        scratch_shapes=[pltpu.VMEM((tm, D_MODEL), BF16), pltpu.VMEM((tm, D_MODEL), F32)],
        compiler_params=_cparams(("parallel", "arbitrary")),
        name="mlp",
    )(x, nw, wu, wd, fnw)


def _rope_tables(pos0, t):
    half = SWA_HD // 2
    inv = ROPE_THETA ** (-jnp.arange(half, dtype=F32) / half)
    ang = (pos0 + jnp.arange(t)).astype(F32)[:, None] * inv[None, :]
    cos = jnp.cos(ang)
    sin = jnp.sin(ang)
    return jnp.tile(cos, (1, 4)), jnp.tile(jnp.concatenate([-sin, sin], axis=1), (1, 2))


def _prep_layer(l, norm1_w, w_in, pool_w, pool_scale, dn_conv_w, dn_a_log, dn_dt_bias, dn_onorm_w, swa_sinks,
                proj_a, proj_b, proj_c, w_out, norm2_w, w_up, w_down):
    w = w_in[l]
    w_perm = jnp.concatenate(
        [w[:, 3336:6408], w[:, 512:2048], w[:, 0:512], w[:, 2048:2560], w[:, 2568:3080], w[:, 3080:3336],
         w[:, 2560:2568], jnp.zeros((D_MODEL, H_WIDTH - BA_OFF - 2 * DN_HEADS), w.dtype)], axis=1).astype(BF16)
    lane_pad = lambda v: jnp.zeros((1, 128), F32).at[0, DN_HEADS:2 * DN_HEADS].set(v.astype(F32))
    return dict(
        norm1=norm1_w[l][None].astype(F32), w_in=w_perm,
        pool_w=pool_w[l].astype(BF16), pool_scale=pool_scale[l][None].astype(F32),
        conv_w=dn_conv_w[l].astype(F32), alog=lane_pad(dn_a_log[l]), dtb=lane_pad(dn_dt_bias[l]),
        onw=dn_onorm_w[l][None].astype(F32), sinks=swa_sinks[l].astype(F32),
        pa=proj_a[l].astype(BF16), pb=proj_b[l].astype(BF16), pc=proj_c[l].astype(BF16),
        wo=w_out[l].astype(BF16), norm2=norm2_w[l][None].astype(F32),
        wu=w_up[l].astype(BF16), wd=w_down[l].astype(BF16))


def _group_step(x3, p, fnw, final, pos0, pool_init, conv_init, s0, kind, swa_extra, pool_tiles, delta_tiles):
    b, t, _ = x3.shape
    x2 = x3.reshape(b * t, D_MODEL)
    h = _inproj(x2, p['norm1'], p['w_in'])
    h3 = h.reshape(b, t, H_WIDTH)
    o_a = _pool(h3, pool_init, p['pool_w'], p['pool_scale'], pos0, *pool_tiles)
    o_b, s_new = _delta(h3, conv_init, s0, p['conv_w'], p['alog'], p['dtb'], p['onw'], *delta_tiles)
    cos, sin = _rope_tables(pos0, t)
    if kind == 'prompt':
        o_c, k_rot = _swa_prompt(h3, swa_extra, cos, sin, p['sinks'], pos0)
    else:
        kc, vc = swa_extra
        o_c, k_rot = _swa_step(h3, kc, vc, cos, sin, p['sinks'])
    m = b * t
    h1 = _merge(o_a.reshape(m, 512), o_b.reshape(m, 512), o_c.reshape(m, 512), h, x2,
                p['pa'], p['pb'], p['pc'], p['wo'])
    out = _mlp(h1, p['norm2'], p['wu'], p['wd'], fnw, final)
    return out.reshape(b, t, D_MODEL), h3, s_new, k_rot


def kernel(x_prompt, x_sample, state_pool, state_conv, state_delta, cache_swa_k, cache_swa_v, meta_tokens, norm1_w, w_in, pool_w, pool_scale, dn_conv_w, dn_a_log, dn_dt_bias, dn_onorm_w, swa_sinks, proj_a, proj_b, proj_c, w_out, norm2_w, w_up, w_down, final_norm_w):
    depth = w_in.shape[0]
    bp, tp, _ = x_prompt.shape
    bs, ts, _ = x_sample.shape
    past = cache_swa_k.shape[2]
    fnw = final_norm_w[None].astype(F32)

    xm = meta_tokens[None].astype(F32)
    xp = x_prompt
    xs = x_sample
    outs = {k: [] for k in ('pool_p', 'conv_p', 'delta_p', 'k_p', 'v_p', 'pool_s', 'conv_s', 'delta_s', 'k_s', 'v_s')}
    for l in range(depth):
        p = _prep_layer(l, norm1_w, w_in, pool_w, pool_scale, dn_conv_w, dn_a_log, dn_dt_bias, dn_onorm_w,
                        swa_sinks, proj_a, proj_b, proj_c, w_out, norm2_w, w_up, w_down)
        final = l == depth - 1

        xm, hm, s_m, k_m = _group_step(
            xm, p, fnw, final, 0,
            jnp.zeros((1, POOL_HALO, POOL_WIDTH), F32), jnp.zeros((1, CONV_HALO, CONV_CH), F32),
            jnp.zeros((1, DN_HEADS, DN_DK, DN_DV), F32), 'step', (None, None),
            (1, N_META), (1, N_META, N_META))

        meta_kv = jnp.zeros((SWA_WINDOW, 256), F32)
        meta_kv = meta_kv.at[SWA_WINDOW - N_META:, :SWA_KV_WIDTH].set(k_m[0])
        meta_kv = meta_kv.at[SWA_WINDOW - N_META:, SWA_KV_WIDTH:].set(hm[0, :, KVC_OFF + SWA_KV_WIDTH:KVC_OFF + 256])
        xp, hp, s_p, k_p = _group_step(
            xp, p, fnw, final, N_META,
            jnp.broadcast_to(hm[:, :, U_OFF:U_OFF + POOL_WIDTH], (bp, POOL_HALO, POOL_WIDTH)),
            jnp.broadcast_to(hm[:, N_META - CONV_HALO:, QKV_OFF:QKV_OFF + CONV_CH], (bp, CONV_HALO, CONV_CH)),
            jnp.broadcast_to(s_m, (bp, DN_HEADS, DN_DK, DN_DV)), 'prompt', meta_kv,
            (1, 512), (1, 2 * DN_CHUNK, DN_CHUNK))
        outs['pool_p'].append(hp[:, tp - POOL_BUF:, U_OFF:U_OFF + POOL_WIDTH])
        outs['conv_p'].append(hp[:, tp - (CONV_W - 1):, QKV_OFF:QKV_OFF + CONV_CH])
        outs['delta_p'].append(s_p)
        outs['k_p'].append(k_p.reshape(bp, SWA_WINDOW, SWA_KV_HEADS, SWA_HD))
        outs['v_p'].append(hp[:, tp - SWA_WINDOW:, KVC_OFF + SWA_KV_WIDTH:KVC_OFF + 256]
                           .reshape(bp, SWA_WINDOW, SWA_KV_HEADS, SWA_HD))

        kc = cache_swa_k[l].reshape(bs, past, SWA_KV_WIDTH)
        vc = cache_swa_v[l].reshape(bs, past, SWA_KV_WIDTH)
        xs, hs, s_s, k_s = _group_step(
            xs, p, fnw, final, PAST_LEN,
            jnp.concatenate([jnp.zeros((bs, POOL_HALO - POOL_BUF, POOL_WIDTH), F32), state_pool[l]], axis=1),
            jnp.concatenate([jnp.zeros((bs, CONV_HALO - (CONV_W - 1), CONV_CH), F32), state_conv[l]], axis=1),
            state_delta[l], 'step', (kc, vc),
            (bs, ts), (16, ts, ts))
        outs['pool_s'].append(jnp.concatenate([state_pool[l], hs[:, :, U_OFF:U_OFF + POOL_WIDTH]], axis=1)[:, -POOL_BUF:])
        outs['conv_s'].append(jnp.concatenate([state_conv[l], hs[:, :, QKV_OFF:QKV_OFF + CONV_CH]], axis=1)[:, -(CONV_W - 1):])
        outs['delta_s'].append(s_s)
        v_new = hs[:, :, KVC_OFF + SWA_KV_WIDTH:KVC_OFF + 256]
        outs['k_s'].append(jnp.concatenate([kc, k_s], axis=1)[:, -past:].reshape(bs, past, SWA_KV_HEADS, SWA_HD))
        outs['v_s'].append(jnp.concatenate([vc, v_new], axis=1)[:, -past:].reshape(bs, past, SWA_KV_HEADS, SWA_HD))

    st = {k: jnp.stack(v) for k, v in outs.items()}
    return (xp, xs, st['pool_p'], st['conv_p'], st['delta_p'], st['k_p'], st['v_p'],
            st['pool_s'], st['conv_s'], st['delta_s'], st['k_s'], st['v_s'])
```

```python
import functools
import math

import jax
import jax.numpy as jnp
from jax import lax
from jax.experimental import pallas as pl
from jax.experimental.pallas import tpu as pltpu

F32 = jnp.float32
BF16 = jnp.bfloat16

D_MODEL = 1024
N_META = 16
EPS = 1e-6
POOL_GROUPS = 4
POOL_GROUP_DIM = 128
POOL_WIDTH = 512
POOL_WINDOWS = (2, 4, 8, 16)
POOL_BUF = 15
POOL_HALO = 16
DN_HEADS = 4
DN_DK = 128
DN_DV = 128
DN_QK = 512
DN_VW = 512
CONV_W = 4
CONV_CH = 1536
CONV_HALO = 8
DN_CHUNK = 64
SWA_HEADS = 8
SWA_KV_HEADS = 2
SWA_GROUP = 4
SWA_HD = 64
SWA_WIDTH = 512
SWA_KV_WIDTH = 128
SWA_WINDOW = 128
ROPE_THETA = 10000.0
D_FF = 4096
PAST_LEN = 16384

G_OFF = 0
QKV_OFF = 3072
U_OFF = 4608
Z_OFF = 5120
QC_OFF = 5632
KVC_OFF = 6144
BA_OFF = 6400
H_WIDTH = 6656
IN_TN = 1664

VMEM_LIMIT = 56 * 1024 * 1024
NEG_BIG = -1e30


def _cparams(sem):
    return pltpu.CompilerParams(dimension_semantics=sem, vmem_limit_bytes=VMEM_LIMIT)


def _rms(x, w):
    return x * lax.rsqrt(jnp.mean(x * x, axis=-1, keepdims=True) + EPS) * w


def _inproj_kernel(x_ref, nw_ref, w_ref, o_ref, xn_ref):
    @pl.when(pl.program_id(1) == 0)
    def _():
        xn_ref[...] = _rms(x_ref[...], nw_ref[...]).astype(BF16)

    o_ref[...] = jnp.dot(xn_ref[...], w_ref[...], preferred_element_type=F32)


def _inproj(x, nw, w):
    m = x.shape[0]
    tm = min(m, 1024)
    assert m % tm == 0
    return pl.pallas_call(
        _inproj_kernel,
        grid=(m // tm, H_WIDTH // IN_TN),
        in_specs=[
            pl.BlockSpec((tm, D_MODEL), lambda i, j: (i, 0)),
            pl.BlockSpec((1, D_MODEL), lambda i, j: (0, 0)),
            pl.BlockSpec((D_MODEL, IN_TN), lambda i, j: (0, j)),
        ],
        out_specs=pl.BlockSpec((tm, IN_TN), lambda i, j: (i, j)),
        out_shape=jax.ShapeDtypeStruct((m, H_WIDTH), F32),
        scratch_shapes=[pltpu.VMEM((tm, D_MODEL), BF16)],
        compiler_params=_cparams(("parallel", "arbitrary")),
        name="inproj",
    )(x, nw, w)


def _pool_kernel(u_ref, init_ref, pw_ref, ps_ref, o_ref, ext_ref, *, pos0, tt):
    t = pl.program_id(1)
    bb = u_ref.shape[0]

    @pl.when(t == 0)
    def _():
        ext_ref[:, 0:POOL_HALO, :] = init_ref[...]

    @pl.when(t > 0)
    def _():
        ext_ref[:, 0:POOL_HALO, :] = ext_ref[:, tt:tt + POOL_HALO, :]

    ext_ref[:, POOL_HALO:POOL_HALO + tt, :] = u_ref[...]

    pos = pos0 + t * tt + lax.broadcasted_iota(jnp.int32, (1, tt, 1), 1)
    for g, w in enumerate(POOL_WINDOWS):
        cs = slice(g * POOL_GROUP_DIM, (g + 1) * POOL_GROUP_DIM)
        x = ext_ref[:, POOL_HALO:POOL_HALO + tt, cs]
        s = x
        for k in range(1, w):
            s = s + ext_ref[:, POOL_HALO - k:POOL_HALO - k + tt, cs]
        cnt = jnp.minimum(w, pos + 1).astype(F32)
        d = (s / cnt - x).reshape(bb * tt, POOL_GROUP_DIM)
        y = jnp.dot(d.astype(BF16), pw_ref[g], preferred_element_type=F32)
        o_ref[:, :, cs] = (y * ps_ref[:, cs]).reshape(bb, tt, POOL_GROUP_DIM)


def _pool(h3, init, pw, ps, pos0, bb, tt):
    b, t, _ = h3.shape
    return pl.pallas_call(
        functools.partial(_pool_kernel, pos0=pos0, tt=tt),
        grid=(b // bb, t // tt),
        in_specs=[
            pl.BlockSpec((bb, tt, POOL_WIDTH), lambda i, j: (i, j, U_OFF // POOL_WIDTH)),
            pl.BlockSpec((bb, POOL_HALO, POOL_WIDTH), lambda i, j: (i, 0, 0)),
            pl.BlockSpec((POOL_GROUPS, POOL_GROUP_DIM, POOL_GROUP_DIM), lambda i, j: (0, 0, 0)),
            pl.BlockSpec((1, POOL_WIDTH), lambda i, j: (0, 0)),
        ],
        out_specs=pl.BlockSpec((bb, tt, POOL_WIDTH), lambda i, j: (i, j, 0)),
        out_shape=jax.ShapeDtypeStruct((b, t, POOL_WIDTH), F32),
        scratch_shapes=[pltpu.VMEM((bb, POOL_HALO + tt, POOL_WIDTH), F32)],
        compiler_params=_cparams(("parallel", "arbitrary")),
        name="pool",
    )(h3, init, pw, ps)


def _softplus(x):
    return jnp.maximum(x, 0.0) + jnp.log1p(jnp.exp(-jnp.abs(x)))


def _dot_nt(a, b, **kw):
    return lax.dot_general(a, b, (((1,), (1,)), ((), ())), preferred_element_type=F32, **kw)


def _dot_tn(a, b):
    return lax.dot_general(a, b, (((0,), (0,)), ((), ())), preferred_element_type=F32)


def _dot(a, b):
    return jnp.dot(a, b, preferred_element_type=F32)


def _delta_kernel(qkv_ref, z_ref, ba_ref, cinit_ref, s0_ref, cw_ref, alog_ref, dtb_ref, onw_ref, *rest,
                  bb, tb, chunk):
    o_ref, s_ref, ext_ref = rest[-3:]
    t = pl.program_id(1)
    rows = bb * tb
    cpb = tb // chunk
    heads = range(DN_HEADS)
    mm = BF16 if rows >= 16 else F32
    mc = BF16 if chunk >= 16 else F32

    @pl.when(t == 0)
    def _():
        ext_ref[:, 0:CONV_HALO, :] = cinit_ref[...]
        s_ref[...] = s0_ref[...]

    @pl.when(t > 0)
    def _():
        ext_ref[:, 0:CONV_HALO, :] = ext_ref[:, tb:tb + CONV_HALO, :]

    ext_ref[:, CONV_HALO:CONV_HALO + tb, :] = qkv_ref[...]

    base = CONV_HALO - (CONV_W - 1)
    y = ext_ref[:, base:base + tb, :] * cw_ref[0:1, :]
    for j in range(1, CONV_W):
        y = y + ext_ref[:, base + j:base + j + tb, :] * cw_ref[j:j + 1, :]
    y = (y * jax.nn.sigmoid(y)).reshape(rows, CONV_CH)

    bav = ba_ref[...].reshape(rows, 128)
    zv = z_ref[...].reshape(rows, DN_VW)
    beta_full = jax.nn.sigmoid(bav)
    g_full = -jnp.exp(alog_ref[...]) * _softplus(bav + dtb_ref[...])

    row = lax.broadcasted_iota(jnp.int32, (rows, rows), 0)
    col = lax.broadcasted_iota(jnp.int32, (rows, rows), 1)
    shift = int(math.log2(chunk))
    same = (row >> shift) == (col >> shift)
    tri = same & (row >= col)
    strict = same & (row > col)

    gcum = jnp.dot(tri.astype(F32), g_full, preferred_element_type=F32, precision=lax.Precision.HIGHEST)

    lane = lax.broadcasted_iota(jnp.int32, (rows, 128), 1)
    gcol = [jnp.sum(jnp.where(lane == DN_HEADS + h, gcum, 0.0), axis=1, keepdims=True) for h in heads]
    bcol = [jnp.sum(jnp.where(lane == h, beta_full, 0.0), axis=1, keepdims=True) for h in heads]
    if rows % 128 == 0:
        gcum_t = gcum.T
        grow = [gcum_t[DN_HEADS + h:DN_HEADS + h + 1, :] for h in heads]
    else:
        ones = jnp.ones((rows, 128), F32)
        grow = [_dot_nt(ones, jnp.where(lane == DN_HEADS + h, gcum, 0.0), precision=lax.Precision.HIGHEST)
                for h in heads]
    decay = [jnp.where(tri, jnp.exp(jnp.where(tri, gcol[h] - grow[h], 0.0)), 0.0) for h in heads]
    eg = [jnp.exp(gcol[h]) for h in heads]

    def l2n(v):
        return v * lax.rsqrt(jnp.sum(v * v, axis=-1, keepdims=True) + EPS)

    qn = [l2n(y[:, h * DN_DK:(h + 1) * DN_DK]) * (DN_DK ** -0.5) for h in heads]
    kn = [l2n(y[:, DN_QK + h * DN_DK:DN_QK + (h + 1) * DN_DK]) for h in heads]
    vh = [y[:, 2 * DN_QK + h * DN_DV:2 * DN_QK + (h + 1) * DN_DV] for h in heads]
    kb = [kn[h].astype(mm) for h in heads]
    kk = [_dot_nt(kb[h], kb[h]) for h in heads]
    qk = [_dot_nt(qn[h].astype(mm), kb[h]) for h in heads]

    m = [jnp.where(strict, kk[h] * decay[h] * bcol[h], 0.0) for h in heads]
    bshift = min(shift, 4)
    p = [jnp.where((row >> bshift) == (col >> bshift), -m[h], 0.0) for h in heads]
    xp = p
    for _ in range(bshift - 1):
        xb = [xp[h].astype(mm) for h in heads]
        xp = [_dot(xb[h], xb[h]) for h in heads]
        p = [p[h] + xp[h] + _dot(p[h].astype(mm), xp[h].astype(mm)) for h in heads]
    for s in range(bshift, shift):
        lower = ((row >> (s + 1)) == (col >> (s + 1))) & ((row >> s) > (col >> s))
        c_blk = [jnp.where(lower, m[h], 0.0) for h in heads]
        pb = [p[h].astype(mm) for h in heads]
        a = [c_blk[h] + _dot(pb[h], c_blk[h].astype(mm)) for h in heads]
        p = [p[h] - a[h] - _dot(a[h].astype(mm), pb[h]) for h in heads]
    rhs = [jnp.concatenate([vh[h] * bcol[h], kn[h] * (bcol[h] * eg[h])], axis=1) for h in heads]
    uw = [rhs[h] + _dot(p[h].astype(mm), rhs[h].astype(mm)) for h in heads]
    qkd = [(qk[h] * decay[h]).astype(mm) for h in heads]
    qg = [qn[h] * eg[h] for h in heads]

    pairs = [(b, h) for b in range(bb) for h in heads]
    state = {bh: s_ref[bh[0], bh[1]] for bh in pairs}
    dlt = {h: [] for h in heads}
    oq = {h: [] for h in heads}
    for c in range(cpb):
        def rs(b):
            return slice(b * tb + c * chunk, b * tb + (c + 1) * chunk)
        prod = {}
        for b, h in pairs:
            lhs = jnp.concatenate([uw[h][rs(b), DN_DV:], qg[h][rs(b)]], axis=0).astype(mc)
            prod[b, h] = _dot(lhs, state[b, h].astype(mc))
        for b, h in pairs:
            r = rs(b)
            d = uw[h][r, :DN_DV] - prod[b, h][:chunk]
            glast = gcol[h][r.stop - 1:r.stop, :]
            kg = kn[h][r] * jnp.exp(glast - gcol[h][r])
            state[b, h] = state[b, h] * jnp.exp(glast) + _dot_tn(kg.astype(mc), d.astype(mc))
            dlt[h].append((r.start, d))
            oq[h].append((r.start, prod[b, h][chunk:]))
    for b, h in pairs:
        s_ref[b, h] = state[b, h]

    def stack(parts):
        parts = [v for _, v in sorted(parts, key=lambda sv: sv[0])]
        return parts[0] if len(parts) == 1 else jnp.concatenate(parts, axis=0)

    for h in heads:
        hs = slice(h * DN_DV, (h + 1) * DN_DV)
        o = stack(oq[h]) + _dot(qkd[h], stack(dlt[h]).astype(mm))
        zh = zv[:, hs]
        o_ref[:, :, hs] = (_rms(o, onw_ref[...]) * (zh * jax.nn.sigmoid(zh))).reshape(bb, tb, DN_DV)


def _delta(h3, cinit, s0, s0_layer, cw, alog, dtb, onw, bb, tb, chunk, stack=None):
    b, t, _ = h3.shape
    assert b % bb == 0 and t % tb == 0 and tb % chunk == 0
    state_block = (None, bb, DN_HEADS, DN_DK, DN_DV)
    in_specs = [
        pl.BlockSpec((bb, tb, CONV_CH), lambda i, j: (i, j, QKV_OFF // CONV_CH)),
        pl.BlockSpec((bb, tb, DN_VW), lambda i, j: (i, j, Z_OFF // DN_VW)),
        pl.BlockSpec((bb, tb, 128), lambda i, j: (i, j, BA_OFF // 128)),
        pl.BlockSpec((bb, CONV_HALO, CONV_CH), lambda i, j: (i, 0, 0)),
        pl.BlockSpec(state_block, lambda i, j: (s0_layer, i, 0, 0, 0)),
        pl.BlockSpec((CONV_W, CONV_CH), lambda i, j: (0, 0)),
        pl.BlockSpec((1, 128), lambda i, j: (0, 0)),
        pl.BlockSpec((1, 128), lambda i, j: (0, 0)),
        pl.BlockSpec((1, DN_DV), lambda i, j: (0, 0)),
    ]
    args = [h3, h3, h3, cinit, s0, cw, alog, dtb, onw]
    layer, depth, prev = (0, 1, None) if stack is None else stack
    aliases = {}
    if prev is not None:
        in_specs.append(pl.BlockSpec(memory_space=pl.ANY))
        aliases = {len(args): 1}
        args.append(prev)
    o, s = pl.pallas_call(
        functools.partial(_delta_kernel, bb=bb, tb=tb, chunk=chunk),
        grid=(b // bb, t // tb),
        in_specs=in_specs,
        out_specs=[
            pl.BlockSpec((bb, tb, DN_VW), lambda i, j: (i, j, 0)),
            pl.BlockSpec(state_block, lambda i, j: (layer, i, 0, 0, 0)),
        ],
        out_shape=[
            jax.ShapeDtypeStruct((b, t, DN_VW), F32),
            jax.ShapeDtypeStruct((depth, b, DN_HEADS, DN_DK, DN_DV), F32),
        ],
        scratch_shapes=[pltpu.VMEM((bb, CONV_HALO + tb, CONV_CH), F32)],
        input_output_aliases=aliases,
        compiler_params=_cparams(("parallel", "arbitrary")),
        name="delta",
    )(*args)
    return o, (s[0] if stack is None else s)


def _rope(x, cos, sin):
    width = x.shape[-1]
    reps = width // cos.shape[-1]
    if reps > 1:
        cos = jnp.concatenate([cos] * reps, axis=1)
        sin = jnp.concatenate([sin] * reps, axis=1)
    lane = lax.broadcasted_iota(jnp.int32, x.shape, 1)
    first_half = (lane & (SWA_HD - 1)) < (SWA_HD // 2)
    other = jnp.where(first_half, pltpu.roll(x, width - SWA_HD // 2, 1), pltpu.roll(x, SWA_HD // 2, 1))
    return x * cos + other * sin


def _pad_heads(x, lo):
    xr = pltpu.roll(x, SWA_HD, 1)
    z = jnp.zeros_like(x)
    return [jnp.where(lo, x, z), jnp.where(lo, z, xr), jnp.where(lo, xr, z), jnp.where(lo, z, x)]


def _swa_prompt_kernel(sinks_ref, q_ref, kv_ref, meta_ref, cos_ref, sin_ref, o_ref, klast_ref, kcat_ref, vcat_ref,
                       *, pos0):
    n = pl.program_id(1)
    blk = SWA_WINDOW
    lo = lax.broadcasted_iota(jnp.int32, (blk, 128), 1) < SWA_HD

    def put(half, k, v):
        rows = slice(half * blk, (half + 1) * blk)
        for i, (a, b) in enumerate(zip(_pad_heads(k, lo), _pad_heads(v, lo))):
            kcat_ref[i, rows, :] = a.astype(BF16)
            vcat_ref[i, rows, :] = b.astype(BF16)

    @pl.when(n == 0)
    def _():
        put(0, meta_ref[:, :SWA_KV_WIDTH], meta_ref[:, SWA_KV_WIDTH:])

    @pl.when(n > 0)
    def _():
        kcat_ref[:, 0:blk, :] = kcat_ref[:, blk:2 * blk, :]
        vcat_ref[:, 0:blk, :] = vcat_ref[:, blk:2 * blk, :]

    cos = cos_ref[...]
    sin = sin_ref[...]
    q = _rope(q_ref[0], cos, sin) * (SWA_HD ** -0.5)
    kv = kv_ref[0]
    k = _rope(kv[:, :SWA_KV_WIDTH], cos, sin)
    klast_ref[0] = k
    put(1, k, kv[:, SWA_KV_WIDTH:])

    r = lax.broadcasted_iota(jnp.int32, (blk, 2 * blk), 0)
    j = lax.broadcasted_iota(jnp.int32, (blk, 2 * blk), 1)
    jmin = jnp.maximum(blk - pos0 - n * blk, 0)
    mask = ((j < blk) & (j > r) & (j >= jmin)) | ((j >= blk) & ((j - blk) <= r))
    heads = range(SWA_HEADS)
    qt = [q[:, i * 128:(i + 1) * 128].astype(BF16) for i in range(SWA_HEADS // 2)]
    s = [jnp.where(mask, _dot_nt(qt[h // 2], kcat_ref[2 * (h // SWA_GROUP) + h % 2]), NEG_BIG) for h in heads]
    m = [jnp.maximum(jnp.max(s[h], axis=1, keepdims=True), sinks_ref[h]) for h in heads]
    e = [jnp.exp(s[h] - m[h]).astype(BF16) for h in heads]
    ones = jnp.ones((2 * blk, 128), BF16)
    den = [_dot(e[h], ones) + jnp.exp(sinks_ref[h] - m[h]) for h in heads]
    for i in range(SWA_HEADS // 2):
        he, ho = 2 * i, 2 * i + 1
        g = he // SWA_GROUP
        acc = _dot(e[he], vcat_ref[2 * g]) + _dot(e[ho], vcat_ref[2 * g + 1])
        o_ref[0, :, i * 128:(i + 1) * 128] = acc / jnp.where(lo, den[he], den[ho])


def _swa_prompt(h3, meta_kv, cos, sin, sinks, pos0):
    b, t, _ = h3.shape
    blk = SWA_WINDOW
    assert t % blk == 0
    return pl.pallas_call(
        functools.partial(_swa_prompt_kernel, pos0=pos0),
        grid=(b, t // blk),
        in_specs=[
            pl.BlockSpec(memory_space=pltpu.SMEM),
            pl.BlockSpec((1, blk, SWA_WIDTH), lambda i, n: (i, n, QC_OFF // SWA_WIDTH)),
            pl.BlockSpec((1, blk, 256), lambda i, n: (i, n, KVC_OFF // 256)),
            pl.BlockSpec((blk, 256), lambda i, n: (0, 0)),
            pl.BlockSpec((blk, 128), lambda i, n: (n, 0)),
            pl.BlockSpec((blk, 128), lambda i, n: (n, 0)),
        ],
        out_specs=[
            pl.BlockSpec((1, blk, SWA_WIDTH), lambda i, n: (i, n, 0)),
            pl.BlockSpec((1, blk, SWA_KV_WIDTH), lambda i, n: (i, 0, 0)),
        ],
        out_shape=[
            jax.ShapeDtypeStruct((b, t, SWA_WIDTH), F32),
            jax.ShapeDtypeStruct((b, blk, SWA_KV_WIDTH), F32),
        ],
        scratch_shapes=[pltpu.VMEM((4, 2 * blk, 128), BF16), pltpu.VMEM((4, 2 * blk, 128), BF16)],
        compiler_params=_cparams(("parallel", "arbitrary")),
        name="swa_prompt",
    )(sinks, h3, h3, meta_kv, cos, sin)


def _swa_step_kernel(sink_ref, q_ref, kv_ref, *rest, bb, t_new, window):
    if window:
        kc_ref, vc_ref, cos_ref, sin_ref, o_ref, knew_ref = rest
    else:
        cos_ref, sin_ref, o_ref, knew_ref = rest
    cos = cos_ref[...]
    sin = sin_ref[...]
    rows = SWA_HEADS * t_new
    mn = BF16 if t_new >= 16 else F32
    lo = lax.broadcasted_iota(jnp.int32, (t_new, 128), 1) < SWA_HD
    tq = lax.broadcasted_iota(jnp.int32, (rows, t_new), 0) & (t_new - 1)
    mask_n = lax.broadcasted_iota(jnp.int32, (rows, t_new), 1) <= tq
    if window:
        diff = ((lax.broadcasted_iota(jnp.int32, (rows, window), 0) & (t_new - 1)) + window
                - lax.broadcasted_iota(jnp.int32, (rows, window), 1))
        mask_c = (diff >= 0) & (diff < SWA_WINDOW)
    sink = sink_ref[...]
    seqs = range(bb)

    qall, kk, vv = [], [], []
    for s in seqs:
        q = _rope(q_ref[s], cos, sin) * (SWA_HD ** -0.5)
        kv = kv_ref[s]
        k = _rope(kv[:, :SWA_KV_WIDTH], cos, sin)
        knew_ref[s] = k
        pieces = []
        for h in range(SWA_HEADS):
            g = h // SWA_GROUP
            tile = q[:, (h // 2) * 128:(h // 2 + 1) * 128]
            if h % 2 != g:
                tile = pltpu.roll(tile, SWA_HD, 1)
            pieces.append(jnp.where(lo, tile, 0.0) if g == 0 else jnp.where(lo, 0.0, tile))
        qall.append(jnp.concatenate(pieces, axis=0))
        kk.append(k)
        vv.append(kv[:, SWA_KV_WIDTH:])

    s_n = [jnp.where(mask_n, _dot_nt(qall[s].astype(mn), kk[s].astype(mn)), NEG_BIG) for s in seqs]
    m = [jnp.maximum(jnp.max(s_n[s], axis=1, keepdims=True), sink) for s in seqs]
    if window:
        s_c = [jnp.where(mask_c, _dot_nt(qall[s].astype(BF16), kc_ref[s].astype(BF16)), NEG_BIG) for s in seqs]
        m = [jnp.maximum(m[s], jnp.max(s_c[s], axis=1, keepdims=True)) for s in seqs]
    e_n = [jnp.exp(s_n[s] - m[s]) for s in seqs]
    den = [jnp.sum(e_n[s], axis=1, keepdims=True) + jnp.exp(sink - m[s]) for s in seqs]
    acc = [_dot(e_n[s].astype(mn), vv[s].astype(mn)) for s in seqs]
    if window:
        e_c = [jnp.exp(s_c[s] - m[s]) for s in seqs]
        den = [den[s] + jnp.sum(e_c[s], axis=1, keepdims=True) for s in seqs]
        acc = [acc[s] + _dot(e_c[s].astype(BF16), vc_ref[s].astype(BF16)) for s in seqs]
    for s in seqs:
        a = acc[s] / den[s]
        for i in range(SWA_HEADS // 2):
            he, ho = 2 * i, 2 * i + 1
            g = he // SWA_GROUP
            a_e = a[he * t_new:(he + 1) * t_new]
            a_o = a[ho * t_new:(ho + 1) * t_new]
            if g == 1:
                a_e = pltpu.roll(a_e, SWA_HD, 1)
            else:
                a_o = pltpu.roll(a_o, SWA_HD, 1)
            o_ref[s, :, i * 128:(i + 1) * 128] = jnp.where(lo, a_e, a_o)


def _swa_step(h3, kc, vc, cos, sin, sinks, bb):
    b, t, _ = h3.shape
    assert b % bb == 0
    window = 0 if kc is None else kc.shape[1]
    sink_rows = jnp.repeat(sinks, t)[:, None]
    in_specs = [
        pl.BlockSpec((SWA_HEADS * t, 1), lambda i: (0, 0)),
        pl.BlockSpec((bb, t, SWA_WIDTH), lambda i: (i, 0, QC_OFF // SWA_WIDTH)),
        pl.BlockSpec((bb, t, 256), lambda i: (i, 0, KVC_OFF // 256)),
    ]
    args = [sink_rows, h3, h3]
    if window:
        in_specs += [pl.BlockSpec((bb, window, SWA_KV_WIDTH), lambda i: (i, 0, 0))] * 2
        args += [kc, vc]
    in_specs += [pl.BlockSpec((t, 128), lambda i: (0, 0))] * 2
    args += [cos, sin]
    return pl.pallas_call(
        functools.partial(_swa_step_kernel, bb=bb, t_new=t, window=window),
        grid=(b // bb,),
        in_specs=in_specs,
        out_specs=[
            pl.BlockSpec((bb, t, SWA_WIDTH), lambda i: (i, 0, 0)),
            pl.BlockSpec((bb, t, SWA_KV_WIDTH), lambda i: (i, 0, 0)),
        ],
        out_shape=[
            jax.ShapeDtypeStruct((b, t, SWA_WIDTH), F32),
            jax.ShapeDtypeStruct((b, t, SWA_KV_WIDTH), F32),
        ],
        compiler_params=_cparams(("parallel",)),
        name="swa_step",
    )(*args)


def _merge_kernel(oa_ref, ob_ref, oc_ref, ga_ref, gb_ref, gc_ref, x_ref, pa_ref, pb_ref, pc_ref, wo_ref, o_ref):
    def branch(o_r, g_r, p_r):
        return jax.nn.sigmoid(g_r[...]) * jnp.dot(o_r[...].astype(BF16), p_r[...], preferred_element_type=F32)

    m = branch(oa_ref, ga_ref, pa_ref) + branch(ob_ref, gb_ref, pb_ref) + branch(oc_ref, gc_ref, pc_ref)
    o_ref[...] = x_ref[...] + jnp.dot(m.astype(BF16), wo_ref[...], preferred_element_type=F32)


def _merge(oa, ob, oc, h, x, pa, pb, pc, wo):
    m = x.shape[0]
    tm = min(m, 512)
    assert m % tm == 0
    row =lambda w: pl.BlockSpec((tm, w), lambda i: (i, 0))
    gate = lambda c: pl.BlockSpec((tm, D_MODEL), lambda i: (i, c))
    full = lambda a: pl.BlockSpec(a.shape, lambda i: (0, 0))
    return pl.pallas_call(
        _merge_kernel,
        grid=(m // tm,),
        in_specs=[row(512), row(512), row(512), gate(0), gate(1), gate(2), row(D_MODEL),
                  full(pa), full(pb), full(pc), full(wo)],
        out_specs=row(D_MODEL),
        out_shape=jax.ShapeDtypeStruct((m, D_MODEL), F32),
        compiler_params=_cparams(("parallel",)),
        name="merge",
    )(oa, ob, oc, h, h, h, x, pa, pb, pc, wo)


def _mlp_kernel(x_ref, nw_ref, wu_ref, wd_ref, fnw_ref, o_ref, xn_ref, acc_ref, *, final):
    f = pl.program_id(1)

    @pl.when(f == 0)
    def _():
        x = x_ref[...]
        xn_ref[...] = _rms(x, nw_ref[...]).astype(BF16)
        acc_ref[...] = x

    a = jnp.maximum(jnp.dot(xn_ref[...], wu_ref[...], preferred_element_type=F32), 0.0)
    acc_ref[...] += jnp.dot((a * a).astype(BF16), wd_ref[...], preferred_element_type=F32)

    @pl.when(f == pl.num_programs(1) - 1)
    def _():
        y = acc_ref[...]
        o_ref[...] = _rms(y, fnw_ref[...]) if final else y


def _mlp(x, nw, wu, wd, fnw, final):
    m = x.shape[0]
    tm = min(m, 1024)
    assert m % tm == 0
    tf = 1024
    return pl.pallas_call(
        functools.partial(_mlp_kernel, final=final),
        grid=(m // tm, D_FF // tf),
        in_specs=[
            pl.BlockSpec((tm, D_MODEL), lambda i, f: (i, 0)),
            pl.BlockSpec((1, D_MODEL), lambda i, f: (0, 0)),
            pl.BlockSpec((D_MODEL, tf), lambda i, f: (0, f)),
            pl.BlockSpec((tf, D_MODEL), lambda i, f: (f, 0)),
            pl.BlockSpec((1, D_MODEL), lambda i, f: (0, 0)),
        ],
        out_specs=pl.BlockSpec((tm, D_MODEL), lambda i, f: (i, 0)),
        out_shape=jax.ShapeDtypeStruct((m, D_MODEL), F32),
        scratch_shapes=[pltpu.VMEM((tm, D_MODEL), BF16), pltpu.VMEM((tm, D_MODEL), F32)],
        compiler_params=_cparams(("parallel", "arbitrary")),
        name="mlp",
    )(x, nw, wu, wd, fnw)


def _rope_tables(pos0, t):
    half = SWA_HD // 2
    inv = ROPE_THETA ** (-jnp.arange(half, dtype=F32) / half)
    ang = (pos0 + jnp.arange(t)).astype(F32)[:, None] * inv[None, :]
    cos = jnp.cos(ang)
    sin = jnp.sin(ang)
    return jnp.tile(cos, (1, 4)), jnp.tile(jnp.concatenate([-sin, sin], axis=1), (1, 2))


def _prep_layer(l, norm1_w, w_in, pool_w, pool_scale, dn_conv_w, dn_a_log, dn_dt_bias, dn_onorm_w, swa_sinks,
                proj_a, proj_b, proj_c, w_out, norm2_w, w_up, w_down):
    w = w_in[l]
    segments = ((3336, 6408), (512, 2048), (0, 512), (2048, 2560), (2568, 3080), (3080, 3336), (2560, 2568))
    w_perm = jnp.concatenate(
        [w[:, a:b].astype(BF16) for a, b in segments]
        + [jnp.zeros((D_MODEL, H_WIDTH - BA_OFF - 2 * DN_HEADS), BF16)], axis=1)
    lane_pad = lambda v: jnp.zeros((1, 128), F32).at[0, DN_HEADS:2 * DN_HEADS].set(v.astype(F32))
    return dict(
        norm1=norm1_w[l][None].astype(F32), w_in=w_perm,
        pool_w=pool_w[l].astype(BF16), pool_scale=pool_scale[l][None].astype(F32),
        conv_w=dn_conv_w[l].astype(F32), alog=lane_pad(dn_a_log[l]), dtb=lane_pad(dn_dt_bias[l]),
        onw=dn_onorm_w[l][None].astype(F32), sinks=swa_sinks[l].astype(F32),
        pa=proj_a[l].astype(BF16), pb=proj_b[l].astype(BF16), pc=proj_c[l].astype(BF16),
        wo=w_out[l].astype(BF16), norm2=norm2_w[l][None].astype(F32),
        wu=w_up[l].astype(BF16), wd=w_down[l].astype(BF16))


def _group_step(x3, p, fnw, final, pos0, pool_init, conv_init, s0, kind, swa_extra, pool_tiles, delta_tiles,
                swa_bb=1, s_stack=None):
    b, t, _ = x3.shape
    x2 = x3.reshape(b * t, D_MODEL)
    h = _inproj(x2, p['norm1'], p['w_in'])
    h3 = h.reshape(b, t, H_WIDTH)
    o_a = _pool(h3, pool_init, p['pool_w'], p['pool_scale'], pos0, *pool_tiles)
    o_b, s_new = _delta(h3, conv_init, s0[0], s0[1], p['conv_w'], p['alog'], p['dtb'], p['onw'], *delta_tiles,
                        stack=s_stack)
    cos, sin = _rope_tables(pos0, t)
    if kind == 'prompt':
        o_c, k_rot = _swa_prompt(h3, swa_extra, cos, sin, p['sinks'], pos0)
    else:
        kc, vc = swa_extra
        o_c, k_rot = _swa_step(h3, kc, vc, cos, sin, p['sinks'], swa_bb)
    m = b * t
    h1 = _merge(o_a.reshape(m, 512), o_b.reshape(m, 512), o_c.reshape(m, 512), h, x2,
                p['pa'], p['pb'], p['pc'], p['wo'])
    out = _mlp(h1, p['norm2'], p['wu'], p['wd'], fnw, final)
    return out.reshape(b, t, D_MODEL), h3, s_new, k_rot


def kernel(x_prompt, x_sample, state_pool, state_conv, state_delta, cache_swa_k, cache_swa_v, meta_tokens, norm1_w, w_in, pool_w, pool_scale, dn_conv_w, dn_a_log, dn_dt_bias, dn_onorm_w, swa_sinks, proj_a, proj_b, proj_c, w_out, norm2_w, w_up, w_down, final_norm_w):
    depth = w_in.shape[0]
    bp, tp, _ = x_prompt.shape
    bs, ts, _ = x_sample.shape
    past = cache_swa_k.shape[2]
    fnw = final_norm_w[None].astype(F32)

    xm = meta_tokens[None].astype(F32)
    xp = x_prompt
    xs = x_sample
    outs = {k: [] for k in ('pool_p', 'conv_p', 'k_p', 'v_p', 'pool_s', 'conv_s', 'k_s', 'v_s')}
    delta_p = delta_s = None
    for l in range(depth):
        p = _prep_layer(l, norm1_w, w_in, pool_w, pool_scale, dn_conv_w, dn_a_log, dn_dt_bias, dn_onorm_w,
                        swa_sinks, proj_a, proj_b, proj_c, w_out, norm2_w, w_up, w_down)
        final = l == depth - 1

        xm, hm, s_m, k_m = _group_step(
            xm, p, fnw, final, 0,
            jnp.zeros((1, POOL_HALO, POOL_WIDTH), F32), jnp.zeros((1, CONV_HALO, CONV_CH), F32),
            (jnp.zeros((1, 1, DN_HEADS, DN_DK, DN_DV), F32), 0), 'step', (None, None),
            (1, N_META), (1, N_META, N_META))

        meta_kv = jnp.zeros((SWA_WINDOW, 256), F32)
        meta_kv = meta_kv.at[SWA_WINDOW - N_META:, :SWA_KV_WIDTH].set(k_m[0])
        meta_kv = meta_kv.at[SWA_WINDOW - N_META:, SWA_KV_WIDTH:].set(hm[0, :, KVC_OFF + SWA_KV_WIDTH:KVC_OFF + 256])
        xp, hp, s_p, k_p = _group_step(
            xp, p, fnw, final, N_META,
            jnp.broadcast_to(hm[:, :, U_OFF:U_OFF + POOL_WIDTH], (bp, POOL_HALO, POOL_WIDTH)),
            jnp.broadcast_to(hm[:, N_META - CONV_HALO:, QKV_OFF:QKV_OFF + CONV_CH], (bp, CONV_HALO, CONV_CH)),
            (jnp.broadcast_to(s_m[None], (1, bp, DN_HEADS, DN_DK, DN_DV)), 0), 'prompt', meta_kv,
            (1, 512), (1, 2 * DN_CHUNK, DN_CHUNK), s_stack=(l, depth, delta_p))
        delta_p = s_p
        outs['pool_p'].append(hp[:, tp - POOL_BUF:, U_OFF:U_OFF + POOL_WIDTH])
        outs['conv_p'].append(hp[:, tp - (CONV_W - 1):, QKV_OFF:QKV_OFF + CONV_CH])
        outs['k_p'].append(k_p.reshape(bp, SWA_WINDOW, SWA_KV_HEADS, SWA_HD))
        outs['v_p'].append(hp[:, tp - SWA_WINDOW:, KVC_OFF + SWA_KV_WIDTH:KVC_OFF + 256]
                           .reshape(bp, SWA_WINDOW, SWA_KV_HEADS, SWA_HD))

        kc = cache_swa_k[l].reshape(bs, past, SWA_KV_WIDTH)
        vc = cache_swa_v[l].reshape(bs, past, SWA_KV_WIDTH)
        xs, hs, s_s, k_s = _group_step(
            xs, p, fnw, final, PAST_LEN,
            jnp.concatenate([jnp.zeros((bs, POOL_HALO - POOL_BUF, POOL_WIDTH), F32), state_pool[l]], axis=1),
            jnp.concatenate([jnp.zeros((bs, CONV_HALO - (CONV_W - 1), CONV_CH), F32), state_conv[l]], axis=1),
            (state_delta, l), 'step', (kc, vc),
            (bs, ts), (16, ts, ts), 16, s_stack=(l, depth, delta_s))
        delta_s = s_s
        outs['pool_s'].append(jnp.concatenate([state_pool[l], hs[:, :, U_OFF:U_OFF + POOL_WIDTH]], axis=1)[:, -POOL_BUF:])
        outs['conv_s'].append(jnp.concatenate([state_conv[l], hs[:, :, QKV_OFF:QKV_OFF + CONV_CH]], axis=1)[:, -(CONV_W - 1):])
        v_new = hs[:, :, KVC_OFF + SWA_KV_WIDTH:KVC_OFF + 256]
        outs['k_s'].append(jnp.concatenate([kc, k_s], axis=1)[:, -past:].reshape(bs, past, SWA_KV_HEADS, SWA_HD))
        outs['v_s'].append(jnp.concatenate([vc, v_new], axis=1)[:, -past:].reshape(bs, past, SWA_KV_HEADS, SWA_HD))

    st = {k: jnp.stack(v) for k, v in outs.items()}
    return (xp, xs, st['pool_p'], st['conv_p'], delta_p, st['k_p'], st['v_p'],
            st['pool_s'], st['conv_s'], delta_s, st['k_s'], st['v_s'])
```

```python
import functools
import math

import jax
import jax.numpy as jnp
from jax import lax
from jax.experimental import pallas as pl
from jax.experimental.pallas import tpu as pltpu

F32 = jnp.float32
BF16 = jnp.bfloat16

D_MODEL = 1024
N_META = 16
EPS = 1e-6
POOL_GROUPS = 4
POOL_GROUP_DIM = 128
POOL_WIDTH = 512
POOL_WINDOWS = (2, 4, 8, 16)
POOL_BUF = 15
POOL_HALO = 16
DN_HEADS = 4
DN_DK = 128
DN_DV = 128
DN_QK = 512
DN_VW = 512
CONV_W = 4
CONV_CH = 1536
CONV_HALO = 8
DN_CHUNK = 64
SWA_HEADS = 8
SWA_KV_HEADS = 2
SWA_GROUP = 4
SWA_HD = 64
SWA_WIDTH = 512
SWA_KV_WIDTH = 128
SWA_WINDOW = 128
ROPE_THETA = 10000.0
D_FF = 4096
PAST_LEN = 16384

G_OFF = 0
QKV_OFF = 3072
U_OFF = 4608
Z_OFF = 5120
QC_OFF = 5632
KVC_OFF = 6144
BA_OFF = 6400
H_WIDTH = 6656
IN_TN = 1664

VMEM_LIMIT = 56 * 1024 * 1024
NEG_BIG = -1e30


def _cparams(sem):
    return pltpu.CompilerParams(dimension_semantics=sem, vmem_limit_bytes=VMEM_LIMIT)


def _rms(x, w):
    return x * lax.rsqrt(jnp.mean(x * x, axis=-1, keepdims=True) + EPS) * w


def _inproj_kernel(x_ref, nw_ref, w_ref, o_ref):
    xn = _rms(x_ref[...], nw_ref[...]).astype(BF16)
    for j in range(H_WIDTH // IN_TN):
        cs = slice(j * IN_TN, (j + 1) * IN_TN)
        o_ref[:, cs] = jnp.dot(xn, w_ref[:, cs], preferred_element_type=F32).astype(o_ref.dtype)


def _inproj(x, nw, w, act):
    m = x.shape[0]
    tm = min(m, 512 if act == BF16 else 256)
    assert m % tm == 0
    return pl.pallas_call(
        _inproj_kernel,
        grid=(m // tm,),
        in_specs=[
            pl.BlockSpec((tm, D_MODEL), lambda i: (i, 0)),
            pl.BlockSpec((1, D_MODEL), lambda i: (0, 0)),
            pl.BlockSpec((D_MODEL, H_WIDTH), lambda i: (0, 0), pipeline_mode=pl.Buffered(1)),
        ],
        out_specs=pl.BlockSpec((tm, H_WIDTH), lambda i: (i, 0)),
        out_shape=jax.ShapeDtypeStruct((m, H_WIDTH), act),
        compiler_params=_cparams(("parallel",)),
        name="inproj",
    )(x, nw, w)


def _pool_kernel(u_ref, init_ref, pw_ref, ps_ref, o_ref, ext_ref, *, pos0, tt):
    t = pl.program_id(1)
    bb = u_ref.shape[0]

    @pl.when(t == 0)
    def _():
        ext_ref[:, 0:POOL_HALO, :] = init_ref[...]

    @pl.when(t > 0)
    def _():
        ext_ref[:, 0:POOL_HALO, :] = ext_ref[:, tt:tt + POOL_HALO, :]

    ext_ref[:, POOL_HALO:POOL_HALO + tt, :] = u_ref[...].astype(F32)

    pos = pos0 + t * tt + lax.broadcasted_iota(jnp.int32, (1, tt, 1), 1)
    for g, w in enumerate(POOL_WINDOWS):
        cs = slice(g * POOL_GROUP_DIM, (g + 1) * POOL_GROUP_DIM)
        x = ext_ref[:, POOL_HALO:POOL_HALO + tt, cs]
        s = x
        for k in range(1, w):
            s = s + ext_ref[:, POOL_HALO - k:POOL_HALO - k + tt, cs]
        cnt = jnp.minimum(w, pos + 1).astype(F32)
        d = (s / cnt - x).reshape(bb * tt, POOL_GROUP_DIM)
        y = jnp.dot(d.astype(BF16), pw_ref[g], preferred_element_type=F32)
        o_ref[:, :, cs] = (y * ps_ref[:, cs]).reshape(bb, tt, POOL_GROUP_DIM).astype(o_ref.dtype)


def _pool(h3, init, pw, ps, pos0, bb, tt):
    b, t, _ = h3.shape
    return pl.pallas_call(
        functools.partial(_pool_kernel, pos0=pos0, tt=tt),
        grid=(b // bb, t // tt),
        in_specs=[
            pl.BlockSpec((bb, tt, POOL_WIDTH), lambda i, j: (i, j, U_OFF // POOL_WIDTH)),
            pl.BlockSpec((bb, POOL_HALO, POOL_WIDTH), lambda i, j: (i, 0, 0)),
            pl.BlockSpec((POOL_GROUPS, POOL_GROUP_DIM, POOL_GROUP_DIM), lambda i, j: (0, 0, 0)),
            pl.BlockSpec((1, POOL_WIDTH), lambda i, j: (0, 0)),
        ],
        out_specs=pl.BlockSpec((bb, tt, POOL_WIDTH), lambda i, j: (i, j, 0)),
        out_shape=jax.ShapeDtypeStruct((b, t, POOL_WIDTH), h3.dtype),
        scratch_shapes=[pltpu.VMEM((bb, POOL_HALO + tt, POOL_WIDTH), F32)],
        compiler_params=_cparams(("parallel", "arbitrary")),
        name="pool",
    )(h3, init, pw, ps)


def _softplus(x):
    return jnp.maximum(x, 0.0) + jnp.log1p(jnp.exp(-jnp.abs(x)))


def _dot_nt(a, b, **kw):
    return lax.dot_general(a, b, (((1,), (1,)), ((), ())), preferred_element_type=F32, **kw)


def _dot_tn(a, b):
    return lax.dot_general(a, b, (((0,), (0,)), ((), ())), preferred_element_type=F32)


def _dot(a, b):
    return jnp.dot(a, b, preferred_element_type=F32)


def _delta_kernel(qkv_ref, z_ref, ba_ref, cinit_ref, s0_ref, cw_ref, alog_ref, dtb_ref, onw_ref, *rest,
                  bb, tb, chunk):
    o_ref, s_ref, ext_ref = rest[-3:]
    t = pl.program_id(1)
    rows = bb * tb
    cpb = tb // chunk
    heads = range(DN_HEADS)
    mm = BF16 if rows >= 16 else F32
    mc = BF16 if chunk >= 16 else F32

    @pl.when(t == 0)
    def _():
        ext_ref[:, 0:CONV_HALO, :] = cinit_ref[...]
        s_ref[...] = s0_ref[...]

    @pl.when(t > 0)
    def _():
        ext_ref[:, 0:CONV_HALO, :] = ext_ref[:, tb:tb + CONV_HALO, :]

    ext_ref[:, CONV_HALO:CONV_HALO + tb, :] = qkv_ref[...].astype(F32)

    base = CONV_HALO - (CONV_W - 1)
    y = ext_ref[:, base:base + tb, :] * cw_ref[0:1, :]
    for j in range(1, CONV_W):
        y = y + ext_ref[:, base + j:base + j + tb, :] * cw_ref[j:j + 1, :]
    y = (y * jax.nn.sigmoid(y)).reshape(rows, CONV_CH)

    bav = ba_ref[...].astype(F32).reshape(rows, 128)
    zv = z_ref[...].astype(F32).reshape(rows, DN_VW)
    beta_full = jax.nn.sigmoid(bav)
    g_full = -jnp.exp(alog_ref[...]) * _softplus(bav + dtb_ref[...])

    row = lax.broadcasted_iota(jnp.int32, (rows, rows), 0)
    col = lax.broadcasted_iota(jnp.int32, (rows, rows), 1)
    shift = int(math.log2(chunk))
    same = (row >> shift) == (col >> shift)
    tri = same & (row >= col)
    strict = same & (row > col)

    gcum = jnp.dot(tri.astype(F32), g_full, preferred_element_type=F32, precision=lax.Precision.HIGHEST)

    lane = lax.broadcasted_iota(jnp.int32, (rows, 128), 1)
    gcol = [jnp.sum(jnp.where(lane == DN_HEADS + h, gcum, 0.0), axis=1, keepdims=True) for h in heads]
    bcol = [jnp.sum(jnp.where(lane == h, beta_full, 0.0), axis=1, keepdims=True) for h in heads]
    if rows % 128 == 0:
        gcum_t = gcum.T
        grow = [gcum_t[DN_HEADS + h:DN_HEADS + h + 1, :] for h in heads]
    else:
        ones = jnp.ones((rows, 128), F32)
        grow = [_dot_nt(ones, jnp.where(lane == DN_HEADS + h, gcum, 0.0), precision=lax.Precision.HIGHEST)
                for h in heads]
    decay = [jnp.where(tri, jnp.exp(jnp.where(tri, gcol[h] - grow[h], 0.0)), 0.0) for h in heads]
    eg = [jnp.exp(gcol[h]) for h in heads]

    def l2n(v):
        return v * lax.rsqrt(jnp.sum(v * v, axis=-1, keepdims=True) + EPS)

    qn = [l2n(y[:, h * DN_DK:(h + 1) * DN_DK]) * (DN_DK ** -0.5) for h in heads]
    kn = [l2n(y[:, DN_QK + h * DN_DK:DN_QK + (h + 1) * DN_DK]) for h in heads]
    vh = [y[:, 2 * DN_QK + h * DN_DV:2 * DN_QK + (h + 1) * DN_DV] for h in heads]
    kb = [kn[h].astype(mm) for h in heads]
    kk = [_dot_nt(kb[h], kb[h]) for h in heads]
    qk = [_dot_nt(qn[h].astype(mm), kb[h]) for h in heads]

    m = [jnp.where(strict, kk[h] * decay[h] * bcol[h], 0.0) for h in heads]
    bshift = min(shift, 4)
    p = [jnp.where((row >> bshift) == (col >> bshift), -m[h], 0.0) for h in heads]
    xp = p
    for _ in range(bshift - 1):
        xb = [xp[h].astype(mm) for h in heads]
        xp = [_dot(xb[h], xb[h]) for h in heads]
        p = [p[h] + xp[h] + _dot(p[h].astype(mm), xp[h].astype(mm)) for h in heads]
    for s in range(bshift, shift):
        lower = ((row >> (s + 1)) == (col >> (s + 1))) & ((row >> s) > (col >> s))
        c_blk = [jnp.where(lower, m[h], 0.0) for h in heads]
        pb = [p[h].astype(mm) for h in heads]
        a = [c_blk[h] + _dot(pb[h], c_blk[h].astype(mm)) for h in heads]
        p = [p[h] - a[h] - _dot(a[h].astype(mm), pb[h]) for h in heads]
    rhs = [jnp.concatenate([vh[h] * bcol[h], kn[h] * (bcol[h] * eg[h])], axis=1) for h in heads]
    uw = [rhs[h] + _dot(p[h].astype(mm), rhs[h].astype(mm)) for h in heads]
    qkd = [(qk[h] * decay[h]).astype(mm) for h in heads]
    qg = [qn[h] * eg[h] for h in heads]

    pairs = [(b, h) for b in range(bb) for h in heads]
    state = {bh: s_ref[bh[0], bh[1]] for bh in pairs}
    dlt = {h: [] for h in heads}
    oq = {h: [] for h in heads}
    for c in range(cpb):
        def rs(b):
            return slice(b * tb + c * chunk, b * tb + (c + 1) * chunk)
        prod = {}
        for b, h in pairs:
            lhs = jnp.concatenate([uw[h][rs(b), DN_DV:], qg[h][rs(b)]], axis=0).astype(mc)
            prod[b, h] = _dot(lhs, state[b, h].astype(mc))
        for b, h in pairs:
            r = rs(b)
            d = uw[h][r, :DN_DV] - prod[b, h][:chunk]
            glast = gcol[h][r.stop - 1:r.stop, :]
            kg = kn[h][r] * jnp.exp(glast - gcol[h][r])
            state[b, h] = state[b, h] * jnp.exp(glast) + _dot_tn(kg.astype(mc), d.astype(mc))
            dlt[h].append((r.start, d))
            oq[h].append((r.start, prod[b, h][chunk:]))
    for b, h in pairs:
        s_ref[b, h] = state[b, h]

    def stack(parts):
        parts = [v for _, v in sorted(parts, key=lambda sv: sv[0])]
        return parts[0] if len(parts) == 1 else jnp.concatenate(parts, axis=0)

    for h in heads:
        hs = slice(h * DN_DV, (h + 1) * DN_DV)
        o = stack(oq[h]) + _dot(qkd[h], stack(dlt[h]).astype(mm))
        zh = zv[:, hs]
        gated = _rms(o, onw_ref[...]) * (zh * jax.nn.sigmoid(zh))
        o_ref[:, :, hs] = gated.reshape(bb, tb, DN_DV).astype(o_ref.dtype)


def _delta(h3, cinit, s0, s0_layer, cw, alog, dtb, onw, bb, tb, chunk, stack=None):
    b, t, _ = h3.shape
    assert b % bb == 0 and t % tb == 0 and tb % chunk == 0
    state_block = (None, bb, DN_HEADS, DN_DK, DN_DV)
    in_specs = [
        pl.BlockSpec((bb, tb, CONV_CH), lambda i, j: (i, j, QKV_OFF // CONV_CH)),
        pl.BlockSpec((bb, tb, DN_VW), lambda i, j: (i, j, Z_OFF // DN_VW)),
        pl.BlockSpec((bb, tb, 128), lambda i, j: (i, j, BA_OFF // 128)),
        pl.BlockSpec((bb, CONV_HALO, CONV_CH), lambda i, j: (i, 0, 0)),
        pl.BlockSpec(state_block, lambda i, j: (s0_layer, i, 0, 0, 0)),
        pl.BlockSpec((CONV_W, CONV_CH), lambda i, j: (0, 0)),
        pl.BlockSpec((1, 128), lambda i, j: (0, 0)),
        pl.BlockSpec((1, 128), lambda i, j: (0, 0)),
        pl.BlockSpec((1, DN_DV), lambda i, j: (0, 0)),
    ]
    args = [h3, h3, h3, cinit, s0, cw, alog, dtb, onw]
    layer, depth, prev = (0, 1, None) if stack is None else stack
    aliases = {}
    if prev is not None:
        in_specs.append(pl.BlockSpec(memory_space=pl.ANY))
        aliases = {len(args): 1}
        args.append(prev)
    o, s = pl.pallas_call(
        functools.partial(_delta_kernel, bb=bb, tb=tb, chunk=chunk),
        grid=(b // bb, t // tb),
        in_specs=in_specs,
        out_specs=[
            pl.BlockSpec((bb, tb, DN_VW), lambda i, j: (i, j, 0)),
            pl.BlockSpec(state_block, lambda i, j: (layer, i, 0, 0, 0)),
        ],
        out_shape=[
            jax.ShapeDtypeStruct((b, t, DN_VW), h3.dtype),
            jax.ShapeDtypeStruct((depth, b, DN_HEADS, DN_DK, DN_DV), F32),
        ],
        scratch_shapes=[pltpu.VMEM((bb, CONV_HALO + tb, CONV_CH), F32)],
        input_output_aliases=aliases,
        compiler_params=_cparams(("parallel", "arbitrary")),
        name="delta",
    )(*args)
    return o, (s[0] if stack is None else s)


def _rope(x, cos, sin):
    width = x.shape[-1]
    reps = width // cos.shape[-1]
    if reps > 1:
        cos = jnp.concatenate([cos] * reps, axis=1)
        sin = jnp.concatenate([sin] * reps, axis=1)
    lane = lax.broadcasted_iota(jnp.int32, x.shape, 1)
    first_half = (lane & (SWA_HD - 1)) < (SWA_HD // 2)
    other = jnp.where(first_half, pltpu.roll(x, width - SWA_HD // 2, 1), pltpu.roll(x, SWA_HD // 2, 1))
    return x * cos + other * sin


def _pad_heads(x, lo):
    xr = pltpu.roll(x, SWA_HD, 1)
    z = jnp.zeros_like(x)
    return [jnp.where(lo, x, z), jnp.where(lo, z, xr), jnp.where(lo, xr, z), jnp.where(lo, z, x)]


def _swa_prompt_kernel(sinks_ref, q_ref, kv_ref, meta_ref, cos_ref, sin_ref, o_ref, klast_ref, kcat_ref, vcat_ref,
                       *, pos0):
    n = pl.program_id(1)
    blk = SWA_WINDOW
    lo = lax.broadcasted_iota(jnp.int32, (blk, 128), 1) < SWA_HD

    def put(half, k, v):
        rows = slice(half * blk, (half + 1) * blk)
        for i, (a, b) in enumerate(zip(_pad_heads(k, lo), _pad_heads(v, lo))):
            kcat_ref[i, rows, :] = a.astype(BF16)
            vcat_ref[i, rows, :] = b.astype(BF16)

    @pl.when(n == 0)
    def _():
        put(0, meta_ref[:, :SWA_KV_WIDTH], meta_ref[:, SWA_KV_WIDTH:])

    @pl.when(n > 0)
    def _():
        kcat_ref[:, 0:blk, :] = kcat_ref[:, blk:2 * blk, :]
        vcat_ref[:, 0:blk, :] = vcat_ref[:, blk:2 * blk, :]

    cos = cos_ref[...]
    sin = sin_ref[...]
    q = _rope(q_ref[0].astype(F32), cos, sin) * (SWA_HD ** -0.5)
    kv = kv_ref[0].astype(F32)
    k = _rope(kv[:, :SWA_KV_WIDTH], cos, sin)
    klast_ref[0] = k
    put(1, k, kv[:, SWA_KV_WIDTH:])

    r = lax.broadcasted_iota(jnp.int32, (blk, 2 * blk), 0)
    j = lax.broadcasted_iota(jnp.int32, (blk, 2 * blk), 1)
    jmin = jnp.maximum(blk - pos0 - n * blk, 0)
    mask = ((j < blk) & (j > r) & (j >= jmin)) | ((j >= blk) & ((j - blk) <= r))
    heads = range(SWA_HEADS)
    qt = [q[:, i * 128:(i + 1) * 128].astype(BF16) for i in range(SWA_HEADS // 2)]
    s = [jnp.where(mask, _dot_nt(qt[h // 2], kcat_ref[2 * (h // SWA_GROUP) + h % 2]), NEG_BIG) for h in heads]
    m = [jnp.maximum(jnp.max(s[h], axis=1, keepdims=True), sinks_ref[h]) for h in heads]
    e = [jnp.exp(s[h] - m[h]).astype(BF16) for h in heads]
    ones = jnp.ones((2 * blk, 128), BF16)
    den = [_dot(e[h], ones) + jnp.exp(sinks_ref[h] - m[h]) for h in heads]
    for i in range(SWA_HEADS // 2):
        he, ho = 2 * i, 2 * i + 1
        g = he // SWA_GROUP
        acc = _dot(e[he], vcat_ref[2 * g]) + _dot(e[ho], vcat_ref[2 * g + 1])
        o_ref[0, :, i * 128:(i + 1) * 128] = (acc / jnp.where(lo, den[he], den[ho])).astype(o_ref.dtype)


def _swa_prompt(h3, meta_kv, cos, sin, sinks, pos0):
    b, t, _ = h3.shape
    blk = SWA_WINDOW
    assert t % blk == 0
    return pl.pallas_call(
        functools.partial(_swa_prompt_kernel, pos0=pos0),
        grid=(b, t // blk),
        in_specs=[
            pl.BlockSpec(memory_space=pltpu.SMEM),
            pl.BlockSpec((1, blk, SWA_WIDTH), lambda i, n: (i, n, QC_OFF // SWA_WIDTH)),
            pl.BlockSpec((1, blk, 256), lambda i, n: (i, n, KVC_OFF // 256)),
            pl.BlockSpec((blk, 256), lambda i, n: (0, 0)),
            pl.BlockSpec((blk, 128), lambda i, n: (n, 0)),
            pl.BlockSpec((blk, 128), lambda i, n: (n, 0)),
        ],
        out_specs=[
            pl.BlockSpec((1, blk, SWA_WIDTH), lambda i, n: (i, n, 0)),
            pl.BlockSpec((1, blk, SWA_KV_WIDTH), lambda i, n: (i, 0, 0)),
        ],
        out_shape=[
            jax.ShapeDtypeStruct((b, t, SWA_WIDTH), h3.dtype),
            jax.ShapeDtypeStruct((b, blk, SWA_KV_WIDTH), F32),
        ],
        scratch_shapes=[pltpu.VMEM((4, 2 * blk, 128), BF16), pltpu.VMEM((4, 2 * blk, 128), BF16)],
        compiler_params=_cparams(("parallel", "arbitrary")),
        name="swa_prompt",
    )(sinks, h3, h3, meta_kv, cos, sin)


def _swa_step_kernel(sink_ref, q_ref, kv_ref, *rest, bb, t_new, window):
    if window:
        kc_ref, vc_ref, cos_ref, sin_ref, o_ref, knew_ref = rest
    else:
        cos_ref, sin_ref, o_ref, knew_ref = rest
    cos = cos_ref[...]
    sin = sin_ref[...]
    rows = SWA_HEADS * t_new
    mn = BF16 if t_new >= 16 else F32
    lo = lax.broadcasted_iota(jnp.int32, (t_new, 128), 1) < SWA_HD
    tq = lax.broadcasted_iota(jnp.int32, (rows, t_new), 0) & (t_new - 1)
    mask_n = lax.broadcasted_iota(jnp.int32, (rows, t_new), 1) <= tq
    if window:
        diff = ((lax.broadcasted_iota(jnp.int32, (rows, window), 0) & (t_new - 1)) + window
                - lax.broadcasted_iota(jnp.int32, (rows, window), 1))
        mask_c = (diff >= 0) & (diff < SWA_WINDOW)
    sink = sink_ref[...]
    seqs = range(bb)

    qall, kk, vv = [], [], []
    for s in seqs:
        q = _rope(q_ref[s].astype(F32), cos, sin) * (SWA_HD ** -0.5)
        kv = kv_ref[s].astype(F32)
        k = _rope(kv[:, :SWA_KV_WIDTH], cos, sin)
        knew_ref[s] = k
        pieces = []
        for h in range(SWA_HEADS):
            g = h // SWA_GROUP
            tile = q[:, (h // 2) * 128:(h // 2 + 1) * 128]
            if h % 2 != g:
                tile = pltpu.roll(tile, SWA_HD, 1)
            pieces.append(jnp.where(lo, tile, 0.0) if g == 0 else jnp.where(lo, 0.0, tile))
        qall.append(jnp.concatenate(pieces, axis=0))
        kk.append(k)
        vv.append(kv[:, SWA_KV_WIDTH:])

    s_n = [jnp.where(mask_n, _dot_nt(qall[s].astype(mn), kk[s].astype(mn)), NEG_BIG) for s in seqs]
    m = [jnp.maximum(jnp.max(s_n[s], axis=1, keepdims=True), sink) for s in seqs]
    if window:
        s_c = [jnp.where(mask_c, _dot_nt(qall[s].astype(BF16), kc_ref[s].astype(BF16)), NEG_BIG) for s in seqs]
        m = [jnp.maximum(m[s], jnp.max(s_c[s], axis=1, keepdims=True)) for s in seqs]
    e_n = [jnp.exp(s_n[s] - m[s]) for s in seqs]
    den = [jnp.sum(e_n[s], axis=1, keepdims=True) + jnp.exp(sink - m[s]) for s in seqs]
    acc = [_dot(e_n[s].astype(mn), vv[s].astype(mn)) for s in seqs]
    if window:
        e_c = [jnp.exp(s_c[s] - m[s]) for s in seqs]
        den = [den[s] + jnp.sum(e_c[s], axis=1, keepdims=True) for s in seqs]
        acc = [acc[s] + _dot(e_c[s].astype(BF16), vc_ref[s].astype(BF16)) for s in seqs]
    for s in seqs:
        a = acc[s] / den[s]
        for i in range(SWA_HEADS // 2):
            he, ho = 2 * i, 2 * i + 1
            g = he // SWA_GROUP
            a_e = a[he * t_new:(he + 1) * t_new]
            a_o = a[ho * t_new:(ho + 1) * t_new]
            if g == 1:
                a_e = pltpu.roll(a_e, SWA_HD, 1)
            else:
                a_o = pltpu.roll(a_o, SWA_HD, 1)
            o_ref[s, :, i * 128:(i + 1) * 128] = jnp.where(lo, a_e, a_o).astype(o_ref.dtype)


def _swa_step(h3, kc, vc, cos, sin, sinks, bb):
    b, t, _ = h3.shape
    assert b % bb == 0
    window = 0 if kc is None else kc.shape[1]
    sink_rows = jnp.repeat(sinks, t)[:, None]
    in_specs = [
        pl.BlockSpec((SWA_HEADS * t, 1), lambda i: (0, 0)),
        pl.BlockSpec((bb, t, SWA_WIDTH), lambda i: (i, 0, QC_OFF // SWA_WIDTH)),
        pl.BlockSpec((bb, t, 256), lambda i: (i, 0, KVC_OFF // 256)),
    ]
    args = [sink_rows, h3, h3]
    if window:
        in_specs += [pl.BlockSpec((bb, window, SWA_KV_WIDTH), lambda i: (i, 0, 0))] * 2
        args += [kc, vc]
    in_specs += [pl.BlockSpec((t, 128), lambda i: (0, 0))] * 2
    args += [cos, sin]
    return pl.pallas_call(
        functools.partial(_swa_step_kernel, bb=bb, t_new=t, window=window),
        grid=(b // bb,),
        in_specs=in_specs,
        out_specs=[
            pl.BlockSpec((bb, t, SWA_WIDTH), lambda i: (i, 0, 0)),
            pl.BlockSpec((bb, t, SWA_KV_WIDTH), lambda i: (i, 0, 0)),
        ],
        out_shape=[
            jax.ShapeDtypeStruct((b, t, SWA_WIDTH), h3.dtype),
            jax.ShapeDtypeStruct((b, t, SWA_KV_WIDTH), F32),
        ],
        compiler_params=_cparams(("parallel",)),
        name="swa_step",
    )(*args)


def _merge_kernel(oa_ref, ob_ref, oc_ref, ga_ref, gb_ref, gc_ref, x_ref, pa_ref, pb_ref, pc_ref, wo_ref, o_ref):
    def branch(o_r, g_r, p_r):
        return jax.nn.sigmoid(g_r[...].astype(F32)) * jnp.dot(o_r[...].astype(BF16), p_r[...], preferred_element_type=F32)

    m = branch(oa_ref, ga_ref, pa_ref) + branch(ob_ref, gb_ref, pb_ref) + branch(oc_ref, gc_ref, pc_ref)
    o_ref[...] = x_ref[...] + jnp.dot(m.astype(BF16), wo_ref[...], preferred_element_type=F32)


def _merge(oa, ob, oc, h, x, pa, pb, pc, wo):
    m = x.shape[0]
    tm = min(m, 512)
    assert m % tm == 0
    row =lambda w: pl.BlockSpec((tm, w), lambda i: (i, 0))
    gate = lambda c: pl.BlockSpec((tm, D_MODEL), lambda i: (i, c))
    full = lambda a: pl.BlockSpec(a.shape, lambda i: (0, 0))
    return pl.pallas_call(
        _merge_kernel,
        grid=(m // tm,),
        in_specs=[row(512), row(512), row(512), gate(0), gate(1), gate(2), row(D_MODEL),
                  full(pa), full(pb), full(pc), full(wo)],
        out_specs=row(D_MODEL),
        out_shape=jax.ShapeDtypeStruct((m, D_MODEL), F32),
        compiler_params=_cparams(("parallel",)),
        name="merge",
    )(oa, ob, oc, h, h, h, x, pa, pb, pc, wo)


def _mlp_kernel(x_ref, nw_ref, wu_ref, wd_ref, fnw_ref, o_ref, xn_ref, acc_ref, *, final):
    f = pl.program_id(1)

    @pl.when(f == 0)
    def _():
        x = x_ref[...]
        xn_ref[...] = _rms(x, nw_ref[...]).astype(BF16)
        acc_ref[...] = x

    a = jnp.maximum(jnp.dot(xn_ref[...], wu_ref[...], preferred_element_type=F32), 0.0)
    acc_ref[...] += jnp.dot((a * a).astype(BF16), wd_ref[...], preferred_element_type=F32)

    @pl.when(f == pl.num_programs(1) - 1)
    def _():
        y = acc_ref[...]
        o_ref[...] = _rms(y, fnw_ref[...]) if final else y


def _mlp(x, nw, wu, wd, fnw, final):
    m = x.shape[0]
    tm = min(m, 1024)
    assert m % tm == 0
    tf = 1024
    return pl.pallas_call(
        functools.partial(_mlp_kernel, final=final),
        grid=(m // tm, D_FF // tf),
        in_specs=[
            pl.BlockSpec((tm, D_MODEL), lambda i, f: (i, 0)),
            pl.BlockSpec((1, D_MODEL), lambda i, f: (0, 0)),
            pl.BlockSpec((D_MODEL, tf), lambda i, f: (0, f)),
            pl.BlockSpec((tf, D_MODEL), lambda i, f: (f, 0)),
            pl.BlockSpec((1, D_MODEL), lambda i, f: (0, 0)),
        ],
        out_specs=pl.BlockSpec((tm, D_MODEL), lambda i, f: (i, 0)),
        out_shape=jax.ShapeDtypeStruct((m, D_MODEL), F32),
        scratch_shapes=[pltpu.VMEM((tm, D_MODEL), BF16), pltpu.VMEM((tm, D_MODEL), F32)],
        compiler_params=_cparams(("parallel", "arbitrary")),
        name="mlp",
    )(x, nw, wu, wd, fnw)


def _rope_tables(pos0, t):
    half = SWA_HD // 2
    inv = ROPE_THETA ** (-jnp.arange(half, dtype=F32) / half)
    ang = (pos0 + jnp.arange(t)).astype(F32)[:, None] * inv[None, :]
    cos = jnp.cos(ang)
    sin = jnp.sin(ang)
    return jnp.tile(cos, (1, 4)), jnp.tile(jnp.concatenate([-sin, sin], axis=1), (1, 2))


def _prep_layer(l, norm1_w, w_in, pool_w, pool_scale, dn_conv_w, dn_a_log, dn_dt_bias, dn_onorm_w, swa_sinks,
                proj_a, proj_b, proj_c, w_out, norm2_w, w_up, w_down):
    w = w_in[l]
    segments = ((3336, 6408), (512, 2048), (0, 512), (2048, 2560), (2568, 3080), (3080, 3336), (2560, 2568))
    w_perm = jnp.concatenate(
        [w[:, a:b].astype(BF16) for a, b in segments]
        + [jnp.zeros((D_MODEL, H_WIDTH - BA_OFF - 2 * DN_HEADS), BF16)], axis=1)
    lane_pad = lambda v: jnp.zeros((1, 128), F32).at[0, DN_HEADS:2 * DN_HEADS].set(v.astype(F32))
    return dict(
        norm1=norm1_w[l][None].astype(F32), w_in=w_perm,
        pool_w=pool_w[l].astype(BF16), pool_scale=pool_scale[l][None].astype(F32),
        conv_w=dn_conv_w[l].astype(F32), alog=lane_pad(dn_a_log[l]), dtb=lane_pad(dn_dt_bias[l]),
        onw=dn_onorm_w[l][None].astype(F32), sinks=swa_sinks[l].astype(F32),
        pa=proj_a[l].astype(BF16), pb=proj_b[l].astype(BF16), pc=proj_c[l].astype(BF16),
        wo=w_out[l].astype(BF16), norm2=norm2_w[l][None].astype(F32),
        wu=w_up[l].astype(BF16), wd=w_down[l].astype(BF16))


def _group_step(x3, p, fnw, final, pos0, pool_init, conv_init, s0, kind, swa_extra, pool_tiles, delta_tiles,
                swa_bb=1, s_stack=None, act=F32):
    b, t, _ = x3.shape
    x2 = x3.reshape(b * t, D_MODEL)
    h = _inproj(x2, p['norm1'], p['w_in'], act)
    h3 = h.reshape(b, t, H_WIDTH)
    o_a = _pool(h3, pool_init, p['pool_w'], p['pool_scale'], pos0, *pool_tiles)
    o_b, s_new = _delta(h3, conv_init, s0[0], s0[1], p['conv_w'], p['alog'], p['dtb'], p['onw'], *delta_tiles,
                        stack=s_stack)
    cos, sin = _rope_tables(pos0, t)
    if kind == 'prompt':
        o_c, k_rot = _swa_prompt(h3, swa_extra, cos, sin, p['sinks'], pos0)
    else:
        kc, vc = swa_extra
        o_c, k_rot = _swa_step(h3, kc, vc, cos, sin, p['sinks'], swa_bb)
    m = b * t
    h1 = _merge(o_a.reshape(m, 512), o_b.reshape(m, 512), o_c.reshape(m, 512), h, x2,
                p['pa'], p['pb'], p['pc'], p['wo'])
    out = _mlp(h1, p['norm2'], p['wu'], p['wd'], fnw, final)
    return out.reshape(b, t, D_MODEL), h3, s_new, k_rot


def kernel(x_prompt, x_sample, state_pool, state_conv, state_delta, cache_swa_k, cache_swa_v, meta_tokens, norm1_w, w_in, pool_w, pool_scale, dn_conv_w, dn_a_log, dn_dt_bias, dn_onorm_w, swa_sinks, proj_a, proj_b, proj_c, w_out, norm2_w, w_up, w_down, final_norm_w):
    depth = w_in.shape[0]
    bp, tp, _ = x_prompt.shape
    bs, ts, _ = x_sample.shape
    past = cache_swa_k.shape[2]
    fnw = final_norm_w[None].astype(F32)

    xm = meta_tokens[None].astype(F32)
    xp = x_prompt
    xs = x_sample
    outs = {k: [] for k in ('pool_p', 'conv_p', 'k_p', 'v_p', 'pool_s', 'conv_s', 'k_s', 'v_s')}
    delta_p = delta_s = None
    for l in range(depth):
        p = _prep_layer(l, norm1_w, w_in, pool_w, pool_scale, dn_conv_w, dn_a_log, dn_dt_bias, dn_onorm_w,
                        swa_sinks, proj_a, proj_b, proj_c, w_out, norm2_w, w_up, w_down)
        final = l == depth - 1

        xm, hm, s_m, k_m = _group_step(
            xm, p, fnw, final, 0,
            jnp.zeros((1, POOL_HALO, POOL_WIDTH), F32), jnp.zeros((1, CONV_HALO, CONV_CH), F32),
            (jnp.zeros((1, 1, DN_HEADS, DN_DK, DN_DV), F32), 0), 'step', (None, None),
            (1, N_META), (1, N_META, N_META))

        meta_kv = jnp.zeros((SWA_WINDOW, 256), F32)
        meta_kv = meta_kv.at[SWA_WINDOW - N_META:, :SWA_KV_WIDTH].set(k_m[0])
        meta_kv = meta_kv.at[SWA_WINDOW - N_META:, SWA_KV_WIDTH:].set(hm[0, :, KVC_OFF + SWA_KV_WIDTH:KVC_OFF + 256])
        xp, hp, s_p, k_p = _group_step(
            xp, p, fnw, final, N_META,
            jnp.broadcast_to(hm[:, :, U_OFF:U_OFF + POOL_WIDTH], (bp, POOL_HALO, POOL_WIDTH)),
            jnp.broadcast_to(hm[:, N_META - CONV_HALO:, QKV_OFF:QKV_OFF + CONV_CH], (bp, CONV_HALO, CONV_CH)),
            (jnp.broadcast_to(s_m[None], (1, bp, DN_HEADS, DN_DK, DN_DV)), 0), 'prompt', meta_kv,
            (1, 512), (1, 2 * DN_CHUNK, DN_CHUNK), s_stack=(l, depth, delta_p), act=BF16)
        delta_p = s_p
        outs['pool_p'].append(hp[:, tp - POOL_BUF:, U_OFF:U_OFF + POOL_WIDTH].astype(F32))
        outs['conv_p'].append(hp[:, tp - (CONV_W - 1):, QKV_OFF:QKV_OFF + CONV_CH].astype(F32))
        outs['k_p'].append(k_p.reshape(bp, SWA_WINDOW, SWA_KV_HEADS, SWA_HD))
        outs['v_p'].append(hp[:, tp - SWA_WINDOW:, KVC_OFF + SWA_KV_WIDTH:KVC_OFF + 256].astype(F32)
                           .reshape(bp, SWA_WINDOW, SWA_KV_HEADS, SWA_HD))

        kc = cache_swa_k[l].reshape(bs, past, SWA_KV_WIDTH)
        vc = cache_swa_v[l].reshape(bs, past, SWA_KV_WIDTH)
        xs, hs, s_s, k_s = _group_step(
            xs, p, fnw, final, PAST_LEN,
            jnp.concatenate([jnp.zeros((bs, POOL_HALO - POOL_BUF, POOL_WIDTH), F32), state_pool[l]], axis=1),
            jnp.concatenate([jnp.zeros((bs, CONV_HALO - (CONV_W - 1), CONV_CH), F32), state_conv[l]], axis=1),
            (state_delta, l), 'step', (kc, vc),
            (bs, ts), (16, ts, ts), 16, s_stack=(l, depth, delta_s))
        delta_s = s_s
        outs['pool_s'].append(jnp.concatenate([state_pool[l], hs[:, :, U_OFF:U_OFF + POOL_WIDTH]], axis=1)[:, -POOL_BUF:])
        outs['conv_s'].append(jnp.concatenate([state_conv[l], hs[:, :, QKV_OFF:QKV_OFF + CONV_CH]], axis=1)[:, -(CONV_W - 1):])
        v_new = hs[:, :, KVC_OFF + SWA_KV_WIDTH:KVC_OFF + 256]
        outs['k_s'].append(jnp.concatenate([kc, k_s], axis=1)[:, -past:].reshape(bs, past, SWA_KV_HEADS, SWA_HD))
        outs['v_s'].append(jnp.concatenate([vc, v_new], axis=1)[:, -past:].reshape(bs, past, SWA_KV_HEADS, SWA_HD))

    st = {k: jnp.stack(v) for k, v in outs.items()}
    return (xp, xs, st['pool_p'], st['conv_p'], delta_p, st['k_p'], st['v_p'],
            st['pool_s'], st['conv_s'], delta_s, st['k_s'], st['v_s'])
```

```python
import functools
import math

import jax
import jax.numpy as jnp
from jax import lax
from jax.experimental import pallas as pl
from jax.experimental.pallas import tpu as pltpu

F32 = jnp.float32
BF16 = jnp.bfloat16

D_MODEL = 1024
N_META = 16
EPS = 1e-6
POOL_GROUPS = 4
POOL_GROUP_DIM = 128
POOL_WIDTH = 512
POOL_WINDOWS = (2, 4, 8, 16)
POOL_BUF = 15
POOL_HALO = 16
DN_HEADS = 4
DN_DK = 128
DN_DV = 128
DN_QK = 512
DN_VW = 512
CONV_W = 4
CONV_CH = 1536
CONV_HALO = 8
DN_CHUNK = 64
HEAD_GROUP = 4
SWA_HEADS = 8
SWA_KV_HEADS = 2
SWA_GROUP = 4
SWA_HD = 64
SWA_WIDTH = 512
SWA_KV_WIDTH = 128
SWA_WINDOW = 128
ROPE_THETA = 10000.0
D_FF = 4096
PAST_LEN = 16384

G_OFF = 0
QKV_OFF = 3072
U_OFF = 4608
Z_OFF = 5120
QC_OFF = 5632
KVC_OFF = 6144
BA_OFF = 6400
H_WIDTH = 6656
IN_CHUNKS = ((0, 1536), (1536, 3072), (3072, 4608), (4608, 6144), (6144, 6656))

VMEM_LIMIT = 56 * 1024 * 1024
NEG_BIG = -1e30


def _cparams(sem):
    return pltpu.CompilerParams(dimension_semantics=sem, vmem_limit_bytes=VMEM_LIMIT)


def _rms(x, w):
    return x * lax.rsqrt(jnp.mean(x * x, axis=-1, keepdims=True) + EPS) * w


def _l2n(v):
    return v * lax.rsqrt(jnp.sum(v * v, axis=-1, keepdims=True) + EPS)


def _short_conv(ext_ref, cw_ref, rows):
    base = CONV_HALO - (CONV_W - 1)
    y = ext_ref[..., base:base + rows, :] * cw_ref[0:1, :]
    for j in range(1, CONV_W):
        y = y + ext_ref[..., base + j:base + j + rows, :] * cw_ref[j:j + 1, :]
    return y * jax.nn.sigmoid(y)


def _qkv_heads(y):
    heads = range(DN_HEADS)
    qn = [_l2n(y[:, h * DN_DK:(h + 1) * DN_DK]) * (DN_DK ** -0.5) for h in heads]
    kn = [_l2n(y[:, DN_QK + h * DN_DK:DN_QK + (h + 1) * DN_DK]) for h in heads]
    vh = [y[:, 2 * DN_QK + h * DN_DV:2 * DN_QK + (h + 1) * DN_DV] for h in heads]
    return qn, kn, vh


def _inproj_kernel(x_ref, nw_ref, w_ref, *rest, tiles_per_seq):
    if tiles_per_seq:
        cinit_ref, cw_ref, o_ref, tail_ref, ext_ref = rest
    else:
        (o_ref,) = rest
    tm = x_ref.shape[0]
    if tiles_per_seq:
        first = pl.program_id(0) % tiles_per_seq == 0

        @pl.when(first)
        def _():
            ext_ref[0:CONV_HALO, :] = cinit_ref[0]

        @pl.when(jnp.logical_not(first))
        def _():
            ext_ref[0:CONV_HALO, :] = ext_ref[tm:tm + CONV_HALO, :]

    xn = _rms(x_ref[...], nw_ref[...]).astype(BF16)
    for a, b in sorted(IN_CHUNKS, key=lambda ab: ab[0] != QKV_OFF):
        acc = jnp.dot(xn, w_ref[:, a:b], preferred_element_type=F32)
        if tiles_per_seq and a == QKV_OFF:
            ext_ref[CONV_HALO:CONV_HALO + tm, :] = acc
            tail_ref[0] = acc[tm - CONV_HALO:, :]
            qn, kn, vh = _qkv_heads(_short_conv(ext_ref, cw_ref, tm))
            acc = jnp.concatenate(qn + kn + vh, axis=1)
        o_ref[:, a:b] = acc.astype(o_ref.dtype)


def _inproj(x, nw, w, act, conv=None):
    m = x.shape[0]
    tm = min(m, 512 if act == BF16 else 256)
    assert m % tm == 0
    in_specs = [
        pl.BlockSpec((tm, D_MODEL), lambda i: (i, 0)),
        pl.BlockSpec((1, D_MODEL), lambda i: (0, 0)),
        pl.BlockSpec((D_MODEL, H_WIDTH), lambda i: (0, 0), pipeline_mode=pl.Buffered(1)),
    ]
    out_specs = [pl.BlockSpec((tm, H_WIDTH), lambda i: (i, 0))]
    out_shape = [jax.ShapeDtypeStruct((m, H_WIDTH), act)]
    args = [x, nw, w]
    scratch = []
    tiles_per_seq = 0
    if conv is not None:
        cinit, cw, seq_rows = conv
        assert seq_rows % tm == 0
        tiles_per_seq = seq_rows // tm
        in_specs += [pl.BlockSpec((1, CONV_HALO, CONV_CH), lambda i: (i // tiles_per_seq, 0, 0)),
                     pl.BlockSpec((CONV_W, CONV_CH), lambda i: (0, 0))]
        args += [cinit, cw]
        out_specs.append(pl.BlockSpec((1, CONV_HALO, CONV_CH), lambda i: (i, 0, 0)))
        out_shape.append(jax.ShapeDtypeStruct((m // tm, CONV_HALO, CONV_CH), F32))
        scratch = [pltpu.VMEM((CONV_HALO + tm, CONV_CH), F32)]
    res = pl.pallas_call(
        functools.partial(_inproj_kernel, tiles_per_seq=tiles_per_seq),
        grid=(m // tm,),
        in_specs=in_specs,
        out_specs=out_specs,
        out_shape=out_shape,
        scratch_shapes=scratch,
        compiler_params=_cparams(("arbitrary",) if conv is not None else ("parallel",)),
        name="inproj",
    )(*args)
    return (res[0], res[1]) if conv is not None else (res[0], None)


def _pool_kernel(u_ref, init_ref, pw_ref, ps_ref, o_ref, ext_ref, *, pos0, tt):
    t = pl.program_id(1)
    bb = u_ref.shape[0]

    @pl.when(t == 0)
    def _():
        ext_ref[:, 0:POOL_HALO, :] = init_ref[...]

    @pl.when(t > 0)
    def _():
        ext_ref[:, 0:POOL_HALO, :] = ext_ref[:, tt:tt + POOL_HALO, :]

    ext_ref[:, POOL_HALO:POOL_HALO + tt, :] = u_ref[...].astype(F32)

    pos = pos0 + t * tt + lax.broadcasted_iota(jnp.int32, (1, tt, 1), 1)
    for g, w in enumerate(POOL_WINDOWS):
        cs = slice(g * POOL_GROUP_DIM, (g + 1) * POOL_GROUP_DIM)
        x = ext_ref[:, POOL_HALO:POOL_HALO + tt, cs]
        s = x
        for k in range(1, w):
            s = s + ext_ref[:, POOL_HALO - k:POOL_HALO - k + tt, cs]
        cnt = jnp.minimum(w, pos + 1).astype(F32)
        d = (s / cnt - x).reshape(bb * tt, POOL_GROUP_DIM)
        y = jnp.dot(d.astype(BF16), pw_ref[g], preferred_element_type=F32)
        o_ref[:, :, cs] = (y * ps_ref[:, cs]).reshape(bb, tt, POOL_GROUP_DIM).astype(o_ref.dtype)


def _pool(h3, init, pw, ps, pos0, bb, tt):
    b, t, _ = h3.shape
    return pl.pallas_call(
        functools.partial(_pool_kernel, pos0=pos0, tt=tt),
        grid=(b // bb, t // tt),
        in_specs=[
            pl.BlockSpec((bb, tt, POOL_WIDTH), lambda i, j: (i, j, U_OFF // POOL_WIDTH)),
            pl.BlockSpec((bb, POOL_HALO, POOL_WIDTH), lambda i, j: (i, 0, 0)),
            pl.BlockSpec((POOL_GROUPS, POOL_GROUP_DIM, POOL_GROUP_DIM), lambda i, j: (0, 0, 0)),
            pl.BlockSpec((1, POOL_WIDTH), lambda i, j: (0, 0)),
        ],
        out_specs=pl.BlockSpec((bb, tt, POOL_WIDTH), lambda i, j: (i, j, 0)),
        out_shape=jax.ShapeDtypeStruct((b, t, POOL_WIDTH), h3.dtype),
        scratch_shapes=[pltpu.VMEM((bb, POOL_HALO + tt, POOL_WIDTH), F32)],
        compiler_params=_cparams(("parallel", "arbitrary")),
        name="pool",
    )(h3, init, pw, ps)


def _softplus(x):
    return jnp.maximum(x, 0.0) + jnp.log1p(jnp.exp(-jnp.abs(x)))


def _dot_nt(a, b, **kw):
    return lax.dot_general(a, b, (((1,), (1,)), ((), ())), preferred_element_type=F32, **kw)


def _dot_tn(a, b):
    return lax.dot_general(a, b, (((0,), (0,)), ((), ())), preferred_element_type=F32)


def _dot(a, b):
    return jnp.dot(a, b, preferred_element_type=F32)


def _delta_kernel(qkv_ref, z_ref, ba_ref, cinit_ref, s0_ref, cw_ref, alog_ref, dtb_ref, alogt_ref, dtbt_ref,
                  onw_ref, *rest, bb, tb, chunk, conv_done):
    o_ref, s_ref, ext_ref = rest[-3:]
    t = pl.program_id(1)
    rows = bb * tb
    cpb = tb // chunk
    heads = list(range(DN_HEADS))
    mm = BF16 if rows >= 16 else F32
    mc = BF16 if chunk >= 16 else F32

    @pl.when(t == 0)
    def _():
        s_ref[...] = s0_ref[...]

    if conv_done:
        y = qkv_ref[...].astype(F32).reshape(rows, CONV_CH)
        qn = [y[:, h * DN_DK:(h + 1) * DN_DK] for h in heads]
        kn = [y[:, DN_QK + h * DN_DK:DN_QK + (h + 1) * DN_DK] for h in heads]
        vh = [y[:, 2 * DN_QK + h * DN_DV:2 * DN_QK + (h + 1) * DN_DV] for h in heads]
    else:
        @pl.when(t == 0)
        def _():
            ext_ref[:, 0:CONV_HALO, :] = cinit_ref[...]

        @pl.when(t > 0)
        def _():
            ext_ref[:, 0:CONV_HALO, :] = ext_ref[:, tb:tb + CONV_HALO, :]

        ext_ref[:, CONV_HALO:CONV_HALO + tb, :] = qkv_ref[...].astype(F32)
        qn, kn, vh = _qkv_heads(_short_conv(ext_ref, cw_ref, tb).reshape(rows, CONV_CH))

    bav = ba_ref[...].astype(F32).reshape(rows, 128)
    zv = z_ref[...].astype(F32).reshape(rows, DN_VW)

    row = lax.broadcasted_iota(jnp.int32, (rows, rows), 0)
    col = lax.broadcasted_iota(jnp.int32, (rows, rows), 1)
    shift = int(math.log2(chunk))
    same = (row >> shift) == (col >> shift)
    tri = same & (row >= col)
    strict = same & (row > col)
    lane = lax.broadcasted_iota(jnp.int32, (rows, 128), 1)

    if rows == 128:
        sub = bav.T[0:2 * DN_HEADS, :]
        g_t = -jnp.exp(alogt_ref[...]) * _softplus(sub + dtbt_ref[...])
        gcum_t = jnp.dot(g_t, (same & (col >= row)).astype(F32), preferred_element_type=F32,
                         precision=lax.Precision.HIGHEST)
        live = lax.broadcasted_iota(jnp.int32, sub.shape, 0) < DN_HEADS
        packed = jnp.where(live, jax.nn.sigmoid(sub), gcum_t)
        cols = jnp.concatenate([packed, jnp.zeros((rows - 2 * DN_HEADS, rows), F32)], axis=0).T
        beta_full = gcum = cols
        grow = [gcum_t[DN_HEADS + h:DN_HEADS + h + 1, :] for h in heads]
    else:
        beta_full = jax.nn.sigmoid(bav)
        g_full = -jnp.exp(alog_ref[...]) * _softplus(bav + dtb_ref[...])
        gcum = jnp.dot(tri.astype(F32), g_full, preferred_element_type=F32, precision=lax.Precision.HIGHEST)
        ones = jnp.ones((rows, 128), F32)
        grow = [_dot_nt(ones, jnp.where(lane == DN_HEADS + h, gcum, 0.0), precision=lax.Precision.HIGHEST)
                for h in heads]
    gcol = [jnp.sum(jnp.where(lane == DN_HEADS + h, gcum, 0.0), axis=1, keepdims=True) for h in heads]
    bcol = [jnp.sum(jnp.where(lane == h, beta_full, 0.0), axis=1, keepdims=True) for h in heads]
    eg = [jnp.exp(gcol[h]) for h in heads]

    bshift = min(shift, 4)
    uw, qkd, qg = {}, {}, {}
    for hg in (heads[i:i + HEAD_GROUP] for i in range(0, DN_HEADS, HEAD_GROUP)):
        decay = {h: jnp.where(tri, jnp.exp(jnp.where(tri, gcol[h] - grow[h], 0.0)), 0.0) for h in hg}
        kb = {h: kn[h].astype(mm) for h in hg}
        kk = {h: _dot_nt(kb[h], kb[h]) for h in hg}
        qk = {h: _dot_nt(qn[h].astype(mm), kb[h]) for h in hg}
        m = {h: jnp.where(strict, kk[h] * decay[h] * bcol[h], 0.0) for h in hg}
        p = {h: jnp.where((row >> bshift) == (col >> bshift), -m[h], 0.0) for h in hg}
        xp = p
        for _ in range(bshift - 1):
            xb = {h: xp[h].astype(mm) for h in hg}
            xp = {h: _dot(xb[h], xb[h]) for h in hg}
            p = {h: p[h] + xp[h] + _dot(p[h].astype(mm), xp[h].astype(mm)) for h in hg}
        for s in range(bshift, shift):
            lower = ((row >> (s + 1)) == (col >> (s + 1))) & ((row >> s) > (col >> s))
            c_blk = {h: jnp.where(lower, m[h], 0.0) for h in hg}
            pb = {h: p[h].astype(mm) for h in hg}
            a = {h: c_blk[h] + _dot(pb[h], c_blk[h].astype(mm)) for h in hg}
            p = {h: p[h] - a[h] - _dot(a[h].astype(mm), pb[h]) for h in hg}
        for h in hg:
            rhs = jnp.concatenate([vh[h] * bcol[h], kn[h] * (bcol[h] * eg[h])], axis=1)
            uw[h] = rhs + _dot(p[h].astype(mm), rhs.astype(mm))
            qkd[h] = (qk[h] * decay[h]).astype(mm)
            qg[h] = qn[h] * eg[h]

    pairs = [(b, h) for b in range(bb) for h in heads]
    state = {bh: s_ref[bh[0], bh[1]] for bh in pairs}
    dlt = {h: [] for h in heads}
    oq = {h: [] for h in heads}
    for c in range(cpb):
        def rs(b):
            return slice(b * tb + c * chunk, b * tb + (c + 1) * chunk)
        prod = {}
        for b, h in pairs:
            lhs = jnp.concatenate([uw[h][rs(b), DN_DV:], qg[h][rs(b)]], axis=0).astype(mc)
            prod[b, h] = _dot(lhs, state[b, h].astype(mc))
        for b, h in pairs:
            r = rs(b)
            d = uw[h][r, :DN_DV] - prod[b, h][:chunk]
            glast = gcol[h][r.stop - 1:r.stop, :]
            kg = kn[h][r] * jnp.exp(glast - gcol[h][r])
            state[b, h] = state[b, h] * jnp.exp(glast) + _dot_tn(kg.astype(mc), d.astype(mc))
            dlt[h].append((r.start, d))
            oq[h].append((r.start, prod[b, h][chunk:]))
    for b, h in pairs:
        s_ref[b, h] = state[b, h]

    def stack(parts):
        parts = [v for _, v in sorted(parts, key=lambda sv: sv[0])]
        return parts[0] if len(parts) == 1 else jnp.concatenate(parts, axis=0)

    for h in heads:
        hs = slice(h * DN_DV, (h + 1) * DN_DV)
        o = stack(oq[h]) + _dot(qkd[h], stack(dlt[h]).astype(mm))
        zh = zv[:, hs]
        gated = _rms(o, onw_ref[...]) * (zh * jax.nn.sigmoid(zh))
        o_ref[:, :, hs] = gated.reshape(bb, tb, DN_DV).astype(o_ref.dtype)


def _delta(h3, cinit, s0, s0_layer, cw, gate, onw, bb, tb, chunk, stack=None, conv_done=False):
    b, t, _ = h3.shape
    assert b % bb == 0 and t % tb == 0 and tb % chunk == 0
    state_block = (None, bb, DN_HEADS, DN_DK, DN_DV)
    in_specs = [
        pl.BlockSpec((bb, tb, CONV_CH), lambda i, j: (i, j, QKV_OFF // CONV_CH)),
        pl.BlockSpec((bb, tb, DN_VW), lambda i, j: (i, j, Z_OFF // DN_VW)),
        pl.BlockSpec((bb, tb, 128), lambda i, j: (i, j, BA_OFF // 128)),
        pl.BlockSpec((bb, CONV_HALO, CONV_CH), lambda i, j: (i, 0, 0)),
        pl.BlockSpec(state_block, lambda i, j: (s0_layer, i, 0, 0, 0)),
        pl.BlockSpec((CONV_W, CONV_CH), lambda i, j: (0, 0)),
        pl.BlockSpec((1, 128), lambda i, j: (0, 0)),
        pl.BlockSpec((1, 128), lambda i, j: (0, 0)),
        pl.BlockSpec((2 * DN_HEADS, 128), lambda i, j: (0, 0)),
        pl.BlockSpec((2 * DN_HEADS, 128), lambda i, j: (0, 0)),
        pl.BlockSpec((1, DN_DV), lambda i, j: (0, 0)),
    ]
    args = [h3, h3, h3, cinit, s0, cw, *gate, onw]
    layer, depth, prev = (0, 1, None) if stack is None else stack
    aliases = {}
    if prev is not None:
        in_specs.append(pl.BlockSpec(memory_space=pl.ANY))
        aliases = {len(args): 1}
        args.append(prev)
    o, s = pl.pallas_call(
        functools.partial(_delta_kernel, bb=bb, tb=tb, chunk=chunk, conv_done=conv_done),
        grid=(b // bb, t // tb),
        in_specs=in_specs,
        out_specs=[
            pl.BlockSpec((bb, tb, DN_VW), lambda i, j: (i, j, 0)),
            pl.BlockSpec(state_block, lambda i, j: (layer, i, 0, 0, 0)),
        ],
        out_shape=[
            jax.ShapeDtypeStruct((b, t, DN_VW), h3.dtype),
            jax.ShapeDtypeStruct((depth, b, DN_HEADS, DN_DK, DN_DV), F32),
        ],
        scratch_shapes=[pltpu.VMEM((bb, CONV_HALO + tb, CONV_CH), F32)],
        input_output_aliases=aliases,
        compiler_params=_cparams(("parallel", "arbitrary")),
        name="delta",
    )(*args)
    return o, (s[0] if stack is None else s)


def _rope(x, cos, sin):
    width = x.shape[-1]
    reps = width // cos.shape[-1]
    if reps > 1:
        cos = jnp.concatenate([cos] * reps, axis=1)
        sin = jnp.concatenate([sin] * reps, axis=1)
    lane = lax.broadcasted_iota(jnp.int32, x.shape, 1)
    first_half = (lane & (SWA_HD - 1)) < (SWA_HD // 2)
    other = jnp.where(first_half, pltpu.roll(x, width - SWA_HD // 2, 1), pltpu.roll(x, SWA_HD // 2, 1))
    return x * cos + other * sin


def _pad_heads(x, lo):
    xr = pltpu.roll(x, SWA_HD, 1)
    z = jnp.zeros_like(x)
    return [jnp.where(lo, x, z), jnp.where(lo, z, xr), jnp.where(lo, xr, z), jnp.where(lo, z, x)]


def _swa_prompt_kernel(sinks_ref, q_ref, kv_ref, meta_ref, cos_ref, sin_ref, o_ref, klast_ref, kcat_ref, vcat_ref,
                       *, pos0):
    n = pl.program_id(1)
    blk = SWA_WINDOW
    lo = lax.broadcasted_iota(jnp.int32, (blk, 128), 1) < SWA_HD

    def put(half, k, v):
        rows = slice(half * blk, (half + 1) * blk)
        for i, (a, b) in enumerate(zip(_pad_heads(k, lo), _pad_heads(v, lo))):
            kcat_ref[i, rows, :] = a.astype(BF16)
            vcat_ref[i, rows, :] = b.astype(BF16)

    @pl.when(n == 0)
    def _():
        put(0, meta_ref[:, :SWA_KV_WIDTH], meta_ref[:, SWA_KV_WIDTH:])

    @pl.when(n > 0)
    def _():
        kcat_ref[:, 0:blk, :] = kcat_ref[:, blk:2 * blk, :]
        vcat_ref[:, 0:blk, :] = vcat_ref[:, blk:2 * blk, :]

    cos = cos_ref[...]
    sin = sin_ref[...]
    q = _rope(q_ref[0].astype(F32), cos, sin) * (SWA_HD ** -0.5)
    kv = kv_ref[0].astype(F32)
    k = _rope(kv[:, :SWA_KV_WIDTH], cos, sin)
    klast_ref[0] = k
    put(1, k, kv[:, SWA_KV_WIDTH:])

    r = lax.broadcasted_iota(jnp.int32, (blk, 2 * blk), 0)
    j = lax.broadcasted_iota(jnp.int32, (blk, 2 * blk), 1)
    jmin = jnp.maximum(blk - pos0 - n * blk, 0)
    mask = ((j < blk) & (j > r) & (j >= jmin)) | ((j >= blk) & ((j - blk) <= r))
    heads = range(SWA_HEADS)
    qt = [q[:, i * 128:(i + 1) * 128].astype(BF16) for i in range(SWA_HEADS // 2)]
    s = [jnp.where(mask, _dot_nt(qt[h // 2], kcat_ref[2 * (h // SWA_GROUP) + h % 2]), NEG_BIG) for h in heads]
    m = [jnp.maximum(jnp.max(s[h], axis=1, keepdims=True), sinks_ref[h]) for h in heads]
    e = [jnp.exp(s[h] - m[h]).astype(BF16) for h in heads]
    ones = jnp.ones((2 * blk, 128), BF16)
    den = [_dot(e[h], ones) + jnp.exp(sinks_ref[h] - m[h]) for h in heads]
    for i in range(SWA_HEADS // 2):
        he, ho = 2 * i, 2 * i + 1
        g = he // SWA_GROUP
        acc = _dot(e[he], vcat_ref[2 * g]) + _dot(e[ho], vcat_ref[2 * g + 1])
        o_ref[0, :, i * 128:(i + 1) * 128] = (acc / jnp.where(lo, den[he], den[ho])).astype(o_ref.dtype)


def _swa_prompt(h3, meta_kv, cos, sin, sinks, pos0):
    b, t, _ = h3.shape
    blk = SWA_WINDOW
    assert t % blk == 0
    return pl.pallas_call(
        functools.partial(_swa_prompt_kernel, pos0=pos0),
        grid=(b, t // blk),
        in_specs=[
            pl.BlockSpec(memory_space=pltpu.SMEM),
            pl.BlockSpec((1, blk, SWA_WIDTH), lambda i, n: (i, n, QC_OFF // SWA_WIDTH)),
            pl.BlockSpec((1, blk, 256), lambda i, n: (i, n, KVC_OFF // 256)),
            pl.BlockSpec((blk, 256), lambda i, n: (0, 0)),
            pl.BlockSpec((blk, 128), lambda i, n: (n, 0)),
            pl.BlockSpec((blk, 128), lambda i, n: (n, 0)),
        ],
        out_specs=[
            pl.BlockSpec((1, blk, SWA_WIDTH), lambda i, n: (i, n, 0)),
            pl.BlockSpec((1, blk, SWA_KV_WIDTH), lambda i, n: (i, 0, 0)),
        ],
        out_shape=[
            jax.ShapeDtypeStruct((b, t, SWA_WIDTH), h3.dtype),
            jax.ShapeDtypeStruct((b, blk, SWA_KV_WIDTH), F32),
        ],
        scratch_shapes=[pltpu.VMEM((4, 2 * blk, 128), BF16), pltpu.VMEM((4, 2 * blk, 128), BF16)],
        compiler_params=_cparams(("parallel", "arbitrary")),
        name="swa_prompt",
    )(sinks, h3, h3, meta_kv, cos, sin)


def _swa_step_kernel(sink_ref, q_ref, kv_ref, *rest, bb, t_new, window):
    if window:
        kc_ref, vc_ref, cos_ref, sin_ref, o_ref, knew_ref = rest
    else:
        cos_ref, sin_ref, o_ref, knew_ref = rest
    cos = cos_ref[...]
    sin = sin_ref[...]
    rows = SWA_HEADS * t_new
    mn = BF16 if t_new >= 16 else F32
    lo = lax.broadcasted_iota(jnp.int32, (t_new, 128), 1) < SWA_HD
    tq = lax.broadcasted_iota(jnp.int32, (rows, t_new), 0) & (t_new - 1)
    mask_n = lax.broadcasted_iota(jnp.int32, (rows, t_new), 1) <= tq
    if window:
        diff = ((lax.broadcasted_iota(jnp.int32, (rows, window), 0) & (t_new - 1)) + window
                - lax.broadcasted_iota(jnp.int32, (rows, window), 1))
        mask_c = (diff >= 0) & (diff < SWA_WINDOW)
    sink = sink_ref[...]
    seqs = range(bb)

    qall, kk, vv = [], [], []
    for s in seqs:
        q = _rope(q_ref[s].astype(F32), cos, sin) * (SWA_HD ** -0.5)
        kv = kv_ref[s].astype(F32)
        k = _rope(kv[:, :SWA_KV_WIDTH], cos, sin)
        knew_ref[s] = k
        pieces = []
        for h in range(SWA_HEADS):
            g = h // SWA_GROUP
            tile = q[:, (h // 2) * 128:(h // 2 + 1) * 128]
            if h % 2 != g:
                tile = pltpu.roll(tile, SWA_HD, 1)
            pieces.append(jnp.where(lo, tile, 0.0) if g == 0 else jnp.where(lo, 0.0, tile))
        qall.append(jnp.concatenate(pieces, axis=0))
        kk.append(k)
        vv.append(kv[:, SWA_KV_WIDTH:])

    s_n = [jnp.where(mask_n, _dot_nt(qall[s].astype(mn), kk[s].astype(mn)), NEG_BIG) for s in seqs]
    m = [jnp.maximum(jnp.max(s_n[s], axis=1, keepdims=True), sink) for s in seqs]
    if window:
        s_c = [jnp.where(mask_c, _dot_nt(qall[s].astype(BF16), kc_ref[s].astype(BF16)), NEG_BIG) for s in seqs]
        m = [jnp.maximum(m[s], jnp.max(s_c[s], axis=1, keepdims=True)) for s in seqs]
    e_n = [jnp.exp(s_n[s] - m[s]) for s in seqs]
    den = [jnp.sum(e_n[s], axis=1, keepdims=True) + jnp.exp(sink - m[s]) for s in seqs]
    acc = [_dot(e_n[s].astype(mn), vv[s].astype(mn)) for s in seqs]
    if window:
        e_c = [jnp.exp(s_c[s] - m[s]) for s in seqs]
        den = [den[s] + jnp.sum(e_c[s], axis=1, keepdims=True) for s in seqs]
        acc = [acc[s] + _dot(e_c[s].astype(BF16), vc_ref[s].astype(BF16)) for s in seqs]
    for s in seqs:
        a = acc[s] / den[s]
        for i in range(SWA_HEADS // 2):
            he, ho = 2 * i, 2 * i + 1
            g = he // SWA_GROUP
            a_e = a[he * t_new:(he + 1) * t_new]
            a_o = a[ho * t_new:(ho + 1) * t_new]
            if g == 1:
                a_e = pltpu.roll(a_e, SWA_HD, 1)
            else:
                a_o = pltpu.roll(a_o, SWA_HD, 1)
            o_ref[s, :, i * 128:(i + 1) * 128] = jnp.where(lo, a_e, a_o).astype(o_ref.dtype)


def _swa_step(h3, kc, vc, cos, sin, sinks, bb):
    b, t, _ = h3.shape
    assert b % bb == 0
    window = 0 if kc is None else kc.shape[1]
    sink_rows = jnp.repeat(sinks, t)[:, None]
    in_specs = [
        pl.BlockSpec((SWA_HEADS * t, 1), lambda i: (0, 0)),
        pl.BlockSpec((bb, t, SWA_WIDTH), lambda i: (i, 0, QC_OFF // SWA_WIDTH)),
        pl.BlockSpec((bb, t, 256), lambda i: (i, 0, KVC_OFF // 256)),
    ]
    args = [sink_rows, h3, h3]
    if window:
        in_specs += [pl.BlockSpec((bb, window, SWA_KV_WIDTH), lambda i: (i, 0, 0))] * 2
        args += [kc, vc]
    in_specs += [pl.BlockSpec((t, 128), lambda i: (0, 0))] * 2
    args += [cos, sin]
    return pl.pallas_call(
        functools.partial(_swa_step_kernel, bb=bb, t_new=t, window=window),
        grid=(b // bb,),
        in_specs=in_specs,
        out_specs=[
            pl.BlockSpec((bb, t, SWA_WIDTH), lambda i: (i, 0, 0)),
            pl.BlockSpec((bb, t, SWA_KV_WIDTH), lambda i: (i, 0, 0)),
        ],
        out_shape=[
            jax.ShapeDtypeStruct((b, t, SWA_WIDTH), h3.dtype),
            jax.ShapeDtypeStruct((b, t, SWA_KV_WIDTH), F32),
        ],
        compiler_params=_cparams(("parallel",)),
        name="swa_step",
    )(*args)


def _merge_kernel(oa_ref, ob_ref, oc_ref, ga_ref, gb_ref, gc_ref, x_ref, pa_ref, pb_ref, pc_ref, wo_ref, o_ref):
    def branch(o_r, g_r, p_r):
        return jax.nn.sigmoid(g_r[...].astype(F32)) * jnp.dot(o_r[...].astype(BF16), p_r[...], preferred_element_type=F32)

    m = branch(oa_ref, ga_ref, pa_ref) + branch(ob_ref, gb_ref, pb_ref) + branch(oc_ref, gc_ref, pc_ref)
    o_ref[...] = x_ref[...] + jnp.dot(m.astype(BF16), wo_ref[...], preferred_element_type=F32)


def _merge(oa, ob, oc, h, x, pa, pb, pc, wo):
    m = x.shape[0]
    tm = min(m, 512)
    assert m % tm == 0
    row =lambda w: pl.BlockSpec((tm, w), lambda i: (i, 0))
    gate = lambda c: pl.BlockSpec((tm, D_MODEL), lambda i: (i, c))
    full = lambda a: pl.BlockSpec(a.shape, lambda i: (0, 0))
    return pl.pallas_call(
        _merge_kernel,
        grid=(m // tm,),
        in_specs=[row(512), row(512), row(512), gate(0), gate(1), gate(2), row(D_MODEL),
                  full(pa), full(pb), full(pc), full(wo)],
        out_specs=row(D_MODEL),
        out_shape=jax.ShapeDtypeStruct((m, D_MODEL), F32),
        compiler_params=_cparams(("parallel",)),
        name="merge",
    )(oa, ob, oc, h, h, h, x, pa, pb, pc, wo)


def _mlp_kernel(x_ref, nw_ref, wu_ref, wd_ref, fnw_ref, o_ref, xn_ref, acc_ref, *, final):
    f = pl.program_id(1)

    @pl.when(f == 0)
    def _():
        x = x_ref[...]
        xn_ref[...] = _rms(x, nw_ref[...]).astype(BF16)
        acc_ref[...] = x

    a = jnp.maximum(jnp.dot(xn_ref[...], wu_ref[...], preferred_element_type=F32), 0.0)
    acc_ref[...] += jnp.dot((a * a).astype(BF16), wd_ref[...], preferred_element_type=F32)

    @pl.when(f == pl.num_programs(1) - 1)
    def _():
        y = acc_ref[...]
        o_ref[...] = _rms(y, fnw_ref[...]) if final else y


def _mlp(x, nw, wu, wd, fnw, final):
    m = x.shape[0]
    tm = min(m, 1024)
    assert m % tm == 0
    tf = 1024
    return pl.pallas_call(
        functools.partial(_mlp_kernel, final=final),
        grid=(m // tm, D_FF // tf),
        in_specs=[
            pl.BlockSpec((tm, D_MODEL), lambda i, f: (i, 0)),
            pl.BlockSpec((1, D_MODEL), lambda i, f: (0, 0)),
            pl.BlockSpec((D_MODEL, tf), lambda i, f: (0, f)),
            pl.BlockSpec((tf, D_MODEL), lambda i, f: (f, 0)),
            pl.BlockSpec((1, D_MODEL), lambda i, f: (0, 0)),
        ],
        out_specs=pl.BlockSpec((tm, D_MODEL), lambda i, f: (i, 0)),
        out_shape=jax.ShapeDtypeStruct((m, D_MODEL), F32),
        scratch_shapes=[pltpu.VMEM((tm, D_MODEL), BF16), pltpu.VMEM((tm, D_MODEL), F32)],
        compiler_params=_cparams(("parallel", "arbitrary")),
        name="mlp",
    )(x, nw, wu, wd, fnw)


def _rope_tables(pos0, t):
    half = SWA_HD // 2
    inv = ROPE_THETA ** (-jnp.arange(half, dtype=F32) / half)
    ang = (pos0 + jnp.arange(t)).astype(F32)[:, None] * inv[None, :]
    cos = jnp.cos(ang)
    sin = jnp.sin(ang)
    return jnp.tile(cos, (1, 4)), jnp.tile(jnp.concatenate([-sin, sin], axis=1), (1, 2))


def _prep_layer(l, norm1_w, w_in, pool_w, pool_scale, dn_conv_w, dn_a_log, dn_dt_bias, dn_onorm_w, swa_sinks,
                proj_a, proj_b, proj_c, w_out, norm2_w, w_up, w_down):
    w = w_in[l]
    segments = ((3336, 6408), (512, 2048), (0, 512), (2048, 2560), (2568, 3080), (3080, 3336), (2560, 2568))
    w_perm = jnp.concatenate(
        [w[:, a:b].astype(BF16) for a, b in segments]
        + [jnp.zeros((D_MODEL, H_WIDTH - BA_OFF - 2 * DN_HEADS), BF16)], axis=1)
    lane_pad = lambda v: jnp.zeros((1, 128), F32).at[0, DN_HEADS:2 * DN_HEADS].set(v.astype(F32))
    row_pad = lambda v: jnp.zeros((2 * DN_HEADS, 128), F32).at[DN_HEADS:].set(
        jnp.broadcast_to(v.astype(F32)[:, None], (DN_HEADS, 128)))
    return dict(
        norm1=norm1_w[l][None].astype(F32), w_in=w_perm,
        pool_w=pool_w[l].astype(BF16), pool_scale=pool_scale[l][None].astype(F32),
        conv_w=dn_conv_w[l].astype(F32),
        gate=(lane_pad(dn_a_log[l]), lane_pad(dn_dt_bias[l]), row_pad(dn_a_log[l]), row_pad(dn_dt_bias[l])),
        onw=dn_onorm_w[l][None].astype(F32), sinks=swa_sinks[l].astype(F32),
        pa=proj_a[l].astype(BF16), pb=proj_b[l].astype(BF16), pc=proj_c[l].astype(BF16),
        wo=w_out[l].astype(BF16), norm2=norm2_w[l][None].astype(F32),
        wu=w_up[l].astype(BF16), wd=w_down[l].astype(BF16))


def _group_step(x3, p, fnw, final, pos0, pool_init, conv_init, s0, kind, swa_extra, pool_tiles, delta_tiles,
                swa_bb=1, s_stack=None, act=F32, fuse_conv=False):
    b, t, _ = x3.shape
    x2 = x3.reshape(b * t, D_MODEL)
    h, conv_tails = _inproj(x2, p['norm1'], p['w_in'], act,
                            conv=(conv_init, p['conv_w'], t) if fuse_conv else None)
    h3 = h.reshape(b, t, H_WIDTH)
    o_a = _pool(h3, pool_init, p['pool_w'], p['pool_scale'], pos0, *pool_tiles)
    o_b, s_new = _delta(h3, conv_init, s0[0], s0[1], p['conv_w'], p['gate'], p['onw'], *delta_tiles,
                        stack=s_stack, conv_done=fuse_conv)
    cos, sin = _rope_tables(pos0, t)
    if kind == 'prompt':
        o_c, k_rot = _swa_prompt(h3, swa_extra, cos, sin, p['sinks'], pos0)
    else:
        kc, vc = swa_extra
        o_c, k_rot = _swa_step(h3, kc, vc, cos, sin, p['sinks'], swa_bb)
    m = b * t
    h1 = _merge(o_a.reshape(m, 512), o_b.reshape(m, 512), o_c.reshape(m, 512), h, x2,
                p['pa'], p['pb'], p['pc'], p['wo'])
    out = _mlp(h1, p['norm2'], p['wu'], p['wd'], fnw, final)
    return out.reshape(b, t, D_MODEL), h3, s_new, k_rot, conv_tails


def kernel(x_prompt, x_sample, state_pool, state_conv, state_delta, cache_swa_k, cache_swa_v, meta_tokens, norm1_w, w_in, pool_w, pool_scale, dn_conv_w, dn_a_log, dn_dt_bias, dn_onorm_w, swa_sinks, proj_a, proj_b, proj_c, w_out, norm2_w, w_up, w_down, final_norm_w):
    depth = w_in.shape[0]
    bp, tp, _ = x_prompt.shape
    bs, ts, _ = x_sample.shape
    past = cache_swa_k.shape[2]
    fnw = final_norm_w[None].astype(F32)

    xm = meta_tokens[None].astype(F32)
    xp = x_prompt
    xs = x_sample
    outs = {k: [] for k in ('pool_p', 'conv_p', 'k_p', 'v_p', 'pool_s', 'conv_s', 'k_s', 'v_s')}
    delta_p = delta_s = None
    for l in range(depth):
        p = _prep_layer(l, norm1_w, w_in, pool_w, pool_scale, dn_conv_w, dn_a_log, dn_dt_bias, dn_onorm_w,
                        swa_sinks, proj_a, proj_b, proj_c, w_out, norm2_w, w_up, w_down)
        final = l == depth - 1

        xm, hm, s_m, k_m, _ = _group_step(
            xm, p, fnw, final, 0,
            jnp.zeros((1, POOL_HALO, POOL_WIDTH), F32), jnp.zeros((1, CONV_HALO, CONV_CH), F32),
            (jnp.zeros((1, 1, DN_HEADS, DN_DK, DN_DV), F32), 0), 'step', (None, None),
            (1, N_META), (1, N_META, N_META))

        meta_kv = jnp.zeros((SWA_WINDOW, 256), F32)
        meta_kv = meta_kv.at[SWA_WINDOW - N_META:, :SWA_KV_WIDTH].set(k_m[0])
        meta_kv = meta_kv.at[SWA_WINDOW - N_META:, SWA_KV_WIDTH:].set(hm[0, :, KVC_OFF + SWA_KV_WIDTH:KVC_OFF + 256])
        xp, hp, s_p, k_p, tails_p = _group_step(
            xp, p, fnw, final, N_META,
            jnp.broadcast_to(hm[:, :, U_OFF:U_OFF + POOL_WIDTH], (bp, POOL_HALO, POOL_WIDTH)),
            jnp.broadcast_to(hm[:, N_META - CONV_HALO:, QKV_OFF:QKV_OFF + CONV_CH], (bp, CONV_HALO, CONV_CH)),
            (jnp.broadcast_to(s_m[None], (1, bp, DN_HEADS, DN_DK, DN_DV)), 0), 'prompt', meta_kv,
            (1, 512), (2, DN_CHUNK, DN_CHUNK), s_stack=(l, depth, delta_p), act=BF16, fuse_conv=True)
        delta_p = s_p
        outs['pool_p'].append(hp[:, tp - POOL_BUF:, U_OFF:U_OFF + POOL_WIDTH].astype(F32))
        if tails_p is None:
            outs['conv_p'].append(hp[:, tp - (CONV_W - 1):, QKV_OFF:QKV_OFF + CONV_CH].astype(F32))
        else:
            outs['conv_p'].append(tails_p.reshape(bp, -1, CONV_HALO, CONV_CH)[:, -1, CONV_HALO - (CONV_W - 1):])
        outs['k_p'].append(k_p.reshape(bp, SWA_WINDOW, SWA_KV_HEADS, SWA_HD))
        outs['v_p'].append(hp[:, tp - SWA_WINDOW:, KVC_OFF + SWA_KV_WIDTH:KVC_OFF + 256].astype(F32)
                           .reshape(bp, SWA_WINDOW, SWA_KV_HEADS, SWA_HD))

        kc = cache_swa_k[l].reshape(bs, past, SWA_KV_WIDTH)
        vc = cache_swa_v[l].reshape(bs, past, SWA_KV_WIDTH)
        xs, hs, s_s, k_s, _ = _group_step(
            xs, p, fnw, final, PAST_LEN,
            jnp.concatenate([jnp.zeros((bs, POOL_HALO - POOL_BUF, POOL_WIDTH), F32), state_pool[l]], axis=1),
            jnp.concatenate([jnp.zeros((bs, CONV_HALO - (CONV_W - 1), CONV_CH), F32), state_conv[l]], axis=1),
            (state_delta, l), 'step', (kc, vc),
            (bs, ts), (16, ts, ts), 16, s_stack=(l, depth, delta_s))
        delta_s = s_s
        outs['pool_s'].append(jnp.concatenate([state_pool[l], hs[:, :, U_OFF:U_OFF + POOL_WIDTH]], axis=1)[:, -POOL_BUF:])
        outs['conv_s'].append(jnp.concatenate([state_conv[l], hs[:, :, QKV_OFF:QKV_OFF + CONV_CH]], axis=1)[:, -(CONV_W - 1):])
        v_new = hs[:, :, KVC_OFF + SWA_KV_WIDTH:KVC_OFF + 256]
        outs['k_s'].append(jnp.concatenate([kc, k_s], axis=1)[:, -past:].reshape(bs, past, SWA_KV_HEADS, SWA_HD))
        outs['v_s'].append(jnp.concatenate([vc, v_new], axis=1)[:, -past:].reshape(bs, past, SWA_KV_HEADS, SWA_HD))

    st = {k: jnp.stack(v) for k, v in outs.items()}
    return (xp, xs, st['pool_p'], st['conv_p'], delta_p, st['k_p'], st['v_p'],
            st['pool_s'], st['conv_s'], delta_s, st['k_s'], st['v_s'])
```

```python
import functools
import math

import jax
import jax.numpy as jnp
from jax import lax
from jax.experimental import pallas as pl
from jax.experimental.pallas import tpu as pltpu

F32 = jnp.float32
BF16 = jnp.bfloat16

D_MODEL = 1024
N_META = 16
EPS = 1e-6
POOL_GROUPS = 4
POOL_GROUP_DIM = 128
POOL_WIDTH = 512
POOL_WINDOWS = (2, 4, 8, 16)
POOL_BUF = 15
POOL_HALO = 16
DN_HEADS = 4
DN_DK = 128
DN_DV = 128
DN_QK = 512
DN_VW = 512
CONV_W = 4
CONV_CH = 1536
CONV_HALO = 8
DN_CHUNK = 64
HEAD_GROUP = 4
SWA_HEADS = 8
SWA_KV_HEADS = 2
SWA_GROUP = 4
SWA_HD = 64
SWA_WIDTH = 512
SWA_KV_WIDTH = 128
SWA_WINDOW = 128
ROPE_THETA = 10000.0
D_FF = 4096
PAST_LEN = 16384

G_OFF = 0
QKV_OFF = 3072
U_OFF = 4608
Z_OFF = 5120
QC_OFF = 5632
KVC_OFF = 6144
BA_OFF = 6400
H_WIDTH = 6528
IN_CHUNKS = ((0, 1536), (1536, 3072), (3072, 4608), (4608, 6144), (6144, 6528))

VMEM_LIMIT = 56 * 1024 * 1024
NEG_BIG = -1e30


def _cparams(sem):
    return pltpu.CompilerParams(dimension_semantics=sem, vmem_limit_bytes=VMEM_LIMIT)


def _rms(x, w):
    return x * lax.rsqrt(jnp.mean(x * x, axis=-1, keepdims=True) + EPS) * w


def _l2n(v):
    return v * lax.rsqrt(jnp.sum(v * v, axis=-1, keepdims=True) + EPS)


def _short_conv(ext_ref, cw_ref, rows):
    base = CONV_HALO - (CONV_W - 1)
    y = ext_ref[..., base:base + rows, :] * cw_ref[0:1, :]
    for j in range(1, CONV_W):
        y = y + ext_ref[..., base + j:base + j + rows, :] * cw_ref[j:j + 1, :]
    return y * jax.nn.sigmoid(y)


def _qkv_heads(y):
    heads = range(DN_HEADS)
    qn = [_l2n(y[:, h * DN_DK:(h + 1) * DN_DK]) * (DN_DK ** -0.5) for h in heads]
    kn = [_l2n(y[:, DN_QK + h * DN_DK:DN_QK + (h + 1) * DN_DK]) for h in heads]
    vh = [y[:, 2 * DN_QK + h * DN_DV:2 * DN_QK + (h + 1) * DN_DV] for h in heads]
    return qn, kn, vh


def _inproj_kernel(x_ref, nw_ref, w_ref, *rest, tiles_per_seq):
    if tiles_per_seq:
        cinit_ref, cw_ref, o_ref, tail_ref, ext_ref = rest
    else:
        (o_ref,) = rest
    tm = x_ref.shape[0]
    if tiles_per_seq:
        first = pl.program_id(0) % tiles_per_seq == 0

        @pl.when(first)
        def _():
            ext_ref[0:CONV_HALO, :] = cinit_ref[0]

        @pl.when(jnp.logical_not(first))
        def _():
            ext_ref[0:CONV_HALO, :] = ext_ref[tm:tm + CONV_HALO, :]

    xn = _rms(x_ref[...], nw_ref[...]).astype(BF16)
    for a, b in sorted(IN_CHUNKS, key=lambda ab: ab[0] != QKV_OFF):
        acc = jnp.dot(xn, w_ref[:, a:b], preferred_element_type=F32)
        if tiles_per_seq and a == QKV_OFF:
            ext_ref[CONV_HALO:CONV_HALO + tm, :] = acc
            tail_ref[0] = acc[tm - CONV_HALO:, :]
            qn, kn, vh = _qkv_heads(_short_conv(ext_ref, cw_ref, tm))
            acc = jnp.concatenate(qn + kn + vh, axis=1)
        o_ref[:, a:b] = acc.astype(o_ref.dtype)


def _inproj(x, nw, w, act, conv=None):
    m = x.shape[0]
    tm = min(m, 512 if act == BF16 else 256)
    assert m % tm == 0
    in_specs = [
        pl.BlockSpec((tm, D_MODEL), lambda i: (i, 0)),
        pl.BlockSpec((1, D_MODEL), lambda i: (0, 0)),
        pl.BlockSpec((D_MODEL, H_WIDTH), lambda i: (0, 0), pipeline_mode=pl.Buffered(1)),
    ]
    out_specs = [pl.BlockSpec((tm, H_WIDTH), lambda i: (i, 0))]
    out_shape = [jax.ShapeDtypeStruct((m, H_WIDTH), act)]
    args = [x, nw, w]
    scratch = []
    tiles_per_seq = 0
    if conv is not None:
        cinit, cw, seq_rows = conv
        assert seq_rows % tm == 0
        tiles_per_seq = seq_rows // tm
        in_specs += [pl.BlockSpec((1, CONV_HALO, CONV_CH), lambda i: (i // tiles_per_seq, 0, 0)),
                     pl.BlockSpec((CONV_W, CONV_CH), lambda i: (0, 0))]
        args += [cinit, cw]
        out_specs.append(pl.BlockSpec((1, CONV_HALO, CONV_CH), lambda i: (i, 0, 0)))
        out_shape.append(jax.ShapeDtypeStruct((m // tm, CONV_HALO, CONV_CH), F32))
        scratch = [pltpu.VMEM((CONV_HALO + tm, CONV_CH), F32)]
    res = pl.pallas_call(
        functools.partial(_inproj_kernel, tiles_per_seq=tiles_per_seq),
        grid=(m // tm,),
        in_specs=in_specs,
        out_specs=out_specs,
        out_shape=out_shape,
        scratch_shapes=scratch,
        compiler_params=_cparams(("arbitrary",) if conv is not None else ("parallel",)),
        name="inproj",
    )(*args)
    return (res[0], res[1]) if conv is not None else (res[0], None)


def _pool_kernel(u_ref, init_ref, pw_ref, ps_ref, o_ref, ext_ref, *, pos0, tt):
    t = pl.program_id(1)
    bb = u_ref.shape[0]

    @pl.when(t == 0)
    def _():
        ext_ref[:, 0:POOL_HALO, :] = init_ref[...]

    @pl.when(t > 0)
    def _():
        ext_ref[:, 0:POOL_HALO, :] = ext_ref[:, tt:tt + POOL_HALO, :]

    ext_ref[:, POOL_HALO:POOL_HALO + tt, :] = u_ref[...].astype(F32)

    pos = pos0 + t * tt + lax.broadcasted_iota(jnp.int32, (1, tt, 1), 1)
    for g, w in enumerate(POOL_WINDOWS):
        cs = slice(g * POOL_GROUP_DIM, (g + 1) * POOL_GROUP_DIM)
        x = ext_ref[:, POOL_HALO:POOL_HALO + tt, cs]
        s = x
        for k in range(1, w):
            s = s + ext_ref[:, POOL_HALO - k:POOL_HALO - k + tt, cs]
        cnt = jnp.minimum(w, pos + 1).astype(F32)
        d = (s / cnt - x).reshape(bb * tt, POOL_GROUP_DIM)
        y = jnp.dot(d.astype(BF16), pw_ref[g], preferred_element_type=F32)
        o_ref[:, :, cs] = (y * ps_ref[:, cs]).reshape(bb, tt, POOL_GROUP_DIM).astype(o_ref.dtype)


def _pool(h3, init, pw, ps, pos0, bb, tt):
    b, t, _ = h3.shape
    return pl.pallas_call(
        functools.partial(_pool_kernel, pos0=pos0, tt=tt),
        grid=(b // bb, t // tt),
        in_specs=[
            pl.BlockSpec((bb, tt, POOL_WIDTH), lambda i, j: (i, j, U_OFF // POOL_WIDTH)),
            pl.BlockSpec((bb, POOL_HALO, POOL_WIDTH), lambda i, j: (i, 0, 0)),
            pl.BlockSpec((POOL_GROUPS, POOL_GROUP_DIM, POOL_GROUP_DIM), lambda i, j: (0, 0, 0)),
            pl.BlockSpec((1, POOL_WIDTH), lambda i, j: (0, 0)),
        ],
        out_specs=pl.BlockSpec((bb, tt, POOL_WIDTH), lambda i, j: (i, j, 0)),
        out_shape=jax.ShapeDtypeStruct((b, t, POOL_WIDTH), h3.dtype),
        scratch_shapes=[pltpu.VMEM((bb, POOL_HALO + tt, POOL_WIDTH), F32)],
        compiler_params=_cparams(("parallel", "arbitrary")),
        name="pool",
    )(h3, init, pw, ps)


def _softplus(x):
    return jnp.maximum(x, 0.0) + jnp.log1p(jnp.exp(-jnp.abs(x)))


def _dot_nt(a, b, **kw):
    return lax.dot_general(a, b, (((1,), (1,)), ((), ())), preferred_element_type=F32, **kw)


def _dot_tn(a, b):
    return lax.dot_general(a, b, (((0,), (0,)), ((), ())), preferred_element_type=F32)


def _dot(a, b):
    return jnp.dot(a, b, preferred_element_type=F32)


def _delta_kernel(qkv_ref, z_ref, ba_ref, cinit_ref, s0_ref, cw_ref, alog_ref, dtb_ref, alogt_ref, dtbt_ref,
                  onw_ref, *rest, bb, tb, chunk, conv_done):
    o_ref, s_ref, ext_ref = rest[-3:]
    t = pl.program_id(1)
    rows = bb * tb
    cpb = tb // chunk
    heads = list(range(DN_HEADS))
    mm = BF16 if rows >= 16 else F32
    mc = BF16 if chunk >= 16 else F32

    @pl.when(t == 0)
    def _():
        s_ref[...] = s0_ref[...]

    if conv_done:
        y = qkv_ref[...].astype(F32).reshape(rows, CONV_CH)
        qn = [y[:, h * DN_DK:(h + 1) * DN_DK] for h in heads]
        kn = [y[:, DN_QK + h * DN_DK:DN_QK + (h + 1) * DN_DK] for h in heads]
        vh = [y[:, 2 * DN_QK + h * DN_DV:2 * DN_QK + (h + 1) * DN_DV] for h in heads]
    else:
        @pl.when(t == 0)
        def _():
            ext_ref[:, 0:CONV_HALO, :] = cinit_ref[...]

        @pl.when(t > 0)
        def _():
            ext_ref[:, 0:CONV_HALO, :] = ext_ref[:, tb:tb + CONV_HALO, :]

        ext_ref[:, CONV_HALO:CONV_HALO + tb, :] = qkv_ref[...].astype(F32)
        qn, kn, vh = _qkv_heads(_short_conv(ext_ref, cw_ref, tb).reshape(rows, CONV_CH))

    bav = ba_ref[...].astype(F32).reshape(rows, 128)
    zv = z_ref[...].astype(F32).reshape(rows, DN_VW)

    row = lax.broadcasted_iota(jnp.int32, (rows, rows), 0)
    col = lax.broadcasted_iota(jnp.int32, (rows, rows), 1)
    shift = int(math.log2(chunk))
    same = (row >> shift) == (col >> shift)
    tri = same & (row >= col)
    strict = same & (row > col)
    lane = lax.broadcasted_iota(jnp.int32, (rows, 128), 1)

    if rows == 128:
        sub = bav.T[0:2 * DN_HEADS, :]
        g_t = -jnp.exp(alogt_ref[...]) * _softplus(sub + dtbt_ref[...])
        gcum_t = jnp.dot(g_t, (same & (col >= row)).astype(F32), preferred_element_type=F32,
                         precision=lax.Precision.HIGHEST)
        live = lax.broadcasted_iota(jnp.int32, sub.shape, 0) < DN_HEADS
        packed = jnp.where(live, jax.nn.sigmoid(sub), gcum_t)
        cols = jnp.concatenate([packed, jnp.zeros((rows - 2 * DN_HEADS, rows), F32)], axis=0).T
        beta_full = gcum = cols
        grow = [gcum_t[DN_HEADS + h:DN_HEADS + h + 1, :] for h in heads]
    else:
        beta_full = jax.nn.sigmoid(bav)
        g_full = -jnp.exp(alog_ref[...]) * _softplus(bav + dtb_ref[...])
        gcum = jnp.dot(tri.astype(F32), g_full, preferred_element_type=F32, precision=lax.Precision.HIGHEST)
        ones = jnp.ones((rows, 128), F32)
        grow = [_dot_nt(ones, jnp.where(lane == DN_HEADS + h, gcum, 0.0), precision=lax.Precision.HIGHEST)
                for h in heads]
    gcol = [jnp.sum(jnp.where(lane == DN_HEADS + h, gcum, 0.0), axis=1, keepdims=True) for h in heads]
    bcol = [jnp.sum(jnp.where(lane == h, beta_full, 0.0), axis=1, keepdims=True) for h in heads]
    eg = [jnp.exp(gcol[h]) for h in heads]

    bshift = min(shift, 4)
    uw, qkd, qg = {}, {}, {}
    for hg in (heads[i:i + HEAD_GROUP] for i in range(0, DN_HEADS, HEAD_GROUP)):
        decay = {h: jnp.where(tri, jnp.exp(jnp.where(tri, gcol[h] - grow[h], 0.0)), 0.0) for h in hg}
        kb = {h: kn[h].astype(mm) for h in hg}
        kk = {h: _dot_nt(kb[h], kb[h]) for h in hg}
        qk = {h: _dot_nt(qn[h].astype(mm), kb[h]) for h in hg}
        m = {h: jnp.where(strict, kk[h] * decay[h] * bcol[h], 0.0) for h in hg}
        p = {h: jnp.where((row >> bshift) == (col >> bshift), -m[h], 0.0) for h in hg}
        xp = p
        for _ in range(bshift - 1):
            xb = {h: xp[h].astype(mm) for h in hg}
            xp = {h: _dot(xb[h], xb[h]) for h in hg}
            p = {h: p[h] + xp[h] + _dot(p[h].astype(mm), xp[h].astype(mm)) for h in hg}
        for s in range(bshift, shift):
            lower = ((row >> (s + 1)) == (col >> (s + 1))) & ((row >> s) > (col >> s))
            c_blk = {h: jnp.where(lower, m[h], 0.0) for h in hg}
            pb = {h: p[h].astype(mm) for h in hg}
            a = {h: c_blk[h] + _dot(pb[h], c_blk[h].astype(mm)) for h in hg}
            p = {h: p[h] - a[h] - _dot(a[h].astype(mm), pb[h]) for h in hg}
        for h in hg:
            rhs = jnp.concatenate([vh[h] * bcol[h], kn[h] * (bcol[h] * eg[h])], axis=1)
            uw[h] = rhs + _dot(p[h].astype(mm), rhs.astype(mm))
            qkd[h] = (qk[h] * decay[h]).astype(mm)
            qg[h] = qn[h] * eg[h]

    pairs = [(b, h) for b in range(bb) for h in heads]
    state = {bh: s_ref[bh[0], bh[1]] for bh in pairs}
    dlt = {h: [] for h in heads}
    oq = {h: [] for h in heads}
    for c in range(cpb):
        def rs(b):
            return slice(b * tb + c * chunk, b * tb + (c + 1) * chunk)
        prod = {}
        for b, h in pairs:
            lhs = jnp.concatenate([uw[h][rs(b), DN_DV:], qg[h][rs(b)]], axis=0).astype(mc)
            prod[b, h] = _dot(lhs, state[b, h].astype(mc))
        for b, h in pairs:
            r = rs(b)
            d = uw[h][r, :DN_DV] - prod[b, h][:chunk]
            glast = gcol[h][r.stop - 1:r.stop, :]
            kg = kn[h][r] * jnp.exp(glast - gcol[h][r])
            state[b, h] = state[b, h] * jnp.exp(glast) + _dot_tn(kg.astype(mc), d.astype(mc))
            dlt[h].append((r.start, d))
            oq[h].append((r.start, prod[b, h][chunk:]))
    for b, h in pairs:
        s_ref[b, h] = state[b, h]

    def stack(parts):
        parts = [v for _, v in sorted(parts, key=lambda sv: sv[0])]
        return parts[0] if len(parts) == 1 else jnp.concatenate(parts, axis=0)

    for h in heads:
        hs = slice(h * DN_DV, (h + 1) * DN_DV)
        o = stack(oq[h]) + _dot(qkd[h], stack(dlt[h]).astype(mm))
        zh = zv[:, hs]
        gated = _rms(o, onw_ref[...]) * (zh * jax.nn.sigmoid(zh))
        o_ref[:, :, hs] = gated.reshape(bb, tb, DN_DV).astype(o_ref.dtype)


def _delta(h3, cinit, s0, s0_layer, cw, gate, onw, bb, tb, chunk, stack=None, conv_done=False):
    b, t, _ = h3.shape
    assert b % bb == 0 and t % tb == 0 and tb % chunk == 0
    state_block = (None, bb, DN_HEADS, DN_DK, DN_DV)
    in_specs = [
        pl.BlockSpec((bb, tb, CONV_CH), lambda i, j: (i, j, QKV_OFF // CONV_CH)),
        pl.BlockSpec((bb, tb, DN_VW), lambda i, j: (i, j, Z_OFF // DN_VW)),
        pl.BlockSpec((bb, tb, 128), lambda i, j: (i, j, BA_OFF // 128)),
        pl.BlockSpec((bb, CONV_HALO, CONV_CH), lambda i, j: (i, 0, 0)),
        pl.BlockSpec(state_block, lambda i, j: (s0_layer, i, 0, 0, 0)),
        pl.BlockSpec((CONV_W, CONV_CH), lambda i, j: (0, 0)),
        pl.BlockSpec((1, 128), lambda i, j: (0, 0)),
        pl.BlockSpec((1, 128), lambda i, j: (0, 0)),
        pl.BlockSpec((2 * DN_HEADS, 128), lambda i, j: (0, 0)),
        pl.BlockSpec((2 * DN_HEADS, 128), lambda i, j: (0, 0)),
        pl.BlockSpec((1, DN_DV), lambda i, j: (0, 0)),
    ]
    args = [h3, h3, h3, cinit, s0, cw, *gate, onw]
    layer, depth, prev = (0, 1, None) if stack is None else stack
    aliases = {}
    if prev is not None:
        in_specs.append(pl.BlockSpec(memory_space=pl.ANY))
        aliases = {len(args): 1}
        args.append(prev)
    o, s = pl.pallas_call(
        functools.partial(_delta_kernel, bb=bb, tb=tb, chunk=chunk, conv_done=conv_done),
        grid=(b // bb, t // tb),
        in_specs=in_specs,
        out_specs=[
            pl.BlockSpec((bb, tb, DN_VW), lambda i, j: (i, j, 0)),
            pl.BlockSpec(state_block, lambda i, j: (layer, i, 0, 0, 0)),
        ],
        out_shape=[
            jax.ShapeDtypeStruct((b, t, DN_VW), h3.dtype),
            jax.ShapeDtypeStruct((depth, b, DN_HEADS, DN_DK, DN_DV), F32),
        ],
        scratch_shapes=[pltpu.VMEM((bb, CONV_HALO + tb, CONV_CH), F32)],
        input_output_aliases=aliases,
        compiler_params=_cparams(("parallel", "arbitrary")),
        name="delta",
    )(*args)
    return o, (s[0] if stack is None else s)


def _rope(x, cos, sin):
    width = x.shape[-1]
    reps = width // cos.shape[-1]
    if reps > 1:
        cos = jnp.concatenate([cos] * reps, axis=1)
        sin = jnp.concatenate([sin] * reps, axis=1)
    lane = lax.broadcasted_iota(jnp.int32, x.shape, 1)
    first_half = (lane & (SWA_HD - 1)) < (SWA_HD // 2)
    other = jnp.where(first_half, pltpu.roll(x, width - SWA_HD // 2, 1), pltpu.roll(x, SWA_HD // 2, 1))
    return x * cos + other * sin


def _pad_heads(x, lo):
    xr = pltpu.roll(x, SWA_HD, 1)
    z = jnp.zeros_like(x)
    return [jnp.where(lo, x, z), jnp.where(lo, z, xr), jnp.where(lo, xr, z), jnp.where(lo, z, x)]


def _swa_prompt_kernel(sinks_ref, q_ref, kv_ref, meta_ref, cos_ref, sin_ref, o_ref, klast_ref, kcat_ref, vcat_ref,
                       *, pos0):
    n = pl.program_id(1)
    blk = SWA_WINDOW
    lo = lax.broadcasted_iota(jnp.int32, (blk, 128), 1) < SWA_HD

    def put(half, k, v):
        rows = slice(half * blk, (half + 1) * blk)
        for i, (a, b) in enumerate(zip(_pad_heads(k, lo), _pad_heads(v, lo))):
            kcat_ref[i, rows, :] = a.astype(BF16)
            vcat_ref[i, rows, :] = b.astype(BF16)

    @pl.when(n == 0)
    def _():
        put(0, meta_ref[:, :SWA_KV_WIDTH], meta_ref[:, SWA_KV_WIDTH:])

    @pl.when(n > 0)
    def _():
        kcat_ref[:, 0:blk, :] = kcat_ref[:, blk:2 * blk, :]
        vcat_ref[:, 0:blk, :] = vcat_ref[:, blk:2 * blk, :]

    cos = cos_ref[...]
    sin = sin_ref[...]
    q = _rope(q_ref[0].astype(F32), cos, sin) * (SWA_HD ** -0.5)
    kv = kv_ref[0].astype(F32)
    k = _rope(kv[:, :SWA_KV_WIDTH], cos, sin)
    klast_ref[0] = k
    put(1, k, kv[:, SWA_KV_WIDTH:])

    r = lax.broadcasted_iota(jnp.int32, (blk, 2 * blk), 0)
    j = lax.broadcasted_iota(jnp.int32, (blk, 2 * blk), 1)
    jmin = jnp.maximum(blk - pos0 - n * blk, 0)
    mask = ((j < blk) & (j > r) & (j >= jmin)) | ((j >= blk) & ((j - blk) <= r))
    heads = range(SWA_HEADS)
    qt = [q[:, i * 128:(i + 1) * 128].astype(BF16) for i in range(SWA_HEADS // 2)]
    s = [jnp.where(mask, _dot_nt(qt[h // 2], kcat_ref[2 * (h // SWA_GROUP) + h % 2]), NEG_BIG) for h in heads]
    m = [jnp.maximum(jnp.max(s[h], axis=1, keepdims=True), sinks_ref[h]) for h in heads]
    e = [jnp.exp(s[h] - m[h]).astype(BF16) for h in heads]
    ones = jnp.ones((2 * blk, 128), BF16)
    den = [_dot(e[h], ones) + jnp.exp(sinks_ref[h] - m[h]) for h in heads]
    for i in range(SWA_HEADS // 2):
        he, ho = 2 * i, 2 * i + 1
        g = he // SWA_GROUP
        acc = _dot(e[he], vcat_ref[2 * g]) + _dot(e[ho], vcat_ref[2 * g + 1])
        o_ref[0, :, i * 128:(i + 1) * 128] = (acc / jnp.where(lo, den[he], den[ho])).astype(o_ref.dtype)


def _swa_prompt(h3, meta_kv, cos, sin, sinks, pos0):
    b, t, _ = h3.shape
    blk = SWA_WINDOW
    assert t % blk == 0
    return pl.pallas_call(
        functools.partial(_swa_prompt_kernel, pos0=pos0),
        grid=(b, t // blk),
        in_specs=[
            pl.BlockSpec(memory_space=pltpu.SMEM),
            pl.BlockSpec((1, blk, SWA_WIDTH), lambda i, n: (i, n, QC_OFF // SWA_WIDTH)),
            pl.BlockSpec((1, blk, 256), lambda i, n: (i, n, KVC_OFF // 256)),
            pl.BlockSpec((blk, 256), lambda i, n: (0, 0)),
            pl.BlockSpec((blk, 128), lambda i, n: (n, 0)),
            pl.BlockSpec((blk, 128), lambda i, n: (n, 0)),
        ],
        out_specs=[
            pl.BlockSpec((1, blk, SWA_WIDTH), lambda i, n: (i, n, 0)),
            pl.BlockSpec((1, blk, SWA_KV_WIDTH), lambda i, n: (i, 0, 0)),
        ],
        out_shape=[
            jax.ShapeDtypeStruct((b, t, SWA_WIDTH), h3.dtype),
            jax.ShapeDtypeStruct((b, blk, SWA_KV_WIDTH), F32),
        ],
        scratch_shapes=[pltpu.VMEM((4, 2 * blk, 128), BF16), pltpu.VMEM((4, 2 * blk, 128), BF16)],
        compiler_params=_cparams(("parallel", "arbitrary")),
        name="swa_prompt",
    )(sinks, h3, h3, meta_kv, cos, sin)


def _swa_step_kernel(sink_ref, q_ref, kv_ref, *rest, bb, t_new, window):
    if window:
        kc_ref, vc_ref, cos_ref, sin_ref = rest[:4]
        o_ref, kout_ref, vout_ref = rest[-3:]
    else:
        cos_ref, sin_ref, o_ref, knew_ref = rest
    cos = cos_ref[...]
    sin = sin_ref[...]
    rows = SWA_HEADS * t_new
    mn = BF16 if t_new >= 16 else F32
    lo = lax.broadcasted_iota(jnp.int32, (t_new, 128), 1) < SWA_HD
    tq = lax.broadcasted_iota(jnp.int32, (rows, t_new), 0) & (t_new - 1)
    mask_n = lax.broadcasted_iota(jnp.int32, (rows, t_new), 1) <= tq
    if window:
        diff = ((lax.broadcasted_iota(jnp.int32, (rows, window), 0) & (t_new - 1)) + window
                - lax.broadcasted_iota(jnp.int32, (rows, window), 1))
        mask_c = (diff >= 0) & (diff < SWA_WINDOW)
    sink = sink_ref[...]
    seqs = range(bb)

    qall, kk, vv = [], [], []
    for s in seqs:
        q = _rope(q_ref[s].astype(F32), cos, sin) * (SWA_HD ** -0.5)
        kv = kv_ref[s].astype(F32)
        k = _rope(kv[:, :SWA_KV_WIDTH], cos, sin)
        if not window:
            knew_ref[s] = k
        pieces = []
        for h in range(SWA_HEADS):
            g = h // SWA_GROUP
            tile = q[:, (h // 2) * 128:(h // 2 + 1) * 128]
            if h % 2 != g:
                tile = pltpu.roll(tile, SWA_HD, 1)
            pieces.append(jnp.where(lo, tile, 0.0) if g == 0 else jnp.where(lo, 0.0, tile))
        qall.append(jnp.concatenate(pieces, axis=0))
        kk.append(k)
        vv.append(kv[:, SWA_KV_WIDTH:])

    s_n = [jnp.where(mask_n, _dot_nt(qall[s].astype(mn), kk[s].astype(mn)), NEG_BIG) for s in seqs]
    m = [jnp.maximum(jnp.max(s_n[s], axis=1, keepdims=True), sink) for s in seqs]
    if window:
        s_c = [jnp.where(mask_c, _dot(qall[s].astype(BF16), kc_ref[s].astype(BF16)), NEG_BIG) for s in seqs]
        m = [jnp.maximum(m[s], jnp.max(s_c[s], axis=1, keepdims=True)) for s in seqs]
    e_n = [jnp.exp(s_n[s] - m[s]) for s in seqs]
    den = [jnp.sum(e_n[s], axis=1, keepdims=True) + jnp.exp(sink - m[s]) for s in seqs]
    acc = [_dot(e_n[s].astype(mn), vv[s].astype(mn)) for s in seqs]
    if window:
        e_c = [jnp.exp(s_c[s] - m[s]) for s in seqs]
        den = [den[s] + jnp.sum(e_c[s], axis=1, keepdims=True) for s in seqs]
        acc = [acc[s] + _dot_nt(e_c[s].astype(BF16), vc_ref[s].astype(BF16)) for s in seqs]

        def slide(old_t, new):
            kept = pltpu.roll(old_t, window - t_new, 1)
            fresh = jnp.concatenate([jnp.zeros((window - t_new, SWA_KV_WIDTH), F32), new], axis=0).T
            newest = lax.broadcasted_iota(jnp.int32, kept.shape, 1) >= window - t_new
            return jnp.where(newest, fresh, kept)

        for s in seqs:
            kout_ref[s] = slide(kc_ref[s], kk[s])
            vout_ref[s] = slide(vc_ref[s], vv[s])
    for s in seqs:
        a = acc[s] / den[s]
        for i in range(SWA_HEADS // 2):
            he, ho = 2 * i, 2 * i + 1
            g = he // SWA_GROUP
            a_e = a[he * t_new:(he + 1) * t_new]
            a_o = a[ho * t_new:(ho + 1) * t_new]
            if g == 1:
                a_e = pltpu.roll(a_e, SWA_HD, 1)
            else:
                a_o = pltpu.roll(a_o, SWA_HD, 1)
            o_ref[s, :, i * 128:(i + 1) * 128] = jnp.where(lo, a_e, a_o).astype(o_ref.dtype)


def _swa_step(h3, caches, cos, sin, sinks, bb):
    b, t, _ = h3.shape
    assert b % bb == 0
    sink_rows = jnp.repeat(sinks, t)[:, None]
    in_specs = [
        pl.BlockSpec((SWA_HEADS * t, 1), lambda i: (0, 0)),
        pl.BlockSpec((bb, t, SWA_WIDTH), lambda i: (i, 0, QC_OFF // SWA_WIDTH)),
        pl.BlockSpec((bb, t, 256), lambda i: (i, 0, KVC_OFF // 256)),
    ]
    args = [sink_rows, h3, h3]
    out_specs = [pl.BlockSpec((bb, t, SWA_WIDTH), lambda i: (i, 0, 0))]
    out_shape = [jax.ShapeDtypeStruct((b, t, SWA_WIDTH), h3.dtype)]
    window = 0
    aliases = {}
    if caches is None:
        out_specs.append(pl.BlockSpec((bb, t, SWA_KV_WIDTH), lambda i: (i, 0, 0)))
        out_shape.append(jax.ShapeDtypeStruct((b, t, SWA_KV_WIDTH), F32))
    else:
        k_t, v_t, layer, depth, prev_k, prev_v = caches
        window = k_t.shape[-1]
        assert window == SWA_KV_WIDTH and t <= window
        cache_block = pl.BlockSpec((None, bb, SWA_KV_WIDTH, window), lambda i: (layer, i, 0, 0))
        in_specs += [cache_block] * 2
        args += [k_t, v_t]
        out_specs += [cache_block] * 2
        out_shape += [jax.ShapeDtypeStruct((depth, b, SWA_KV_WIDTH, window), F32)] * 2
    in_specs += [pl.BlockSpec((t, 128), lambda i: (0, 0))] * 2
    args += [cos, sin]
    if caches is not None and prev_k is not None:
        in_specs += [pl.BlockSpec(memory_space=pl.ANY)] * 2
        aliases = {len(args): 1, len(args) + 1: 2}
        args += [prev_k, prev_v]
    return pl.pallas_call(
        functools.partial(_swa_step_kernel, bb=bb, t_new=t, window=window),
        grid=(b // bb,),
        in_specs=in_specs,
        out_specs=out_specs,
        out_shape=out_shape,
        input_output_aliases=aliases,
        compiler_params=_cparams(("parallel",)),
        name="swa_step",
    )(*args)


def _merge_kernel(oa_ref, ob_ref, oc_ref, ga_ref, gb_ref, gc_ref, x_ref, pa_ref, pb_ref, pc_ref, wo_ref, o_ref):
    def branch(o_r, g_r, p_r):
        return jax.nn.sigmoid(g_r[...].astype(F32)) * jnp.dot(o_r[...].astype(BF16), p_r[...], preferred_element_type=F32)

    m = branch(oa_ref, ga_ref, pa_ref) + branch(ob_ref, gb_ref, pb_ref) + branch(oc_ref, gc_ref, pc_ref)
    o_ref[...] = x_ref[...] + jnp.dot(m.astype(BF16), wo_ref[...], preferred_element_type=F32)


def _merge(oa, ob, oc, h, x, pa, pb, pc, wo):
    m = x.shape[0]
    tm = min(m, 512)
    assert m % tm == 0
    row =lambda w: pl.BlockSpec((tm, w), lambda i: (i, 0))
    gate = lambda c: pl.BlockSpec((tm, D_MODEL), lambda i: (i, c))
    full = lambda a: pl.BlockSpec(a.shape, lambda i: (0, 0))
    return pl.pallas_call(
        _merge_kernel,
        grid=(m // tm,),
        in_specs=[row(512), row(512), row(512), gate(0), gate(1), gate(2), row(D_MODEL),
                  full(pa), full(pb), full(pc), full(wo)],
        out_specs=row(D_MODEL),
        out_shape=jax.ShapeDtypeStruct((m, D_MODEL), F32),
        compiler_params=_cparams(("parallel",)),
        name="merge",
    )(oa, ob, oc, h, h, h, x, pa, pb, pc, wo)


def _mlp_kernel(x_ref, nw_ref, wu_ref, wd_ref, fnw_ref, o_ref, xn_ref, acc_ref, *, final):
    f = pl.program_id(1)

    @pl.when(f == 0)
    def _():
        x = x_ref[...]
        xn_ref[...] = _rms(x, nw_ref[...]).astype(BF16)
        acc_ref[...] = x

    a = jnp.maximum(jnp.dot(xn_ref[...], wu_ref[...], preferred_element_type=F32), 0.0)
    acc_ref[...] += jnp.dot((a * a).astype(BF16), wd_ref[...], preferred_element_type=F32)

    @pl.when(f == pl.num_programs(1) - 1)
    def _():
        y = acc_ref[...]
        o_ref[...] = _rms(y, fnw_ref[...]) if final else y


def _mlp(x, nw, wu, wd, fnw, final):
    m = x.shape[0]
    tm = min(m, 1024)
    assert m % tm == 0
    tf = 1024
    return pl.pallas_call(
        functools.partial(_mlp_kernel, final=final),
        grid=(m // tm, D_FF // tf),
        in_specs=[
            pl.BlockSpec((tm, D_MODEL), lambda i, f: (i, 0)),
            pl.BlockSpec((1, D_MODEL), lambda i, f: (0, 0)),
            pl.BlockSpec((D_MODEL, tf), lambda i, f: (0, f)),
            pl.BlockSpec((tf, D_MODEL), lambda i, f: (f, 0)),
            pl.BlockSpec((1, D_MODEL), lambda i, f: (0, 0)),
        ],
        out_specs=pl.BlockSpec((tm, D_MODEL), lambda i, f: (i, 0)),
        out_shape=jax.ShapeDtypeStruct((m, D_MODEL), F32),
        scratch_shapes=[pltpu.VMEM((tm, D_MODEL), BF16), pltpu.VMEM((tm, D_MODEL), F32)],
        compiler_params=_cparams(("parallel", "arbitrary")),
        name="mlp",
    )(x, nw, wu, wd, fnw)


def _rope_tables(pos0, t):
    half = SWA_HD // 2
    inv = ROPE_THETA ** (-jnp.arange(half, dtype=F32) / half)
    ang = (pos0 + jnp.arange(t)).astype(F32)[:, None] * inv[None, :]
    cos = jnp.cos(ang)
    sin = jnp.sin(ang)
    return jnp.tile(cos, (1, 4)), jnp.tile(jnp.concatenate([-sin, sin], axis=1), (1, 2))


def _prep_layer(l, norm1_w, w_in, pool_w, pool_scale, dn_conv_w, dn_a_log, dn_dt_bias, dn_onorm_w, swa_sinks,
                proj_a, proj_b, proj_c, w_out, norm2_w, w_up, w_down):
    w = w_in[l]
    segments = ((3336, 6408), (512, 2048), (0, 512), (2048, 2560), (2568, 3080), (3080, 3336), (2560, 2568))
    w_perm = jnp.concatenate(
        [w[:, a:b].astype(BF16) for a, b in segments]
        + [jnp.zeros((D_MODEL, H_WIDTH - BA_OFF - 2 * DN_HEADS), BF16)], axis=1)
    lane_pad = lambda v: jnp.zeros((1, 128), F32).at[0, DN_HEADS:2 * DN_HEADS].set(v.astype(F32))
    row_pad = lambda v: jnp.zeros((2 * DN_HEADS, 128), F32).at[DN_HEADS:].set(
        jnp.broadcast_to(v.astype(F32)[:, None], (DN_HEADS, 128)))
    return dict(
        norm1=norm1_w[l][None].astype(F32), w_in=w_perm,
        pool_w=pool_w[l].astype(BF16), pool_scale=pool_scale[l][None].astype(F32),
        conv_w=dn_conv_w[l].astype(F32),
        gate=(lane_pad(dn_a_log[l]), lane_pad(dn_dt_bias[l]), row_pad(dn_a_log[l]), row_pad(dn_dt_bias[l])),
        onw=dn_onorm_w[l][None].astype(F32), sinks=swa_sinks[l].astype(F32),
        pa=proj_a[l].astype(BF16), pb=proj_b[l].astype(BF16), pc=proj_c[l].astype(BF16),
        wo=w_out[l].astype(BF16), norm2=norm2_w[l][None].astype(F32),
        wu=w_up[l].astype(BF16), wd=w_down[l].astype(BF16))


def _group_step(x3, p, fnw, final, pos0, pool_init, conv_init, s0, kind, swa_extra, pool_tiles, delta_tiles,
                swa_bb=1, s_stack=None, act=F32, fuse_conv=False):
    b, t, _ = x3.shape
    x2 = x3.reshape(b * t, D_MODEL)
    h, conv_tails = _inproj(x2, p['norm1'], p['w_in'], act,
                            conv=(conv_init, p['conv_w'], t) if fuse_conv else None)
    h3 = h.reshape(b, t, H_WIDTH)
    o_a = _pool(h3, pool_init, p['pool_w'], p['pool_scale'], pos0, *pool_tiles)
    o_b, s_new = _delta(h3, conv_init, s0[0], s0[1], p['conv_w'], p['gate'], p['onw'], *delta_tiles,
                        stack=s_stack, conv_done=fuse_conv)
    cos, sin = _rope_tables(pos0, t)
    if kind == 'prompt':
        o_c, k_rot = _swa_prompt(h3, swa_extra, cos, sin, p['sinks'], pos0)
    else:
        o_c, *k_rot = _swa_step(h3, swa_extra, cos, sin, p['sinks'], swa_bb)
    m = b * t
    h1 = _merge(o_a.reshape(m, 512), o_b.reshape(m, 512), o_c.reshape(m, 512), h, x2,
                p['pa'], p['pb'], p['pc'], p['wo'])
    out = _mlp(h1, p['norm2'], p['wu'], p['wd'], fnw, final)
    return out.reshape(b, t, D_MODEL), h3, s_new, k_rot, conv_tails


def kernel(x_prompt, x_sample, state_pool, state_conv, state_delta, cache_swa_k, cache_swa_v, meta_tokens, norm1_w, w_in, pool_w, pool_scale, dn_conv_w, dn_a_log, dn_dt_bias, dn_onorm_w, swa_sinks, proj_a, proj_b, proj_c, w_out, norm2_w, w_up, w_down, final_norm_w):
    depth = w_in.shape[0]
    bp, tp, _ = x_prompt.shape
    bs, ts, _ = x_sample.shape
    past = cache_swa_k.shape[2]
    fnw = final_norm_w[None].astype(F32)

    xm = meta_tokens[None].astype(F32)
    xp = x_prompt
    xs = x_sample
    outs = {k: [] for k in ('pool_p', 'conv_p', 'k_p', 'v_p', 'pool_s', 'conv_s')}
    delta_p = delta_s = None
    k_s = v_s = None
    to_t = lambda c: c.transpose(0, 1, 3, 4, 2).reshape(depth, bs, SWA_KV_WIDTH, past)
    cache_k_t, cache_v_t = to_t(cache_swa_k), to_t(cache_swa_v)
    for l in range(depth):
        p = _prep_layer(l, norm1_w, w_in, pool_w, pool_scale, dn_conv_w, dn_a_log, dn_dt_bias, dn_onorm_w,
                        swa_sinks, proj_a, proj_b, proj_c, w_out, norm2_w, w_up, w_down)
        final = l == depth - 1

        xm, hm, s_m, k_m, _ = _group_step(
            xm, p, fnw, final, 0,
            jnp.zeros((1, POOL_HALO, POOL_WIDTH), F32), jnp.zeros((1, CONV_HALO, CONV_CH), F32),
            (jnp.zeros((1, 1, DN_HEADS, DN_DK, DN_DV), F32), 0), 'step', None,
            (1, N_META), (1, N_META, N_META))

        meta_kv = jnp.zeros((SWA_WINDOW, 256), F32)
        meta_kv = meta_kv.at[SWA_WINDOW - N_META:, :SWA_KV_WIDTH].set(k_m[0][0])
        meta_kv = meta_kv.at[SWA_WINDOW - N_META:, SWA_KV_WIDTH:].set(hm[0, :, KVC_OFF + SWA_KV_WIDTH:KVC_OFF + 256])
        xp, hp, s_p, k_p, tails_p = _group_step(
            xp, p, fnw, final, N_META,
            jnp.broadcast_to(hm[:, :, U_OFF:U_OFF + POOL_WIDTH], (bp, POOL_HALO, POOL_WIDTH)),
            jnp.broadcast_to(hm[:, N_META - CONV_HALO:, QKV_OFF:QKV_OFF + CONV_CH], (bp, CONV_HALO, CONV_CH)),
            (jnp.broadcast_to(s_m[None], (1, bp, DN_HEADS, DN_DK, DN_DV)), 0), 'prompt', meta_kv,
            (1, 512), (2, DN_CHUNK, DN_CHUNK), s_stack=(l, depth, delta_p), act=BF16, fuse_conv=True)
        delta_p = s_p
        outs['pool_p'].append(hp[:, tp - POOL_BUF:, U_OFF:U_OFF + POOL_WIDTH].astype(F32))
        if tails_p is None:
            outs['conv_p'].append(hp[:, tp - (CONV_W - 1):, QKV_OFF:QKV_OFF + CONV_CH].astype(F32))
        else:
            outs['conv_p'].append(tails_p.reshape(bp, -1, CONV_HALO, CONV_CH)[:, -1, CONV_HALO - (CONV_W - 1):])
        outs['k_p'].append(k_p.reshape(bp, SWA_WINDOW, SWA_KV_HEADS, SWA_HD))
        outs['v_p'].append(hp[:, tp - SWA_WINDOW:, KVC_OFF + SWA_KV_WIDTH:KVC_OFF + 256].astype(F32)
                           .reshape(bp, SWA_WINDOW, SWA_KV_HEADS, SWA_HD))

        xs, hs, s_s, (k_s, v_s), _ = _group_step(
            xs, p, fnw, final, PAST_LEN,
            jnp.concatenate([jnp.zeros((bs, POOL_HALO - POOL_BUF, POOL_WIDTH), F32), state_pool[l]], axis=1),
            jnp.concatenate([jnp.zeros((bs, CONV_HALO - (CONV_W - 1), CONV_CH), F32), state_conv[l]], axis=1),
            (state_delta, l), 'step', (cache_k_t, cache_v_t, l, depth, k_s, v_s),
            (bs, ts), (16, ts, ts), 16, s_stack=(l, depth, delta_s))
        delta_s = s_s
        outs['pool_s'].append(jnp.concatenate([state_pool[l], hs[:, :, U_OFF:U_OFF + POOL_WIDTH]], axis=1)[:, -POOL_BUF:])
        outs['conv_s'].append(jnp.concatenate([state_conv[l], hs[:, :, QKV_OFF:QKV_OFF + CONV_CH]], axis=1)[:, -(CONV_W - 1):])

    st = {k: jnp.stack(v) for k, v in outs.items()}
    from_t = lambda c: c.reshape(depth, bs, SWA_KV_HEADS, SWA_HD, past).transpose(0, 1, 4, 2, 3)
    return (xp, xs, st['pool_p'], st['conv_p'], delta_p, st['k_p'], st['v_p'],
            st['pool_s'], st['conv_s'], delta_s, from_t(k_s), from_t(v_s))
```

```python
import functools
import math

import jax
import jax.numpy as jnp
from jax import lax
from jax.experimental import pallas as pl
from jax.experimental.pallas import tpu as pltpu

F32 = jnp.float32
BF16 = jnp.bfloat16

D_MODEL = 1024
N_META = 16
EPS = 1e-6
POOL_GROUPS = 4
POOL_GROUP_DIM = 128
POOL_WIDTH = 512
POOL_WINDOWS = (2, 4, 8, 16)
POOL_BUF = 15
POOL_HALO = 16
DN_HEADS = 4
DN_DK = 128
DN_DV = 128
DN_QK = 512
DN_VW = 512
CONV_W = 4
CONV_CH = 1536
CONV_HALO = 8
DN_CHUNK = 64
HEAD_GROUP = 4
SWA_HEADS = 8
SWA_KV_HEADS = 2
SWA_GROUP = 4
SWA_HD = 64
SWA_WIDTH = 512
SWA_KV_WIDTH = 128
SWA_WINDOW = 128
ROPE_THETA = 10000.0
D_FF = 4096
MLP_TF = 1024
PAST_LEN = 16384

G_OFF = 0
QKV_OFF = 3072
U_OFF = 4608
Z_OFF = 5120
QC_OFF = 5632
KVC_OFF = 6144
BA_OFF = 6400
H_WIDTH = 6528
IN_CHUNKS = ((0, 1536), (1536, 3072), (3072, 4608), (4608, 6144), (6144, 6528))

VMEM_LIMIT = 56 * 1024 * 1024
NEG_BIG = -1e30


def _cparams(sem):
    return pltpu.CompilerParams(dimension_semantics=sem, vmem_limit_bytes=VMEM_LIMIT)


def _rms(x, w):
    return x * lax.rsqrt(jnp.mean(x * x, axis=-1, keepdims=True) + EPS) * w


def _l2n(v):
    return v * lax.rsqrt(jnp.sum(v * v, axis=-1, keepdims=True) + EPS)


def _short_conv(ext_ref, cw_ref, rows):
    base = CONV_HALO - (CONV_W - 1)
    y = ext_ref[..., base:base + rows, :] * cw_ref[0:1, :]
    for j in range(1, CONV_W):
        y = y + ext_ref[..., base + j:base + j + rows, :] * cw_ref[j:j + 1, :]
    return y * jax.nn.sigmoid(y)


def _qkv_heads(y):
    heads = range(DN_HEADS)
    qn = [_l2n(y[:, h * DN_DK:(h + 1) * DN_DK]) * (DN_DK ** -0.5) for h in heads]
    kn = [_l2n(y[:, DN_QK + h * DN_DK:DN_QK + (h + 1) * DN_DK]) for h in heads]
    vh = [y[:, 2 * DN_QK + h * DN_DV:2 * DN_QK + (h + 1) * DN_DV] for h in heads]
    return qn, kn, vh


def _inproj_kernel(x_ref, nw_ref, w_ref, *rest, tiles_per_seq):
    if tiles_per_seq:
        cinit_ref, cw_ref, o_ref, tail_ref, ext_ref = rest
    else:
        (o_ref,) = rest
    tm = x_ref.shape[0]
    if tiles_per_seq:
        first = pl.program_id(0) % tiles_per_seq == 0

        @pl.when(first)
        def _():
            ext_ref[0:CONV_HALO, :] = cinit_ref[0]

        @pl.when(jnp.logical_not(first))
        def _():
            ext_ref[0:CONV_HALO, :] = ext_ref[tm:tm + CONV_HALO, :]

    xn = _rms(x_ref[...], nw_ref[...]).astype(BF16)
    for a, b in sorted(IN_CHUNKS, key=lambda ab: ab[0] != QKV_OFF):
        acc = jnp.dot(xn, w_ref[:, a:b], preferred_element_type=F32)
        if tiles_per_seq and a == QKV_OFF:
            ext_ref[CONV_HALO:CONV_HALO + tm, :] = acc
            tail_ref[0] = acc[tm - CONV_HALO:, :]
            qn, kn, vh = _qkv_heads(_short_conv(ext_ref, cw_ref, tm))
            acc = jnp.concatenate(qn + kn + vh, axis=1)
        o_ref[:, a:b] = acc.astype(o_ref.dtype)


def _inproj(x, nw, w, act, conv=None):
    m = x.shape[0]
    tm = min(m, 512 if act == BF16 else 256)
    assert m % tm == 0
    in_specs = [
        pl.BlockSpec((tm, D_MODEL), lambda i: (i, 0)),
        pl.BlockSpec((1, D_MODEL), lambda i: (0, 0)),
        pl.BlockSpec((D_MODEL, H_WIDTH), lambda i: (0, 0), pipeline_mode=pl.Buffered(1)),
    ]
    out_specs = [pl.BlockSpec((tm, H_WIDTH), lambda i: (i, 0))]
    out_shape = [jax.ShapeDtypeStruct((m, H_WIDTH), act)]
    args = [x, nw, w]
    scratch = []
    tiles_per_seq = 0
    if conv is not None:
        cinit, cw, seq_rows = conv
        assert seq_rows % tm == 0
        tiles_per_seq = seq_rows // tm
        in_specs += [pl.BlockSpec((1, CONV_HALO, CONV_CH), lambda i: (i // tiles_per_seq, 0, 0)),
                     pl.BlockSpec((CONV_W, CONV_CH), lambda i: (0, 0))]
        args += [cinit, cw]
        out_specs.append(pl.BlockSpec((1, CONV_HALO, CONV_CH), lambda i: (i, 0, 0)))
        out_shape.append(jax.ShapeDtypeStruct((m // tm, CONV_HALO, CONV_CH), F32))
        scratch = [pltpu.VMEM((CONV_HALO + tm, CONV_CH), F32)]
    res = pl.pallas_call(
        functools.partial(_inproj_kernel, tiles_per_seq=tiles_per_seq),
        grid=(m // tm,),
        in_specs=in_specs,
        out_specs=out_specs,
        out_shape=out_shape,
        scratch_shapes=scratch,
        compiler_params=_cparams(("arbitrary",) if conv is not None else ("parallel",)),
        name="inproj",
    )(*args)
    return (res[0], res[1]) if conv is not None else (res[0], None)


def _pool_kernel(u_ref, init_ref, pw_ref, ps_ref, o_ref, ext_ref, *, pos0, tt):
    t = pl.program_id(1)
    bb = u_ref.shape[0]

    @pl.when(t == 0)
    def _():
        ext_ref[:, 0:POOL_HALO, :] = init_ref[...]

    @pl.when(t > 0)
    def _():
        ext_ref[:, 0:POOL_HALO, :] = ext_ref[:, tt:tt + POOL_HALO, :]

    ext_ref[:, POOL_HALO:POOL_HALO + tt, :] = u_ref[...].astype(F32)

    pos = pos0 + t * tt + lax.broadcasted_iota(jnp.int32, (1, tt, 1), 1)
    for g, w in enumerate(POOL_WINDOWS):
        cs = slice(g * POOL_GROUP_DIM, (g + 1) * POOL_GROUP_DIM)
        x = ext_ref[:, POOL_HALO:POOL_HALO + tt, cs]
        s = x
        for k in range(1, w):
            s = s + ext_ref[:, POOL_HALO - k:POOL_HALO - k + tt, cs]
        cnt = jnp.minimum(w, pos + 1).astype(F32)
        d = (s / cnt - x).reshape(bb * tt, POOL_GROUP_DIM)
        y = jnp.dot(d.astype(BF16), pw_ref[g], preferred_element_type=F32)
        o_ref[:, :, cs] = (y * ps_ref[:, cs]).reshape(bb, tt, POOL_GROUP_DIM).astype(o_ref.dtype)


def _pool(h3, init, pw, ps, pos0, bb, tt):
    b, t, _ = h3.shape
    return pl.pallas_call(
        functools.partial(_pool_kernel, pos0=pos0, tt=tt),
        grid=(b // bb, t // tt),
        in_specs=[
            pl.BlockSpec((bb, tt, POOL_WIDTH), lambda i, j: (i, j, U_OFF // POOL_WIDTH)),
            pl.BlockSpec((bb, POOL_HALO, POOL_WIDTH), lambda i, j: (i, 0, 0)),
            pl.BlockSpec((POOL_GROUPS, POOL_GROUP_DIM, POOL_GROUP_DIM), lambda i, j: (0, 0, 0)),
            pl.BlockSpec((1, POOL_WIDTH), lambda i, j: (0, 0)),
        ],
        out_specs=pl.BlockSpec((bb, tt, POOL_WIDTH), lambda i, j: (i, j, 0)),
        out_shape=jax.ShapeDtypeStruct((b, t, POOL_WIDTH), h3.dtype),
        scratch_shapes=[pltpu.VMEM((bb, POOL_HALO + tt, POOL_WIDTH), F32)],
        compiler_params=_cparams(("parallel", "arbitrary")),
        name="pool",
    )(h3, init, pw, ps)


def _softplus(x):
    return jnp.maximum(x, 0.0) + jnp.log1p(jnp.exp(-jnp.abs(x)))


def _dot_nt(a, b, **kw):
    return lax.dot_general(a, b, (((1,), (1,)), ((), ())), preferred_element_type=F32, **kw)


def _dot_tn(a, b):
    return lax.dot_general(a, b, (((0,), (0,)), ((), ())), preferred_element_type=F32)


def _dot(a, b):
    return jnp.dot(a, b, preferred_element_type=F32)


def _delta_kernel(qkv_ref, z_ref, ba_ref, cinit_ref, s0_ref, cw_ref, alog_ref, dtb_ref, alogt_ref, dtbt_ref,
                  onw_ref, *rest, bb, tb, chunk, conv_done):
    o_ref, s_ref, ext_ref = rest[-3:]
    t = pl.program_id(1)
    rows = bb * tb
    cpb = tb // chunk
    heads = list(range(DN_HEADS))
    mm = BF16 if rows >= 16 else F32
    mc = BF16 if chunk >= 16 else F32

    @pl.when(t == 0)
    def _():
        s_ref[...] = s0_ref[...]

    if conv_done:
        y = qkv_ref[...].astype(F32).reshape(rows, CONV_CH)
        qn = [y[:, h * DN_DK:(h + 1) * DN_DK] for h in heads]
        kn = [y[:, DN_QK + h * DN_DK:DN_QK + (h + 1) * DN_DK] for h in heads]
        vh = [y[:, 2 * DN_QK + h * DN_DV:2 * DN_QK + (h + 1) * DN_DV] for h in heads]
    else:
        @pl.when(t == 0)
        def _():
            ext_ref[:, 0:CONV_HALO, :] = cinit_ref[...]

        @pl.when(t > 0)
        def _():
            ext_ref[:, 0:CONV_HALO, :] = ext_ref[:, tb:tb + CONV_HALO, :]

        ext_ref[:, CONV_HALO:CONV_HALO + tb, :] = qkv_ref[...].astype(F32)
        qn, kn, vh = _qkv_heads(_short_conv(ext_ref, cw_ref, tb).reshape(rows, CONV_CH))

    bav = ba_ref[...].astype(F32).reshape(rows, 128)
    zv = z_ref[...].astype(F32).reshape(rows, DN_VW)

    row = lax.broadcasted_iota(jnp.int32, (rows, rows), 0)
    col = lax.broadcasted_iota(jnp.int32, (rows, rows), 1)
    shift = int(math.log2(chunk))
    same = (row >> shift) == (col >> shift)
    tri = same & (row >= col)
    strict = same & (row > col)
    lane = lax.broadcasted_iota(jnp.int32, (rows, 128), 1)

    if rows == 128:
        sub = bav.T[0:2 * DN_HEADS, :]
        g_t = -jnp.exp(alogt_ref[...]) * _softplus(sub + dtbt_ref[...])
        gcum_t = jnp.dot(g_t, (same & (col >= row)).astype(F32), preferred_element_type=F32,
                         precision=lax.Precision.HIGHEST)
        live = lax.broadcasted_iota(jnp.int32, sub.shape, 0) < DN_HEADS
        packed = jnp.where(live, jax.nn.sigmoid(sub), gcum_t)
        cols = jnp.concatenate([packed, jnp.zeros((rows - 2 * DN_HEADS, rows), F32)], axis=0).T
        beta_full = gcum = cols
        grow = [gcum_t[DN_HEADS + h:DN_HEADS + h + 1, :] for h in heads]
    else:
        beta_full = jax.nn.sigmoid(bav)
        g_full = -jnp.exp(alog_ref[...]) * _softplus(bav + dtb_ref[...])
        gcum = jnp.dot(tri.astype(F32), g_full, preferred_element_type=F32, precision=lax.Precision.HIGHEST)
        ones = jnp.ones((rows, 128), F32)
        grow = [_dot_nt(ones, jnp.where(lane == DN_HEADS + h, gcum, 0.0), precision=lax.Precision.HIGHEST)
                for h in heads]
    gcol = [jnp.sum(jnp.where(lane == DN_HEADS + h, gcum, 0.0), axis=1, keepdims=True) for h in heads]
    bcol = [jnp.sum(jnp.where(lane == h, beta_full, 0.0), axis=1, keepdims=True) for h in heads]
    eg = [jnp.exp(gcol[h]) for h in heads]

    bshift = min(shift, 4)
    uw, qkd, qg = {}, {}, {}
    for hg in (heads[i:i + HEAD_GROUP] for i in range(0, DN_HEADS, HEAD_GROUP)):
        decay = {h: jnp.where(tri, jnp.exp(jnp.where(tri, gcol[h] - grow[h], 0.0)), 0.0) for h in hg}
        kb = {h: kn[h].astype(mm) for h in hg}
        kk = {h: _dot_nt(kb[h], kb[h]) for h in hg}
        qk = {h: _dot_nt(qn[h].astype(mm), kb[h]) for h in hg}
        m = {h: jnp.where(strict, kk[h] * decay[h] * bcol[h], 0.0) for h in hg}
        p = {h: jnp.where((row >> bshift) == (col >> bshift), -m[h], 0.0) for h in hg}
        xp = p
        for _ in range(bshift - 1):
            xb = {h: xp[h].astype(mm) for h in hg}
            xp = {h: _dot(xb[h], xb[h]) for h in hg}
            p = {h: p[h] + xp[h] + _dot(p[h].astype(mm), xp[h].astype(mm)) for h in hg}
        for s in range(bshift, shift):
            lower = ((row >> (s + 1)) == (col >> (s + 1))) & ((row >> s) > (col >> s))
            c_blk = {h: jnp.where(lower, m[h], 0.0) for h in hg}
            pb = {h: p[h].astype(mm) for h in hg}
            a = {h: c_blk[h] + _dot(pb[h], c_blk[h].astype(mm)) for h in hg}
            p = {h: p[h] - a[h] - _dot(a[h].astype(mm), pb[h]) for h in hg}
        for h in hg:
            rhs = jnp.concatenate([vh[h] * bcol[h], kn[h] * (bcol[h] * eg[h])], axis=1)
            uw[h] = rhs + _dot(p[h].astype(mm), rhs.astype(mm))
            qkd[h] = (qk[h] * decay[h]).astype(mm)
            qg[h] = qn[h] * eg[h]

    pairs = [(b, h) for b in range(bb) for h in heads]
    state = {bh: s_ref[bh[0], bh[1]] for bh in pairs}
    dlt = {h: [] for h in heads}
    oq = {h: [] for h in heads}
    for c in range(cpb):
        def rs(b):
            return slice(b * tb + c * chunk, b * tb + (c + 1) * chunk)
        prod = {}
        for b, h in pairs:
            lhs = jnp.concatenate([uw[h][rs(b), DN_DV:], qg[h][rs(b)]], axis=0).astype(mc)
            prod[b, h] = _dot(lhs, state[b, h].astype(mc))
        for b, h in pairs:
            r = rs(b)
            d = uw[h][r, :DN_DV] - prod[b, h][:chunk]
            glast = gcol[h][r.stop - 1:r.stop, :]
            kg = kn[h][r] * jnp.exp(glast - gcol[h][r])
            state[b, h] = state[b, h] * jnp.exp(glast) + _dot_tn(kg.astype(mc), d.astype(mc))
            dlt[h].append((r.start, d))
            oq[h].append((r.start, prod[b, h][chunk:]))
    for b, h in pairs:
        s_ref[b, h] = state[b, h]

    def stack(parts):
        parts = [v for _, v in sorted(parts, key=lambda sv: sv[0])]
        return parts[0] if len(parts) == 1 else jnp.concatenate(parts, axis=0)

    for h in heads:
        hs = slice(h * DN_DV, (h + 1) * DN_DV)
        o = stack(oq[h]) + _dot(qkd[h], stack(dlt[h]).astype(mm))
        zh = zv[:, hs]
        gated = _rms(o, onw_ref[...]) * (zh * jax.nn.sigmoid(zh))
        o_ref[:, :, hs] = gated.reshape(bb, tb, DN_DV).astype(o_ref.dtype)


def _delta(h3, cinit, s0, s0_layer, cw, gate, onw, bb, tb, chunk, stack=None, conv_done=False):
    b, t, _ = h3.shape
    assert b % bb == 0 and t % tb == 0 and tb % chunk == 0
    state_block = (None, bb, DN_HEADS, DN_DK, DN_DV)
    in_specs = [
        pl.BlockSpec((bb, tb, CONV_CH), lambda i, j: (i, j, QKV_OFF // CONV_CH)),
        pl.BlockSpec((bb, tb, DN_VW), lambda i, j: (i, j, Z_OFF // DN_VW)),
        pl.BlockSpec((bb, tb, 128), lambda i, j: (i, j, BA_OFF // 128)),
        pl.BlockSpec((bb, CONV_HALO, CONV_CH), lambda i, j: (i, 0, 0)),
        pl.BlockSpec(state_block, lambda i, j: (s0_layer, i, 0, 0, 0)),
        pl.BlockSpec((CONV_W, CONV_CH), lambda i, j: (0, 0)),
        pl.BlockSpec((1, 128), lambda i, j: (0, 0)),
        pl.BlockSpec((1, 128), lambda i, j: (0, 0)),
        pl.BlockSpec((2 * DN_HEADS, 128), lambda i, j: (0, 0)),
        pl.BlockSpec((2 * DN_HEADS, 128), lambda i, j: (0, 0)),
        pl.BlockSpec((1, DN_DV), lambda i, j: (0, 0)),
    ]
    args = [h3, h3, h3, cinit, s0, cw, *gate, onw]
    layer, depth, prev = (0, 1, None) if stack is None else stack
    aliases = {}
    if prev is not None:
        in_specs.append(pl.BlockSpec(memory_space=pl.ANY))
        aliases = {len(args): 1}
        args.append(prev)
    o, s = pl.pallas_call(
        functools.partial(_delta_kernel, bb=bb, tb=tb, chunk=chunk, conv_done=conv_done),
        grid=(b // bb, t // tb),
        in_specs=in_specs,
        out_specs=[
            pl.BlockSpec((bb, tb, DN_VW), lambda i, j: (i, j, 0)),
            pl.BlockSpec(state_block, lambda i, j: (layer, i, 0, 0, 0)),
        ],
        out_shape=[
            jax.ShapeDtypeStruct((b, t, DN_VW), h3.dtype),
            jax.ShapeDtypeStruct((depth, b, DN_HEADS, DN_DK, DN_DV), F32),
        ],
        scratch_shapes=[pltpu.VMEM((bb, CONV_HALO + tb, CONV_CH), F32)],
        input_output_aliases=aliases,
        compiler_params=_cparams(("parallel", "arbitrary")),
        name="delta",
    )(*args)
    return o, (s[0] if stack is None else s)


def _rope(x, cos, sin):
    width = x.shape[-1]
    reps = width // cos.shape[-1]
    if reps > 1:
        cos = jnp.concatenate([cos] * reps, axis=1)
        sin = jnp.concatenate([sin] * reps, axis=1)
    lane = lax.broadcasted_iota(jnp.int32, x.shape, 1)
    first_half = (lane & (SWA_HD - 1)) < (SWA_HD // 2)
    other = jnp.where(first_half, pltpu.roll(x, width - SWA_HD // 2, 1), pltpu.roll(x, SWA_HD // 2, 1))
    return x * cos + other * sin


def _pad_heads(x, lo):
    xr = pltpu.roll(x, SWA_HD, 1)
    z = jnp.zeros_like(x)
    return [jnp.where(lo, x, z), jnp.where(lo, z, xr), jnp.where(lo, xr, z), jnp.where(lo, z, x)]


def _swa_prompt_kernel(sinks_ref, q_ref, kv_ref, meta_ref, cos_ref, sin_ref, o_ref, klast_ref, kcat_ref, vcat_ref,
                       *, pos0):
    n = pl.program_id(1)
    blk = SWA_WINDOW
    lo = lax.broadcasted_iota(jnp.int32, (blk, 128), 1) < SWA_HD

    def put(half, k, v):
        rows = slice(half * blk, (half + 1) * blk)
        for i, (a, b) in enumerate(zip(_pad_heads(k, lo), _pad_heads(v, lo))):
            kcat_ref[i, rows, :] = a.astype(BF16)
            vcat_ref[i, rows, :] = b.astype(BF16)

    @pl.when(n == 0)
    def _():
        put(0, meta_ref[:, :SWA_KV_WIDTH], meta_ref[:, SWA_KV_WIDTH:])

    @pl.when(n > 0)
    def _():
        kcat_ref[:, 0:blk, :] = kcat_ref[:, blk:2 * blk, :]
        vcat_ref[:, 0:blk, :] = vcat_ref[:, blk:2 * blk, :]

    cos = cos_ref[...]
    sin = sin_ref[...]
    q = _rope(q_ref[0].astype(F32), cos, sin) * (SWA_HD ** -0.5)
    kv = kv_ref[0].astype(F32)
    k = _rope(kv[:, :SWA_KV_WIDTH], cos, sin)
    klast_ref[0] = k
    put(1, k, kv[:, SWA_KV_WIDTH:])

    r = lax.broadcasted_iota(jnp.int32, (blk, 2 * blk), 0)
    j = lax.broadcasted_iota(jnp.int32, (blk, 2 * blk), 1)
    jmin = jnp.maximum(blk - pos0 - n * blk, 0)
    mask = ((j < blk) & (j > r) & (j >= jmin)) | ((j >= blk) & ((j - blk) <= r))
    heads = range(SWA_HEADS)
    qt = [q[:, i * 128:(i + 1) * 128].astype(BF16) for i in range(SWA_HEADS // 2)]
    s = [jnp.where(mask, _dot_nt(qt[h // 2], kcat_ref[2 * (h // SWA_GROUP) + h % 2]), NEG_BIG) for h in heads]
    m = [jnp.maximum(jnp.max(s[h], axis=1, keepdims=True), sinks_ref[h]) for h in heads]
    e = [jnp.exp(s[h] - m[h]).astype(BF16) for h in heads]
    ones = jnp.ones((2 * blk, 128), BF16)
    den = [_dot(e[h], ones) + jnp.exp(sinks_ref[h] - m[h]) for h in heads]
    for i in range(SWA_HEADS // 2):
        he, ho = 2 * i, 2 * i + 1
        g = he // SWA_GROUP
        acc = _dot(e[he], vcat_ref[2 * g]) + _dot(e[ho], vcat_ref[2 * g + 1])
        o_ref[0, :, i * 128:(i + 1) * 128] = (acc / jnp.where(lo, den[he], den[ho])).astype(o_ref.dtype)


def _swa_prompt(h3, meta_kv, cos, sin, sinks, pos0):
    b, t, _ = h3.shape
    blk = SWA_WINDOW
    assert t % blk == 0
    return pl.pallas_call(
        functools.partial(_swa_prompt_kernel, pos0=pos0),
        grid=(b, t // blk),
        in_specs=[
            pl.BlockSpec(memory_space=pltpu.SMEM),
            pl.BlockSpec((1, blk, SWA_WIDTH), lambda i, n: (i, n, QC_OFF // SWA_WIDTH)),
            pl.BlockSpec((1, blk, 256), lambda i, n: (i, n, KVC_OFF // 256)),
            pl.BlockSpec((blk, 256), lambda i, n: (0, 0)),
            pl.BlockSpec((blk, 128), lambda i, n: (n, 0)),
            pl.BlockSpec((blk, 128), lambda i, n: (n, 0)),
        ],
        out_specs=[
            pl.BlockSpec((1, blk, SWA_WIDTH), lambda i, n: (i, n, 0)),
            pl.BlockSpec((1, blk, SWA_KV_WIDTH), lambda i, n: (i, 0, 0)),
        ],
        out_shape=[
            jax.ShapeDtypeStruct((b, t, SWA_WIDTH), h3.dtype),
            jax.ShapeDtypeStruct((b, blk, SWA_KV_WIDTH), F32),
        ],
        scratch_shapes=[pltpu.VMEM((4, 2 * blk, 128), BF16), pltpu.VMEM((4, 2 * blk, 128), BF16)],
        compiler_params=_cparams(("parallel", "arbitrary")),
        name="swa_prompt",
    )(sinks, h3, h3, meta_kv, cos, sin)


def _swa_step_kernel(sink_ref, q_ref, kv_ref, *rest, bb, t_new, window):
    if window:
        kc_ref, vc_ref, cos_ref, sin_ref = rest[:4]
        o_ref, kout_ref, vout_ref = rest[-3:]
    else:
        cos_ref, sin_ref, o_ref, knew_ref = rest
    cos = cos_ref[...]
    sin = sin_ref[...]
    rows = SWA_HEADS * t_new
    mn = BF16 if t_new >= 16 else F32
    lo = lax.broadcasted_iota(jnp.int32, (t_new, 128), 1) < SWA_HD
    tq = lax.broadcasted_iota(jnp.int32, (rows, t_new), 0) & (t_new - 1)
    mask_n = lax.broadcasted_iota(jnp.int32, (rows, t_new), 1) <= tq
    if window:
        diff = ((lax.broadcasted_iota(jnp.int32, (rows, window), 0) & (t_new - 1)) + window
                - lax.broadcasted_iota(jnp.int32, (rows, window), 1))
        mask_c = (diff >= 0) & (diff < SWA_WINDOW)
    sink = sink_ref[...]
    seqs = range(bb)

    qall, kk, vv = [], [], []
    for s in seqs:
        q = _rope(q_ref[s].astype(F32), cos, sin) * (SWA_HD ** -0.5)
        kv = kv_ref[s].astype(F32)
        k = _rope(kv[:, :SWA_KV_WIDTH], cos, sin)
        if not window:
            knew_ref[s] = k
        pieces = []
        for h in range(SWA_HEADS):
            g = h // SWA_GROUP
            tile = q[:, (h // 2) * 128:(h // 2 + 1) * 128]
            if h % 2 != g:
                tile = pltpu.roll(tile, SWA_HD, 1)
            pieces.append(jnp.where(lo, tile, 0.0) if g == 0 else jnp.where(lo, 0.0, tile))
        qall.append(jnp.concatenate(pieces, axis=0))
        kk.append(k)
        vv.append(kv[:, SWA_KV_WIDTH:])

    s_n = [jnp.where(mask_n, _dot_nt(qall[s].astype(mn), kk[s].astype(mn)), NEG_BIG) for s in seqs]
    m = [jnp.maximum(jnp.max(s_n[s], axis=1, keepdims=True), sink) for s in seqs]
    if window:
        s_c = [jnp.where(mask_c, _dot(qall[s].astype(BF16), kc_ref[s].astype(BF16)), NEG_BIG) for s in seqs]
        m = [jnp.maximum(m[s], jnp.max(s_c[s], axis=1, keepdims=True)) for s in seqs]
    e_n = [jnp.exp(s_n[s] - m[s]) for s in seqs]
    den = [jnp.sum(e_n[s], axis=1, keepdims=True) + jnp.exp(sink - m[s]) for s in seqs]
    acc = [_dot(e_n[s].astype(mn), vv[s].astype(mn)) for s in seqs]
    if window:
        e_c = [jnp.exp(s_c[s] - m[s]) for s in seqs]
        den = [den[s] + jnp.sum(e_c[s], axis=1, keepdims=True) for s in seqs]
        acc = [acc[s] + _dot_nt(e_c[s].astype(BF16), vc_ref[s].astype(BF16)) for s in seqs]

        def slide(old_t, new):
            kept = pltpu.roll(old_t, window - t_new, 1)
            fresh = jnp.concatenate([jnp.zeros((window - t_new, SWA_KV_WIDTH), F32), new], axis=0).T
            newest = lax.broadcasted_iota(jnp.int32, kept.shape, 1) >= window - t_new
            return jnp.where(newest, fresh, kept)

        for s in seqs:
            kout_ref[s] = slide(kc_ref[s], kk[s])
            vout_ref[s] = slide(vc_ref[s], vv[s])
    for s in seqs:
        a = acc[s] / den[s]
        for i in range(SWA_HEADS // 2):
            he, ho = 2 * i, 2 * i + 1
            g = he // SWA_GROUP
            a_e = a[he * t_new:(he + 1) * t_new]
            a_o = a[ho * t_new:(ho + 1) * t_new]
            if g == 1:
                a_e = pltpu.roll(a_e, SWA_HD, 1)
            else:
                a_o = pltpu.roll(a_o, SWA_HD, 1)
            o_ref[s, :, i * 128:(i + 1) * 128] = jnp.where(lo, a_e, a_o).astype(o_ref.dtype)


def _swa_step(h3, caches, cos, sin, sinks, bb):
    b, t, _ = h3.shape
    assert b % bb == 0
    sink_rows = jnp.repeat(sinks, t)[:, None]
    in_specs = [
        pl.BlockSpec((SWA_HEADS * t, 1), lambda i: (0, 0)),
        pl.BlockSpec((bb, t, SWA_WIDTH), lambda i: (i, 0, QC_OFF // SWA_WIDTH)),
        pl.BlockSpec((bb, t, 256), lambda i: (i, 0, KVC_OFF // 256)),
    ]
    args = [sink_rows, h3, h3]
    out_specs = [pl.BlockSpec((bb, t, SWA_WIDTH), lambda i: (i, 0, 0))]
    out_shape = [jax.ShapeDtypeStruct((b, t, SWA_WIDTH), h3.dtype)]
    window = 0
    aliases = {}
    if caches is None:
        out_specs.append(pl.BlockSpec((bb, t, SWA_KV_WIDTH), lambda i: (i, 0, 0)))
        out_shape.append(jax.ShapeDtypeStruct((b, t, SWA_KV_WIDTH), F32))
    else:
        k_t, v_t, layer, depth, prev_k, prev_v = caches
        window = k_t.shape[-1]
        assert window == SWA_KV_WIDTH and t <= window
        cache_block = pl.BlockSpec((None, bb, SWA_KV_WIDTH, window), lambda i: (layer, i, 0, 0))
        in_specs += [cache_block] * 2
        args += [k_t, v_t]
        out_specs += [cache_block] * 2
        out_shape += [jax.ShapeDtypeStruct((depth, b, SWA_KV_WIDTH, window), F32)] * 2
    in_specs += [pl.BlockSpec((t, 128), lambda i: (0, 0))] * 2
    args += [cos, sin]
    if caches is not None and prev_k is not None:
        in_specs += [pl.BlockSpec(memory_space=pl.ANY)] * 2
        aliases = {len(args): 1, len(args) + 1: 2}
        args += [prev_k, prev_v]
    return pl.pallas_call(
        functools.partial(_swa_step_kernel, bb=bb, t_new=t, window=window),
        grid=(b // bb,),
        in_specs=in_specs,
        out_specs=out_specs,
        out_shape=out_shape,
        input_output_aliases=aliases,
        compiler_params=_cparams(("parallel",)),
        name="swa_step",
    )(*args)


def _merge_kernel(oa_ref, ob_ref, oc_ref, ga_ref, gb_ref, gc_ref, x_ref, pa_ref, pb_ref, pc_ref, wo_ref, o_ref):
    def branch(o_r, g_r, p_r):
        return jax.nn.sigmoid(g_r[...].astype(F32)) * jnp.dot(o_r[...].astype(BF16), p_r[...], preferred_element_type=F32)

    m = branch(oa_ref, ga_ref, pa_ref) + branch(ob_ref, gb_ref, pb_ref) + branch(oc_ref, gc_ref, pc_ref)
    o_ref[...] = x_ref[...] + jnp.dot(m.astype(BF16), wo_ref[...], preferred_element_type=F32)


def _merge(oa, ob, oc, h, x, pa, pb, pc, wo):
    m = x.shape[0]
    tm = min(m, 512)
    assert m % tm == 0
    row =lambda w: pl.BlockSpec((tm, w), lambda i: (i, 0))
    gate = lambda c: pl.BlockSpec((tm, D_MODEL), lambda i: (i, c))
    full = lambda a: pl.BlockSpec(a.shape, lambda i: (0, 0))
    return pl.pallas_call(
        _merge_kernel,
        grid=(m // tm,),
        in_specs=[row(512), row(512), row(512), gate(0), gate(1), gate(2), row(D_MODEL),
                  full(pa), full(pb), full(pc), full(wo)],
        out_specs=row(D_MODEL),
        out_shape=jax.ShapeDtypeStruct((m, D_MODEL), F32),
        compiler_params=_cparams(("parallel",)),
        name="merge",
    )(oa, ob, oc, h, h, h, x, pa, pb, pc, wo)


def _mlp_kernel(x_ref, nw_ref, wu_ref, wd_ref, fnw_ref, o_ref, *, final):
    x = x_ref[...]
    xn = _rms(x, nw_ref[...]).astype(BF16)
    y = x
    for j in range(D_FF // MLP_TF):
        fs = slice(j * MLP_TF, (j + 1) * MLP_TF)
        a = jnp.maximum(jnp.dot(xn, wu_ref[:, fs], preferred_element_type=F32), 0.0)
        y = y + jnp.dot((a * a).astype(BF16), wd_ref[fs, :], preferred_element_type=F32)
    o_ref[...] = _rms(y, fnw_ref[...]) if final else y


def _mlp(x, nw, wu, wd, fnw, final):
    m = x.shape[0]
    tm = min(m, 512)
    assert m % tm == 0
    resident = lambda a: pl.BlockSpec(a.shape, lambda i: (0, 0), pipeline_mode=pl.Buffered(1))
    return pl.pallas_call(
        functools.partial(_mlp_kernel, final=final),
        grid=(m // tm,),
        in_specs=[
            pl.BlockSpec((tm, D_MODEL), lambda i: (i, 0)),
            pl.BlockSpec((1, D_MODEL), lambda i: (0, 0)),
            resident(wu),
            resident(wd),
            pl.BlockSpec((1, D_MODEL), lambda i: (0, 0)),
        ],
        out_specs=pl.BlockSpec((tm, D_MODEL), lambda i: (i, 0)),
        out_shape=jax.ShapeDtypeStruct((m, D_MODEL), F32),
        compiler_params=_cparams(("parallel",)),
        name="mlp",
    )(x, nw, wu, wd, fnw)


def _rope_tables(pos0, t):
    half = SWA_HD // 2
    inv = ROPE_THETA ** (-jnp.arange(half, dtype=F32) / half)
    ang = (pos0 + jnp.arange(t)).astype(F32)[:, None] * inv[None, :]
    cos = jnp.cos(ang)
    sin = jnp.sin(ang)
    return jnp.tile(cos, (1, 4)), jnp.tile(jnp.concatenate([-sin, sin], axis=1), (1, 2))


def _prep_layer(l, norm1_w, w_in, pool_w, pool_scale, dn_conv_w, dn_a_log, dn_dt_bias, dn_onorm_w, swa_sinks,
                proj_a, proj_b, proj_c, w_out, norm2_w, w_up, w_down):
    w = w_in[l]
    segments = ((3336, 6408), (512, 2048), (0, 512), (2048, 2560), (2568, 3080), (3080, 3336), (2560, 2568))
    w_perm = jnp.concatenate(
        [w[:, a:b].astype(BF16) for a, b in segments]
        + [jnp.zeros((D_MODEL, H_WIDTH - BA_OFF - 2 * DN_HEADS), BF16)], axis=1)
    lane_pad = lambda v: jnp.zeros((1, 128), F32).at[0, DN_HEADS:2 * DN_HEADS].set(v.astype(F32))
    row_pad = lambda v: jnp.zeros((2 * DN_HEADS, 128), F32).at[DN_HEADS:].set(
        jnp.broadcast_to(v.astype(F32)[:, None], (DN_HEADS, 128)))
    return dict(
        norm1=norm1_w[l][None].astype(F32), w_in=w_perm,
        pool_w=pool_w[l].astype(BF16), pool_scale=pool_scale[l][None].astype(F32),
        conv_w=dn_conv_w[l].astype(F32),
        gate=(lane_pad(dn_a_log[l]), lane_pad(dn_dt_bias[l]), row_pad(dn_a_log[l]), row_pad(dn_dt_bias[l])),
        onw=dn_onorm_w[l][None].astype(F32), sinks=swa_sinks[l].astype(F32),
        pa=proj_a[l].astype(BF16), pb=proj_b[l].astype(BF16), pc=proj_c[l].astype(BF16),
        wo=w_out[l].astype(BF16), norm2=norm2_w[l][None].astype(F32),
        wu=w_up[l].astype(BF16), wd=w_down[l].astype(BF16))


def _group_step(x3, p, fnw, final, pos0, pool_init, conv_init, s0, kind, swa_extra, pool_tiles, delta_tiles,
                swa_bb=1, s_stack=None, act=F32, fuse_conv=False):
    b, t, _ = x3.shape
    x2 = x3.reshape(b * t, D_MODEL)
    h, conv_tails = _inproj(x2, p['norm1'], p['w_in'], act,
                            conv=(conv_init, p['conv_w'], t) if fuse_conv else None)
    h3 = h.reshape(b, t, H_WIDTH)
    o_a = _pool(h3, pool_init, p['pool_w'], p['pool_scale'], pos0, *pool_tiles)
    o_b, s_new = _delta(h3, conv_init, s0[0], s0[1], p['conv_w'], p['gate'], p['onw'], *delta_tiles,
                        stack=s_stack, conv_done=fuse_conv)
    cos, sin = _rope_tables(pos0, t)
    if kind == 'prompt':
        o_c, k_rot = _swa_prompt(h3, swa_extra, cos, sin, p['sinks'], pos0)
    else:
        o_c, *k_rot = _swa_step(h3, swa_extra, cos, sin, p['sinks'], swa_bb)
    m = b * t
    h1 = _merge(o_a.reshape(m, 512), o_b.reshape(m, 512), o_c.reshape(m, 512), h, x2,
                p['pa'], p['pb'], p['pc'], p['wo'])
    out = _mlp(h1, p['norm2'], p['wu'], p['wd'], fnw, final)
    return out.reshape(b, t, D_MODEL), h3, s_new, k_rot, conv_tails


def kernel(x_prompt, x_sample, state_pool, state_conv, state_delta, cache_swa_k, cache_swa_v, meta_tokens, norm1_w, w_in, pool_w, pool_scale, dn_conv_w, dn_a_log, dn_dt_bias, dn_onorm_w, swa_sinks, proj_a, proj_b, proj_c, w_out, norm2_w, w_up, w_down, final_norm_w):
    depth = w_in.shape[0]
    bp, tp, _ = x_prompt.shape
    bs, ts, _ = x_sample.shape
    past = cache_swa_k.shape[2]
    fnw = final_norm_w[None].astype(F32)

    xm = meta_tokens[None].astype(F32)
    xp = x_prompt
    xs = x_sample
    outs = {k: [] for k in ('pool_p', 'conv_p', 'k_p', 'v_p', 'pool_s', 'conv_s')}
    delta_p = delta_s = None
    k_s = v_s = None
    to_t = lambda c: c.transpose(0, 1, 3, 4, 2).reshape(depth, bs, SWA_KV_WIDTH, past)
    cache_k_t, cache_v_t = to_t(cache_swa_k), to_t(cache_swa_v)
    for l in range(depth):
        p = _prep_layer(l, norm1_w, w_in, pool_w, pool_scale, dn_conv_w, dn_a_log, dn_dt_bias, dn_onorm_w,
                        swa_sinks, proj_a, proj_b, proj_c, w_out, norm2_w, w_up, w_down)
        final = l == depth - 1

        xm, hm, s_m, k_m, _ = _group_step(
            xm, p, fnw, final, 0,
            jnp.zeros((1, POOL_HALO, POOL_WIDTH), F32), jnp.zeros((1, CONV_HALO, CONV_CH), F32),
            (jnp.zeros((1, 1, DN_HEADS, DN_DK, DN_DV), F32), 0), 'step', None,
            (1, N_META), (1, N_META, N_META))

        meta_kv = jnp.zeros((SWA_WINDOW, 256), F32)
        meta_kv = meta_kv.at[SWA_WINDOW - N_META:, :SWA_KV_WIDTH].set(k_m[0][0])
        meta_kv = meta_kv.at[SWA_WINDOW - N_META:, SWA_KV_WIDTH:].set(hm[0, :, KVC_OFF + SWA_KV_WIDTH:KVC_OFF + 256])
        xp, hp, s_p, k_p, tails_p = _group_step(
            xp, p, fnw, final, N_META,
            jnp.broadcast_to(hm[:, :, U_OFF:U_OFF + POOL_WIDTH], (bp, POOL_HALO, POOL_WIDTH)),
            jnp.broadcast_to(hm[:, N_META - CONV_HALO:, QKV_OFF:QKV_OFF + CONV_CH], (bp, CONV_HALO, CONV_CH)),
            (jnp.broadcast_to(s_m[None], (1, bp, DN_HEADS, DN_DK, DN_DV)), 0), 'prompt', meta_kv,
            (1, 512), (2, DN_CHUNK, DN_CHUNK), s_stack=(l, depth, delta_p), act=BF16, fuse_conv=True)
        delta_p = s_p
        outs['pool_p'].append(hp[:, tp - POOL_BUF:, U_OFF:U_OFF + POOL_WIDTH].astype(F32))
        if tails_p is None:
            outs['conv_p'].append(hp[:, tp - (CONV_W - 1):, QKV_OFF:QKV_OFF + CONV_CH].astype(F32))
        else:
            outs['conv_p'].append(tails_p.reshape(bp, -1, CONV_HALO, CONV_CH)[:, -1, CONV_HALO - (CONV_W - 1):])
        outs['k_p'].append(k_p.reshape(bp, SWA_WINDOW, SWA_KV_HEADS, SWA_HD))
        outs['v_p'].append(hp[:, tp - SWA_WINDOW:, KVC_OFF + SWA_KV_WIDTH:KVC_OFF + 256].astype(F32)
                           .reshape(bp, SWA_WINDOW, SWA_KV_HEADS, SWA_HD))

        xs, hs, s_s, (k_s, v_s), _ = _group_step(
            xs, p, fnw, final, PAST_LEN,
            jnp.concatenate([jnp.zeros((bs, POOL_HALO - POOL_BUF, POOL_WIDTH), F32), state_pool[l]], axis=1),
            jnp.concatenate([jnp.zeros((bs, CONV_HALO - (CONV_W - 1), CONV_CH), F32), state_conv[l]], axis=1),
            (state_delta, l), 'step', (cache_k_t, cache_v_t, l, depth, k_s, v_s),
            (bs, ts), (16, ts, ts), 16, s_stack=(l, depth, delta_s))
        delta_s = s_s
        outs['pool_s'].append(jnp.concatenate([state_pool[l], hs[:, :, U_OFF:U_OFF + POOL_WIDTH]], axis=1)[:, -POOL_BUF:])
        outs['conv_s'].append(jnp.concatenate([state_conv[l], hs[:, :, QKV_OFF:QKV_OFF + CONV_CH]], axis=1)[:, -(CONV_W - 1):])

    st = {k: jnp.stack(v) for k, v in outs.items()}
    from_t = lambda c: c.reshape(depth, bs, SWA_KV_HEADS, SWA_HD, past).transpose(0, 1, 4, 2, 3)
    return (xp, xs, st['pool_p'], st['conv_p'], delta_p, st['k_p'], st['v_p'],
            st['pool_s'], st['conv_s'], delta_s, from_t(k_s), from_t(v_s))
```

```python
import functools
import math

import jax
import jax.numpy as jnp
from jax import lax
from jax.experimental import pallas as pl
from jax.experimental.pallas import tpu as pltpu

F32 = jnp.float32
BF16 = jnp.bfloat16

D_MODEL = 1024
N_META = 16
EPS = 1e-6
POOL_GROUPS = 4
POOL_GROUP_DIM = 128
POOL_WIDTH = 512
POOL_WINDOWS = (2, 4, 8, 16)
POOL_BUF = 15
POOL_HALO = 16
DN_HEADS = 4
DN_DK = 128
DN_DV = 128
DN_QK = 512
DN_VW = 512
CONV_W = 4
CONV_CH = 1536
CONV_HALO = 8
DN_CHUNK = 64
HEAD_GROUP = 4
SWA_HEADS = 8
SWA_KV_HEADS = 2
SWA_GROUP = 4
SWA_HD = 64
SWA_WIDTH = 512
SWA_KV_WIDTH = 128
SWA_WINDOW = 128
ROPE_THETA = 10000.0
D_FF = 4096
MLP_TF = 1024
PAST_LEN = 16384

G_OFF = 0
QKV_OFF = 3072
U_OFF = 4608
Z_OFF = 5120
QC_OFF = 5632
KVC_OFF = 6144
BA_OFF = 6400
H_WIDTH = 6528
IN_CHUNKS = ((0, 1536), (1536, 3072), (3072, 4608), (4608, 6144), (6144, 6528))

VMEM_LIMIT = 56 * 1024 * 1024
NEG_BIG = -1e30


def _cparams(sem):
    return pltpu.CompilerParams(dimension_semantics=sem, vmem_limit_bytes=VMEM_LIMIT)


def _rms(x, w):
    return x * lax.rsqrt(jnp.mean(x * x, axis=-1, keepdims=True) + EPS) * w


def _l2n(v):
    return v * lax.rsqrt(jnp.sum(v * v, axis=-1, keepdims=True) + EPS)


def _short_conv(ext_ref, cw_ref, rows):
    base = CONV_HALO - (CONV_W - 1)
    y = ext_ref[..., base:base + rows, :] * cw_ref[0:1, :]
    for j in range(1, CONV_W):
        y = y + ext_ref[..., base + j:base + j + rows, :] * cw_ref[j:j + 1, :]
    return y * jax.nn.sigmoid(y)


def _qkv_heads(y):
    heads = range(DN_HEADS)
    qn = [_l2n(y[:, h * DN_DK:(h + 1) * DN_DK]) * (DN_DK ** -0.5) for h in heads]
    kn = [_l2n(y[:, DN_QK + h * DN_DK:DN_QK + (h + 1) * DN_DK]) for h in heads]
    vh = [y[:, 2 * DN_QK + h * DN_DV:2 * DN_QK + (h + 1) * DN_DV] for h in heads]
    return qn, kn, vh


def _inproj_kernel(x_ref, nw_ref, w_ref, *rest, tiles_per_seq):
    if tiles_per_seq:
        cinit_ref, cw_ref, o_ref, tail_ref, ext_ref = rest
    else:
        (o_ref,) = rest
    tm = x_ref.shape[0]
    if tiles_per_seq:
        first = pl.program_id(0) % tiles_per_seq == 0

        @pl.when(first)
        def _():
            ext_ref[0:CONV_HALO, :] = cinit_ref[0]

        @pl.when(jnp.logical_not(first))
        def _():
            ext_ref[0:CONV_HALO, :] = ext_ref[tm:tm + CONV_HALO, :]

    xn = _rms(x_ref[...], nw_ref[...]).astype(BF16)
    for a, b in sorted(IN_CHUNKS, key=lambda ab: ab[0] != QKV_OFF):
        acc = jnp.dot(xn, w_ref[:, a:b], preferred_element_type=F32)
        if tiles_per_seq and a == QKV_OFF:
            ext_ref[CONV_HALO:CONV_HALO + tm, :] = acc
            tail_ref[0] = acc[tm - CONV_HALO:, :]
            qn, kn, vh = _qkv_heads(_short_conv(ext_ref, cw_ref, tm))
            acc = jnp.concatenate(qn + kn + vh, axis=1)
        o_ref[:, a:b] = acc.astype(o_ref.dtype)


def _inproj(x, nw, w, act, conv=None):
    m = x.shape[0]
    tm = min(m, 512 if act == BF16 else 256)
    assert m % tm == 0
    in_specs = [
        pl.BlockSpec((tm, D_MODEL), lambda i: (i, 0)),
        pl.BlockSpec((1, D_MODEL), lambda i: (0, 0)),
        pl.BlockSpec((D_MODEL, H_WIDTH), lambda i: (0, 0), pipeline_mode=pl.Buffered(1)),
    ]
    out_specs = [pl.BlockSpec((tm, H_WIDTH), lambda i: (i, 0))]
    out_shape = [jax.ShapeDtypeStruct((m, H_WIDTH), act)]
    args = [x, nw, w]
    scratch = []
    tiles_per_seq = 0
    if conv is not None:
        cinit, cw, seq_rows = conv
        assert seq_rows % tm == 0
        tiles_per_seq = seq_rows // tm
        in_specs += [pl.BlockSpec((1, CONV_HALO, CONV_CH), lambda i: (i // tiles_per_seq, 0, 0)),
                     pl.BlockSpec((CONV_W, CONV_CH), lambda i: (0, 0))]
        args += [cinit, cw]
        out_specs.append(pl.BlockSpec((1, CONV_HALO, CONV_CH), lambda i: (i, 0, 0)))
        out_shape.append(jax.ShapeDtypeStruct((m // tm, CONV_HALO, CONV_CH), F32))
        scratch = [pltpu.VMEM((CONV_HALO + tm, CONV_CH), F32)]
    res = pl.pallas_call(
        functools.partial(_inproj_kernel, tiles_per_seq=tiles_per_seq),
        grid=(m // tm,),
        in_specs=in_specs,
        out_specs=out_specs,
        out_shape=out_shape,
        scratch_shapes=scratch,
        compiler_params=_cparams(("arbitrary",) if conv is not None else ("parallel",)),
        name="inproj",
    )(*args)
    return (res[0], res[1]) if conv is not None else (res[0], None)


def _pool_kernel(u_ref, init_ref, pw_ref, ps_ref, o_ref, ext_ref, *, pos0, tt):
    t = pl.program_id(1)
    bb = u_ref.shape[0]

    @pl.when(t == 0)
    def _():
        ext_ref[:, 0:POOL_HALO, :] = init_ref[...]

    @pl.when(t > 0)
    def _():
        ext_ref[:, 0:POOL_HALO, :] = ext_ref[:, tt:tt + POOL_HALO, :]

    ext_ref[:, POOL_HALO:POOL_HALO + tt, :] = u_ref[...].astype(F32)

    pos = pos0 + t * tt + lax.broadcasted_iota(jnp.int32, (1, tt, 1), 1)
    for g, w in enumerate(POOL_WINDOWS):
        cs = slice(g * POOL_GROUP_DIM, (g + 1) * POOL_GROUP_DIM)
        s = ext_ref[:, :, cs]
        x = s[:, POOL_HALO:POOL_HALO + tt]
        k = 1
        while k < w:
            s = s + pltpu.roll(s, k, 1)
            k *= 2
        s = s[:, POOL_HALO:POOL_HALO + tt]
        cnt = jnp.minimum(w, pos + 1).astype(F32)
        d = (s / cnt - x).reshape(bb * tt, POOL_GROUP_DIM)
        y = jnp.dot(d.astype(BF16), pw_ref[g], preferred_element_type=F32)
        o_ref[:, :, cs] = (y * ps_ref[:, cs]).reshape(bb, tt, POOL_GROUP_DIM).astype(o_ref.dtype)


def _pool(h3, init, pw, ps, pos0, bb, tt):
    b, t, _ = h3.shape
    return pl.pallas_call(
        functools.partial(_pool_kernel, pos0=pos0, tt=tt),
        grid=(b // bb, t // tt),
        in_specs=[
            pl.BlockSpec((bb, tt, POOL_WIDTH), lambda i, j: (i, j, U_OFF // POOL_WIDTH)),
            pl.BlockSpec((bb, POOL_HALO, POOL_WIDTH), lambda i, j: (i, 0, 0)),
            pl.BlockSpec((POOL_GROUPS, POOL_GROUP_DIM, POOL_GROUP_DIM), lambda i, j: (0, 0, 0)),
            pl.BlockSpec((1, POOL_WIDTH), lambda i, j: (0, 0)),
        ],
        out_specs=pl.BlockSpec((bb, tt, POOL_WIDTH), lambda i, j: (i, j, 0)),
        out_shape=jax.ShapeDtypeStruct((b, t, POOL_WIDTH), h3.dtype),
        scratch_shapes=[pltpu.VMEM((bb, POOL_HALO + tt, POOL_WIDTH), F32)],
        compiler_params=_cparams(("parallel", "arbitrary")),
        name="pool",
    )(h3, init, pw, ps)


def _softplus(x):
    return jnp.maximum(x, 0.0) + jnp.log1p(jnp.exp(-jnp.abs(x)))


def _dot_nt(a, b, **kw):
    return lax.dot_general(a, b, (((1,), (1,)), ((), ())), preferred_element_type=F32, **kw)


def _dot_tn(a, b):
    return lax.dot_general(a, b, (((0,), (0,)), ((), ())), preferred_element_type=F32)


def _dot(a, b):
    return jnp.dot(a, b, preferred_element_type=F32)


def _delta_kernel(qkv_ref, z_ref, ba_ref, cinit_ref, s0_ref, cw_ref, alog_ref, dtb_ref, alogt_ref, dtbt_ref,
                  onw_ref, *rest, bb, tb, chunk, conv_done):
    o_ref, s_ref, ext_ref = rest[-3:]
    t = pl.program_id(1)
    rows = bb * tb
    cpb = tb // chunk
    heads = list(range(DN_HEADS))
    mm = BF16 if rows >= 16 else F32
    mc = BF16 if chunk >= 16 else F32

    @pl.when(t == 0)
    def _():
        s_ref[...] = s0_ref[...]

    if conv_done:
        y = qkv_ref[...].astype(F32).reshape(rows, CONV_CH)
        qn = [y[:, h * DN_DK:(h + 1) * DN_DK] for h in heads]
        kn = [y[:, DN_QK + h * DN_DK:DN_QK + (h + 1) * DN_DK] for h in heads]
        vh = [y[:, 2 * DN_QK + h * DN_DV:2 * DN_QK + (h + 1) * DN_DV] for h in heads]
    else:
        @pl.when(t == 0)
        def _():
            ext_ref[:, 0:CONV_HALO, :] = cinit_ref[...]

        @pl.when(t > 0)
        def _():
            ext_ref[:, 0:CONV_HALO, :] = ext_ref[:, tb:tb + CONV_HALO, :]

        ext_ref[:, CONV_HALO:CONV_HALO + tb, :] = qkv_ref[...].astype(F32)
        qn, kn, vh = _qkv_heads(_short_conv(ext_ref, cw_ref, tb).reshape(rows, CONV_CH))

    bav = ba_ref[...].astype(F32).reshape(rows, 128)
    zv = z_ref[...].astype(F32).reshape(rows, DN_VW)

    row = lax.broadcasted_iota(jnp.int32, (rows, rows), 0)
    col = lax.broadcasted_iota(jnp.int32, (rows, rows), 1)
    shift = int(math.log2(chunk))
    same = (row >> shift) == (col >> shift)
    tri = same & (row >= col)
    strict = same & (row > col)
    lane = lax.broadcasted_iota(jnp.int32, (rows, 128), 1)

    if rows == 128:
        sub = bav.T[0:2 * DN_HEADS, :]
        g_t = -jnp.exp(alogt_ref[...]) * _softplus(sub + dtbt_ref[...])
        gcum_t = jnp.dot(g_t, (same & (col >= row)).astype(F32), preferred_element_type=F32,
                         precision=lax.Precision.HIGHEST)
        live = lax.broadcasted_iota(jnp.int32, sub.shape, 0) < DN_HEADS
        packed = jnp.where(live, jax.nn.sigmoid(sub), gcum_t)
        cols = jnp.concatenate([packed, jnp.zeros((rows - 2 * DN_HEADS, rows), F32)], axis=0).T
        beta_full = gcum = cols
        grow = [gcum_t[DN_HEADS + h:DN_HEADS + h + 1, :] for h in heads]
    else:
        beta_full = jax.nn.sigmoid(bav)
        g_full = -jnp.exp(alog_ref[...]) * _softplus(bav + dtb_ref[...])
        gcum = jnp.dot(tri.astype(F32), g_full, preferred_element_type=F32, precision=lax.Precision.HIGHEST)
        ones = jnp.ones((rows, 128), F32)
        grow = [_dot_nt(ones, jnp.where(lane == DN_HEADS + h, gcum, 0.0), precision=lax.Precision.HIGHEST)
                for h in heads]
    gcol = [jnp.sum(jnp.where(lane == DN_HEADS + h, gcum, 0.0), axis=1, keepdims=True) for h in heads]
    bcol = [jnp.sum(jnp.where(lane == h, beta_full, 0.0), axis=1, keepdims=True) for h in heads]
    eg = [jnp.exp(gcol[h]) for h in heads]

    bshift = min(shift, 4)
    uw, qkd, qg = {}, {}, {}
    for hg in (heads[i:i + HEAD_GROUP] for i in range(0, DN_HEADS, HEAD_GROUP)):
        decay = {h: jnp.where(tri, jnp.exp(jnp.where(tri, gcol[h] - grow[h], 0.0)), 0.0) for h in hg}
        kb = {h: kn[h].astype(mm) for h in hg}
        kk = {h: _dot_nt(kb[h], kb[h]) for h in hg}
        qk = {h: _dot_nt(qn[h].astype(mm), kb[h]) for h in hg}
        m = {h: jnp.where(strict, kk[h] * decay[h] * bcol[h], 0.0) for h in hg}
        p = {h: jnp.where((row >> bshift) == (col >> bshift), -m[h], 0.0) for h in hg}
        xp = p
        for _ in range(bshift - 1):
            xb = {h: xp[h].astype(mm) for h in hg}
            xp = {h: _dot(xb[h], xb[h]) for h in hg}
            p = {h: p[h] + xp[h] + _dot(p[h].astype(mm), xp[h].astype(mm)) for h in hg}
        for s in range(bshift, shift):
            lower = ((row >> (s + 1)) == (col >> (s + 1))) & ((row >> s) > (col >> s))
            c_blk = {h: jnp.where(lower, m[h], 0.0) for h in hg}
            pb = {h: p[h].astype(mm) for h in hg}
            a = {h: c_blk[h] + _dot(pb[h], c_blk[h].astype(mm)) for h in hg}
            p = {h: p[h] - a[h] - _dot(a[h].astype(mm), pb[h]) for h in hg}
        for h in hg:
            rhs = jnp.concatenate([vh[h] * bcol[h], kn[h] * (bcol[h] * eg[h])], axis=1)
            uw[h] = rhs + _dot(p[h].astype(mm), rhs.astype(mm))
            qkd[h] = (qk[h] * decay[h]).astype(mm)
            qg[h] = qn[h] * eg[h]

    pairs = [(b, h) for b in range(bb) for h in heads]
    state = {bh: s_ref[bh[0], bh[1]] for bh in pairs}
    dlt = {h: [] for h in heads}
    oq = {h: [] for h in heads}
    for c in range(cpb):
        def rs(b):
            return slice(b * tb + c * chunk, b * tb + (c + 1) * chunk)
        prod = {}
        for b, h in pairs:
            lhs = jnp.concatenate([uw[h][rs(b), DN_DV:], qg[h][rs(b)]], axis=0).astype(mc)
            prod[b, h] = _dot(lhs, state[b, h].astype(mc))
        for b, h in pairs:
            r = rs(b)
            d = uw[h][r, :DN_DV] - prod[b, h][:chunk]
            glast = gcol[h][r.stop - 1:r.stop, :]
            kg = kn[h][r] * jnp.exp(glast - gcol[h][r])
            state[b, h] = state[b, h] * jnp.exp(glast) + _dot_tn(kg.astype(mc), d.astype(mc))
            dlt[h].append((r.start, d))
            oq[h].append((r.start, prod[b, h][chunk:]))
    for b, h in pairs:
        s_ref[b, h] = state[b, h]

    def stack(parts):
        parts = [v for _, v in sorted(parts, key=lambda sv: sv[0])]
        return parts[0] if len(parts) == 1 else jnp.concatenate(parts, axis=0)

    for h in heads:
        hs = slice(h * DN_DV, (h + 1) * DN_DV)
        o = stack(oq[h]) + _dot(qkd[h], stack(dlt[h]).astype(mm))
        zh = zv[:, hs]
        gated = _rms(o, onw_ref[...]) * (zh * jax.nn.sigmoid(zh))
        o_ref[:, :, hs] = gated.reshape(bb, tb, DN_DV).astype(o_ref.dtype)


def _delta(h3, cinit, s0, s0_layer, cw, gate, onw, bb, tb, chunk, stack=None, conv_done=False):
    b, t, _ = h3.shape
    assert b % bb == 0 and t % tb == 0 and tb % chunk == 0
    state_block = (None, bb, DN_HEADS, DN_DK, DN_DV)
    in_specs = [
        pl.BlockSpec((bb, tb, CONV_CH), lambda i, j: (i, j, QKV_OFF // CONV_CH)),
        pl.BlockSpec((bb, tb, DN_VW), lambda i, j: (i, j, Z_OFF // DN_VW)),
        pl.BlockSpec((bb, tb, 128), lambda i, j: (i, j, BA_OFF // 128)),
        pl.BlockSpec((bb, CONV_HALO, CONV_CH), lambda i, j: (i, 0, 0)),
        pl.BlockSpec(state_block, lambda i, j: (s0_layer, i, 0, 0, 0)),
        pl.BlockSpec((CONV_W, CONV_CH), lambda i, j: (0, 0)),
        pl.BlockSpec((1, 128), lambda i, j: (0, 0)),
        pl.BlockSpec((1, 128), lambda i, j: (0, 0)),
        pl.BlockSpec((2 * DN_HEADS, 128), lambda i, j: (0, 0)),
        pl.BlockSpec((2 * DN_HEADS, 128), lambda i, j: (0, 0)),
        pl.BlockSpec((1, DN_DV), lambda i, j: (0, 0)),
    ]
    args = [h3, h3, h3, cinit, s0, cw, *gate, onw]
    layer, depth, prev = (0, 1, None) if stack is None else stack
    aliases = {}
    if prev is not None:
        in_specs.append(pl.BlockSpec(memory_space=pl.ANY))
        aliases = {len(args): 1}
        args.append(prev)
    o, s = pl.pallas_call(
        functools.partial(_delta_kernel, bb=bb, tb=tb, chunk=chunk, conv_done=conv_done),
        grid=(b // bb, t // tb),
        in_specs=in_specs,
        out_specs=[
            pl.BlockSpec((bb, tb, DN_VW), lambda i, j: (i, j, 0)),
            pl.BlockSpec(state_block, lambda i, j: (layer, i, 0, 0, 0)),
        ],
        out_shape=[
            jax.ShapeDtypeStruct((b, t, DN_VW), h3.dtype),
            jax.ShapeDtypeStruct((depth, b, DN_HEADS, DN_DK, DN_DV), F32),
        ],
        scratch_shapes=[pltpu.VMEM((bb, CONV_HALO + tb, CONV_CH), F32)],
        input_output_aliases=aliases,
        compiler_params=_cparams(("parallel", "arbitrary")),
        name="delta",
    )(*args)
    return o, (s[0] if stack is None else s)


def _rope(x, cos, sin):
    width = x.shape[-1]
    reps = width // cos.shape[-1]
    if reps > 1:
        cos = jnp.concatenate([cos] * reps, axis=1)
        sin = jnp.concatenate([sin] * reps, axis=1)
    lane = lax.broadcasted_iota(jnp.int32, x.shape, 1)
    first_half = (lane & (SWA_HD - 1)) < (SWA_HD // 2)
    other = jnp.where(first_half, pltpu.roll(x, width - SWA_HD // 2, 1), pltpu.roll(x, SWA_HD // 2, 1))
    return x * cos + other * sin


def _pad_heads(x, lo, fill):
    xr = pltpu.roll(x, SWA_HD, 1)
    z = jnp.full_like(x, fill)
    return [jnp.where(lo, x, z), jnp.where(lo, z, xr), jnp.where(lo, xr, z), jnp.where(lo, z, x)]


def _swa_prompt_kernel(sinks_ref, q_ref, kv_ref, meta_ref, cos_ref, sin_ref, o_ref, klast_ref, kcat_ref, vcat_ref,
                       *, pos0):
    n = pl.program_id(1)
    blk = SWA_WINDOW
    lo = lax.broadcasted_iota(jnp.int32, (blk, 128), 1) < SWA_HD

    def put(half, k, v):
        rows = slice(half * blk, (half + 1) * blk)
        for i, (a, b) in enumerate(zip(_pad_heads(k, lo, 0.0), _pad_heads(v, lo, 1.0))):
            kcat_ref[i, rows, :] = a.astype(BF16)
            vcat_ref[i, rows, :] = b.astype(BF16)

    @pl.when(n == 0)
    def _():
        put(0, meta_ref[:, :SWA_KV_WIDTH], meta_ref[:, SWA_KV_WIDTH:])

    @pl.when(n > 0)
    def _():
        kcat_ref[:, 0:blk, :] = kcat_ref[:, blk:2 * blk, :]
        vcat_ref[:, 0:blk, :] = vcat_ref[:, blk:2 * blk, :]

    cos = cos_ref[...]
    sin = sin_ref[...]
    q = _rope(q_ref[0].astype(F32), cos, sin) * (SWA_HD ** -0.5)
    kv = kv_ref[0].astype(F32)
    k = _rope(kv[:, :SWA_KV_WIDTH], cos, sin)
    klast_ref[0] = k
    put(1, k, kv[:, SWA_KV_WIDTH:])

    r = lax.broadcasted_iota(jnp.int32, (blk, 2 * blk), 0)
    j = lax.broadcasted_iota(jnp.int32, (blk, 2 * blk), 1)
    jmin = jnp.maximum(blk - pos0 - n * blk, 0)
    mask = ((j < blk) & (j > r) & (j >= jmin)) | ((j >= blk) & ((j - blk) <= r))
    heads = range(SWA_HEADS)
    qt = [q[:, i * 128:(i + 1) * 128].astype(BF16) for i in range(SWA_HEADS // 2)]
    s = [jnp.where(mask, _dot_nt(qt[h // 2], kcat_ref[2 * (h // SWA_GROUP) + h % 2]), NEG_BIG) for h in heads]
    m = [jnp.maximum(jnp.max(s[h], axis=1, keepdims=True), sinks_ref[h]) for h in heads]
    e = [jnp.exp(s[h] - m[h]).astype(BF16) for h in heads]
    sink_w = [jnp.exp(sinks_ref[h] - m[h]) for h in heads]
    for i in range(SWA_HEADS // 2):
        he, ho = 2 * i, 2 * i + 1
        g = he // SWA_GROUP
        a_e = _dot(e[he], vcat_ref[2 * g])
        a_o = _dot(e[ho], vcat_ref[2 * g + 1])
        den = (jnp.where(lo, pltpu.roll(a_e, SWA_HD, 1), pltpu.roll(a_o, SWA_HD, 1))
               + jnp.where(lo, sink_w[he], sink_w[ho]))
        o_ref[0, :, i * 128:(i + 1) * 128] = (jnp.where(lo, a_e, a_o) / den).astype(o_ref.dtype)


def _swa_prompt(h3, meta_kv, cos, sin, sinks, pos0):
    b, t, _ = h3.shape
    blk = SWA_WINDOW
    assert t % blk == 0
    return pl.pallas_call(
        functools.partial(_swa_prompt_kernel, pos0=pos0),
        grid=(b, t // blk),
        in_specs=[
            pl.BlockSpec(memory_space=pltpu.SMEM),
            pl.BlockSpec((1, blk, SWA_WIDTH), lambda i, n: (i, n, QC_OFF // SWA_WIDTH)),
            pl.BlockSpec((1, blk, 256), lambda i, n: (i, n, KVC_OFF // 256)),
            pl.BlockSpec((blk, 256), lambda i, n: (0, 0)),
            pl.BlockSpec((blk, 128), lambda i, n: (n, 0)),
            pl.BlockSpec((blk, 128), lambda i, n: (n, 0)),
        ],
        out_specs=[
            pl.BlockSpec((1, blk, SWA_WIDTH), lambda i, n: (i, n, 0)),
            pl.BlockSpec((1, blk, SWA_KV_WIDTH), lambda i, n: (i, 0, 0)),
        ],
        out_shape=[
            jax.ShapeDtypeStruct((b, t, SWA_WIDTH), h3.dtype),
            jax.ShapeDtypeStruct((b, blk, SWA_KV_WIDTH), F32),
        ],
        scratch_shapes=[pltpu.VMEM((4, 2 * blk, 128), BF16), pltpu.VMEM((4, 2 * blk, 128), BF16)],
        compiler_params=_cparams(("parallel", "arbitrary")),
        name="swa_prompt",
    )(sinks, h3, h3, meta_kv, cos, sin)


def _swa_step_kernel(sink_ref, q_ref, kv_ref, *rest, bb, t_new, window):
    if window:
        kc_ref, vc_ref, cos_ref, sin_ref = rest[:4]
        o_ref, kout_ref, vout_ref = rest[-3:]
    else:
        cos_ref, sin_ref, o_ref, knew_ref = rest
    cos = cos_ref[...]
    sin = sin_ref[...]
    rows = SWA_HEADS * t_new
    mn = BF16 if t_new >= 16 else F32
    lo = lax.broadcasted_iota(jnp.int32, (t_new, 128), 1) < SWA_HD
    tq = lax.broadcasted_iota(jnp.int32, (rows, t_new), 0) & (t_new - 1)
    mask_n = lax.broadcasted_iota(jnp.int32, (rows, t_new), 1) <= tq
    if window:
        diff = ((lax.broadcasted_iota(jnp.int32, (rows, window), 0) & (t_new - 1)) + window
                - lax.broadcasted_iota(jnp.int32, (rows, window), 1))
        mask_c = (diff >= 0) & (diff < SWA_WINDOW)
    sink = sink_ref[...]
    seqs = range(bb)

    qall, kk, vv = [], [], []
    for s in seqs:
        q = _rope(q_ref[s].astype(F32), cos, sin) * (SWA_HD ** -0.5)
        kv = kv_ref[s].astype(F32)
        k = _rope(kv[:, :SWA_KV_WIDTH], cos, sin)
        if not window:
            knew_ref[s] = k
        pieces = []
        for h in range(SWA_HEADS):
            g = h // SWA_GROUP
            tile = q[:, (h // 2) * 128:(h // 2 + 1) * 128]
            if h % 2 != g:
                tile = pltpu.roll(tile, SWA_HD, 1)
            pieces.append(jnp.where(lo, tile, 0.0) if g == 0 else jnp.where(lo, 0.0, tile))
        qall.append(jnp.concatenate(pieces, axis=0))
        kk.append(k)
        vv.append(kv[:, SWA_KV_WIDTH:])

    s_n = [jnp.where(mask_n, _dot_nt(qall[s].astype(mn), kk[s].astype(mn)), NEG_BIG) for s in seqs]
    m = [jnp.maximum(jnp.max(s_n[s], axis=1, keepdims=True), sink) for s in seqs]
    if window:
        s_c = [jnp.where(mask_c, _dot(qall[s].astype(BF16), kc_ref[s].astype(BF16)), NEG_BIG) for s in seqs]
        m = [jnp.maximum(m[s], jnp.max(s_c[s], axis=1, keepdims=True)) for s in seqs]
    e_n = [jnp.exp(s_n[s] - m[s]) for s in seqs]
    den = [jnp.sum(e_n[s], axis=1, keepdims=True) + jnp.exp(sink - m[s]) for s in seqs]
    acc = [_dot(e_n[s].astype(mn), vv[s].astype(mn)) for s in seqs]
    if window:
        e_c = [jnp.exp(s_c[s] - m[s]) for s in seqs]
        den = [den[s] + jnp.sum(e_c[s], axis=1, keepdims=True) for s in seqs]
        acc = [acc[s] + _dot_nt(e_c[s].astype(BF16), vc_ref[s].astype(BF16)) for s in seqs]

        def slide(old_t, new):
            kept = pltpu.roll(old_t, window - t_new, 1)
            fresh = jnp.concatenate([jnp.zeros((window - t_new, SWA_KV_WIDTH), F32), new], axis=0).T
            newest = lax.broadcasted_iota(jnp.int32, kept.shape, 1) >= window - t_new
            return jnp.where(newest, fresh, kept)

        for s in seqs:
            kout_ref[s] = slide(kc_ref[s], kk[s])
            vout_ref[s] = slide(vc_ref[s], vv[s])
    for s in seqs:
        a = acc[s] / den[s]
        for i in range(SWA_HEADS // 2):
            he, ho = 2 * i, 2 * i + 1
            g = he // SWA_GROUP
            a_e = a[he * t_new:(he + 1) * t_new]
            a_o = a[ho * t_new:(ho + 1) * t_new]
            if g == 1:
                a_e = pltpu.roll(a_e, SWA_HD, 1)
            else:
                a_o = pltpu.roll(a_o, SWA_HD, 1)
            o_ref[s, :, i * 128:(i + 1) * 128] = jnp.where(lo, a_e, a_o).astype(o_ref.dtype)


def _swa_step(h3, caches, cos, sin, sinks, bb):
    b, t, _ = h3.shape
    assert b % bb == 0
    sink_rows = jnp.repeat(sinks, t)[:, None]
    in_specs = [
        pl.BlockSpec((SWA_HEADS * t, 1), lambda i: (0, 0)),
        pl.BlockSpec((bb, t, SWA_WIDTH), lambda i: (i, 0, QC_OFF // SWA_WIDTH)),
        pl.BlockSpec((bb, t, 256), lambda i: (i, 0, KVC_OFF // 256)),
    ]
    args = [sink_rows, h3, h3]
    out_specs = [pl.BlockSpec((bb, t, SWA_WIDTH), lambda i: (i, 0, 0))]
    out_shape = [jax.ShapeDtypeStruct((b, t, SWA_WIDTH), h3.dtype)]
    window = 0
    aliases = {}
    if caches is None:
        out_specs.append(pl.BlockSpec((bb, t, SWA_KV_WIDTH), lambda i: (i, 0, 0)))
        out_shape.append(jax.ShapeDtypeStruct((b, t, SWA_KV_WIDTH), F32))
    else:
        k_t, v_t, layer, depth, prev_k, prev_v = caches
        window = k_t.shape[-1]
        assert window == SWA_KV_WIDTH and t <= window
        cache_block = pl.BlockSpec((None, bb, SWA_KV_WIDTH, window), lambda i: (layer, i, 0, 0))
        in_specs += [cache_block] * 2
        args += [k_t, v_t]
        out_specs += [cache_block] * 2
        out_shape += [jax.ShapeDtypeStruct((depth, b, SWA_KV_WIDTH, window), F32)] * 2
    in_specs += [pl.BlockSpec((t, 128), lambda i: (0, 0))] * 2
    args += [cos, sin]
    if caches is not None and prev_k is not None:
        in_specs += [pl.BlockSpec(memory_space=pl.ANY)] * 2
        aliases = {len(args): 1, len(args) + 1: 2}
        args += [prev_k, prev_v]
    return pl.pallas_call(
        functools.partial(_swa_step_kernel, bb=bb, t_new=t, window=window),
        grid=(b // bb,),
        in_specs=in_specs,
        out_specs=out_specs,
        out_shape=out_shape,
        input_output_aliases=aliases,
        compiler_params=_cparams(("parallel",)),
        name="swa_step",
    )(*args)


def _merge_kernel(oa_ref, ob_ref, oc_ref, ga_ref, gb_ref, gc_ref, x_ref, pa_ref, pb_ref, pc_ref, wo_ref, o_ref):
    def branch(o_r, g_r, p_r):
        return jax.nn.sigmoid(g_r[...].astype(F32)) * jnp.dot(o_r[...].astype(BF16), p_r[...], preferred_element_type=F32)

    m = branch(oa_ref, ga_ref, pa_ref) + branch(ob_ref, gb_ref, pb_ref) + branch(oc_ref, gc_ref, pc_ref)
    o_ref[...] = x_ref[...] + jnp.dot(m.astype(BF16), wo_ref[...], preferred_element_type=F32)


def _merge(oa, ob, oc, h, x, pa, pb, pc, wo):
    m = x.shape[0]
    tm = min(m, 512)
    assert m % tm == 0
    row =lambda w: pl.BlockSpec((tm, w), lambda i: (i, 0))
    gate = lambda c: pl.BlockSpec((tm, D_MODEL), lambda i: (i, c))
    full = lambda a: pl.BlockSpec(a.shape, lambda i: (0, 0))
    return pl.pallas_call(
        _merge_kernel,
        grid=(m // tm,),
        in_specs=[row(512), row(512), row(512), gate(0), gate(1), gate(2), row(D_MODEL),
                  full(pa), full(pb), full(pc), full(wo)],
        out_specs=row(D_MODEL),
        out_shape=jax.ShapeDtypeStruct((m, D_MODEL), F32),
        compiler_params=_cparams(("parallel",)),
        name="merge",
    )(oa, ob, oc, h, h, h, x, pa, pb, pc, wo)


def _mlp_kernel(x_ref, nw_ref, wu_ref, wd_ref, fnw_ref, o_ref, *, final):
    x = x_ref[...]
    xn = _rms(x, nw_ref[...]).astype(BF16)
    y = x
    for j in range(D_FF // MLP_TF):
        fs = slice(j * MLP_TF, (j + 1) * MLP_TF)
        a = jnp.maximum(jnp.dot(xn, wu_ref[:, fs], preferred_element_type=F32), 0.0)
        y = y + jnp.dot((a * a).astype(BF16), wd_ref[fs, :], preferred_element_type=F32)
    o_ref[...] = _rms(y, fnw_ref[...]) if final else y


def _mlp(x, nw, wu, wd, fnw, final):
    m = x.shape[0]
    tm = min(m, 512)
    assert m % tm == 0
    resident = lambda a: pl.BlockSpec(a.shape, lambda i: (0, 0), pipeline_mode=pl.Buffered(1))
    return pl.pallas_call(
        functools.partial(_mlp_kernel, final=final),
        grid=(m // tm,),
        in_specs=[
            pl.BlockSpec((tm, D_MODEL), lambda i: (i, 0)),
            pl.BlockSpec((1, D_MODEL), lambda i: (0, 0)),
            resident(wu),
            resident(wd),
            pl.BlockSpec((1, D_MODEL), lambda i: (0, 0)),
        ],
        out_specs=pl.BlockSpec((tm, D_MODEL), lambda i: (i, 0)),
        out_shape=jax.ShapeDtypeStruct((m, D_MODEL), F32),
        compiler_params=_cparams(("parallel",)),
        name="mlp",
    )(x, nw, wu, wd, fnw)


def _rope_tables(pos0, t):
    half = SWA_HD // 2
    inv = ROPE_THETA ** (-jnp.arange(half, dtype=F32) / half)
    ang = (pos0 + jnp.arange(t)).astype(F32)[:, None] * inv[None, :]
    cos = jnp.cos(ang)
    sin = jnp.sin(ang)
    return jnp.tile(cos, (1, 4)), jnp.tile(jnp.concatenate([-sin, sin], axis=1), (1, 2))


def _prep_layer(l, norm1_w, w_in, pool_w, pool_scale, dn_conv_w, dn_a_log, dn_dt_bias, dn_onorm_w, swa_sinks,
                proj_a, proj_b, proj_c, w_out, norm2_w, w_up, w_down):
    w = w_in[l]
    segments = ((3336, 6408), (512, 2048), (0, 512), (2048, 2560), (2568, 3080), (3080, 3336), (2560, 2568))
    w_perm = jnp.concatenate(
        [w[:, a:b].astype(BF16) for a, b in segments]
        + [jnp.zeros((D_MODEL, H_WIDTH - BA_OFF - 2 * DN_HEADS), BF16)], axis=1)
    lane_pad = lambda v: jnp.zeros((1, 128), F32).at[0, DN_HEADS:2 * DN_HEADS].set(v.astype(F32))
    row_pad = lambda v: jnp.zeros((2 * DN_HEADS, 128), F32).at[DN_HEADS:].set(
        jnp.broadcast_to(v.astype(F32)[:, None], (DN_HEADS, 128)))
    return dict(
        norm1=norm1_w[l][None].astype(F32), w_in=w_perm,
        pool_w=pool_w[l].astype(BF16), pool_scale=pool_scale[l][None].astype(F32),
        conv_w=dn_conv_w[l].astype(F32),
        gate=(lane_pad(dn_a_log[l]), lane_pad(dn_dt_bias[l]), row_pad(dn_a_log[l]), row_pad(dn_dt_bias[l])),
        onw=dn_onorm_w[l][None].astype(F32), sinks=swa_sinks[l].astype(F32),
        pa=proj_a[l].astype(BF16), pb=proj_b[l].astype(BF16), pc=proj_c[l].astype(BF16),
        wo=w_out[l].astype(BF16), norm2=norm2_w[l][None].astype(F32),
        wu=w_up[l].astype(BF16), wd=w_down[l].astype(BF16))


def _group_step(x3, p, fnw, final, pos0, pool_init, conv_init, s0, kind, swa_extra, pool_tiles, delta_tiles,
                swa_bb=1, s_stack=None, act=F32, fuse_conv=False):
    b, t, _ = x3.shape
    x2 = x3.reshape(b * t, D_MODEL)
    h, conv_tails = _inproj(x2, p['norm1'], p['w_in'], act,
                            conv=(conv_init, p['conv_w'], t) if fuse_conv else None)
    h3 = h.reshape(b, t, H_WIDTH)
    o_a = _pool(h3, pool_init, p['pool_w'], p['pool_scale'], pos0, *pool_tiles)
    o_b, s_new = _delta(h3, conv_init, s0[0], s0[1], p['conv_w'], p['gate'], p['onw'], *delta_tiles,
                        stack=s_stack, conv_done=fuse_conv)
    cos, sin = _rope_tables(pos0, t)
    if kind == 'prompt':
        o_c, k_rot = _swa_prompt(h3, swa_extra, cos, sin, p['sinks'], pos0)
    else:
        o_c, *k_rot = _swa_step(h3, swa_extra, cos, sin, p['sinks'], swa_bb)
    m = b * t
    h1 = _merge(o_a.reshape(m, 512), o_b.reshape(m, 512), o_c.reshape(m, 512), h, x2,
                p['pa'], p['pb'], p['pc'], p['wo'])
    out = _mlp(h1, p['norm2'], p['wu'], p['wd'], fnw, final)
    return out.reshape(b, t, D_MODEL), h3, s_new, k_rot, conv_tails


def kernel(x_prompt, x_sample, state_pool, state_conv, state_delta, cache_swa_k, cache_swa_v, meta_tokens, norm1_w, w_in, pool_w, pool_scale, dn_conv_w, dn_a_log, dn_dt_bias, dn_onorm_w, swa_sinks, proj_a, proj_b, proj_c, w_out, norm2_w, w_up, w_down, final_norm_w):
    depth = w_in.shape[0]
    bp, tp, _ = x_prompt.shape
    bs, ts, _ = x_sample.shape
    past = cache_swa_k.shape[2]
    fnw = final_norm_w[None].astype(F32)

    xm = meta_tokens[None].astype(F32)
    xp = x_prompt
    xs = x_sample
    outs = {k: [] for k in ('pool_p', 'conv_p', 'k_p', 'v_p', 'pool_s', 'conv_s')}
    delta_p = delta_s = None
    k_s = v_s = None
    to_t = lambda c: c.transpose(0, 1, 3, 4, 2).reshape(depth, bs, SWA_KV_WIDTH, past)
    cache_k_t, cache_v_t = to_t(cache_swa_k), to_t(cache_swa_v)
    for l in range(depth):
        p = _prep_layer(l, norm1_w, w_in, pool_w, pool_scale, dn_conv_w, dn_a_log, dn_dt_bias, dn_onorm_w,
                        swa_sinks, proj_a, proj_b, proj_c, w_out, norm2_w, w_up, w_down)
        final = l == depth - 1

        xm, hm, s_m, k_m, _ = _group_step(
            xm, p, fnw, final, 0,
            jnp.zeros((1, POOL_HALO, POOL_WIDTH), F32), jnp.zeros((1, CONV_HALO, CONV_CH), F32),
            (jnp.zeros((1, 1, DN_HEADS, DN_DK, DN_DV), F32), 0), 'step', None,
            (1, N_META), (1, N_META, N_META))

        meta_kv = jnp.zeros((SWA_WINDOW, 256), F32)
        meta_kv = meta_kv.at[SWA_WINDOW - N_META:, :SWA_KV_WIDTH].set(k_m[0][0])
        meta_kv = meta_kv.at[SWA_WINDOW - N_META:, SWA_KV_WIDTH:].set(hm[0, :, KVC_OFF + SWA_KV_WIDTH:KVC_OFF + 256])
        xp, hp, s_p, k_p, tails_p = _group_step(
            xp, p, fnw, final, N_META,
            jnp.broadcast_to(hm[:, :, U_OFF:U_OFF + POOL_WIDTH], (bp, POOL_HALO, POOL_WIDTH)),
            jnp.broadcast_to(hm[:, N_META - CONV_HALO:, QKV_OFF:QKV_OFF + CONV_CH], (bp, CONV_HALO, CONV_CH)),
            (jnp.broadcast_to(s_m[None], (1, bp, DN_HEADS, DN_DK, DN_DV)), 0), 'prompt', meta_kv,
            (1, 512), (2, DN_CHUNK, DN_CHUNK), s_stack=(l, depth, delta_p), act=BF16, fuse_conv=True)
        delta_p = s_p
        outs['pool_p'].append(hp[:, tp - POOL_BUF:, U_OFF:U_OFF + POOL_WIDTH].astype(F32))
        if tails_p is None:
            outs['conv_p'].append(hp[:, tp - (CONV_W - 1):, QKV_OFF:QKV_OFF + CONV_CH].astype(F32))
        else:
            outs['conv_p'].append(tails_p.reshape(bp, -1, CONV_HALO, CONV_CH)[:, -1, CONV_HALO - (CONV_W - 1):])
        outs['k_p'].append(k_p.reshape(bp, SWA_WINDOW, SWA_KV_HEADS, SWA_HD))
        outs['v_p'].append(hp[:, tp - SWA_WINDOW:, KVC_OFF + SWA_KV_WIDTH:KVC_OFF + 256].astype(F32)
                           .reshape(bp, SWA_WINDOW, SWA_KV_HEADS, SWA_HD))

        xs, hs, s_s, (k_s, v_s), _ = _group_step(
            xs, p, fnw, final, PAST_LEN,
            jnp.concatenate([jnp.zeros((bs, POOL_HALO - POOL_BUF, POOL_WIDTH), F32), state_pool[l]], axis=1),
            jnp.concatenate([jnp.zeros((bs, CONV_HALO - (CONV_W - 1), CONV_CH), F32), state_conv[l]], axis=1),
            (state_delta, l), 'step', (cache_k_t, cache_v_t, l, depth, k_s, v_s),
            (bs, ts), (16, ts, ts), 16, s_stack=(l, depth, delta_s))
        delta_s = s_s
        outs['pool_s'].append(jnp.concatenate([state_pool[l], hs[:, :, U_OFF:U_OFF + POOL_WIDTH]], axis=1)[:, -POOL_BUF:])
        outs['conv_s'].append(jnp.concatenate([state_conv[l], hs[:, :, QKV_OFF:QKV_OFF + CONV_CH]], axis=1)[:, -(CONV_W - 1):])

    st = {k: jnp.stack(v) for k, v in outs.items()}
    from_t = lambda c: c.reshape(depth, bs, SWA_KV_HEADS, SWA_HD, past).transpose(0, 1, 4, 2, 3)
    return (xp, xs, st['pool_p'], st['conv_p'], delta_p, st['k_p'], st['v_p'],
            st['pool_s'], st['conv_s'], delta_s, from_t(k_s), from_t(v_s))
```

```python
import functools
import math

import jax
import jax.numpy as jnp
from jax import lax
from jax.experimental import pallas as pl
from jax.experimental.pallas import tpu as pltpu

F32 = jnp.float32
BF16 = jnp.bfloat16

D_MODEL = 1024
N_META = 16
EPS = 1e-6
POOL_GROUPS = 4
POOL_GROUP_DIM = 128
POOL_WIDTH = 512
POOL_WINDOWS = (2, 4, 8, 16)
POOL_BUF = 15
POOL_HALO = 16
DN_HEADS = 4
DN_DK = 128
DN_DV = 128
DN_QK = 512
DN_VW = 512
CONV_W = 4
CONV_CH = 1536
CONV_HALO = 8
DN_CHUNK = 64
GROUP_ROWS = 128
SWA_HEADS = 8
SWA_KV_HEADS = 2
SWA_GROUP = 4
SWA_HD = 64
SWA_WIDTH = 512
SWA_KV_WIDTH = 128
SWA_WINDOW = 128
ROPE_THETA = 10000.0
D_FF = 4096
MLP_TF = 1024
PAST_LEN = 16384

G_OFF = 0
QKV_OFF = 3072
U_OFF = 4608
Z_OFF = 5120
QC_OFF = 5632
KVC_OFF = 6144
BA_OFF = 6400
H_WIDTH = 6528
IN_CHUNKS = ((0, 1536), (1536, 3072), (3072, 4608), (4608, 6144), (6144, 6528))

VMEM_LIMIT = 56 * 1024 * 1024
NEG_BIG = -1e30


def _cparams(sem):
    return pltpu.CompilerParams(dimension_semantics=sem, vmem_limit_bytes=VMEM_LIMIT)


def _rms(x, w):
    return x * lax.rsqrt(jnp.mean(x * x, axis=-1, keepdims=True) + EPS) * w


def _l2n(v):
    return v * lax.rsqrt(jnp.sum(v * v, axis=-1, keepdims=True) + EPS)


def _short_conv(ext_ref, cw_ref, rows):
    base = CONV_HALO - (CONV_W - 1)
    y = ext_ref[..., base:base + rows, :] * cw_ref[0:1, :]
    for j in range(1, CONV_W):
        y = y + ext_ref[..., base + j:base + j + rows, :] * cw_ref[j:j + 1, :]
    return y * jax.nn.sigmoid(y)


def _qkv_heads(y):
    heads = range(DN_HEADS)
    qn = [_l2n(y[:, h * DN_DK:(h + 1) * DN_DK]) * (DN_DK ** -0.5) for h in heads]
    kn = [_l2n(y[:, DN_QK + h * DN_DK:DN_QK + (h + 1) * DN_DK]) for h in heads]
    vh = [y[:, 2 * DN_QK + h * DN_DV:2 * DN_QK + (h + 1) * DN_DV] for h in heads]
    return qn, kn, vh


def _inproj_kernel(x_ref, nw_ref, w_ref, *rest, tiles_per_seq):
    if tiles_per_seq:
        cinit_ref, cw_ref, o_ref, tail_ref, ext_ref = rest
    else:
        (o_ref,) = rest
    tm = x_ref.shape[0]
    if tiles_per_seq:
        first = pl.program_id(0) % tiles_per_seq == 0

        @pl.when(first)
        def _():
            ext_ref[0:CONV_HALO, :] = cinit_ref[0]

        @pl.when(jnp.logical_not(first))
        def _():
            ext_ref[0:CONV_HALO, :] = ext_ref[tm:tm + CONV_HALO, :]

    xn = _rms(x_ref[...], nw_ref[...]).astype(BF16)
    for a, b in sorted(IN_CHUNKS, key=lambda ab: ab[0] != QKV_OFF):
        acc = jnp.dot(xn, w_ref[:, a:b], preferred_element_type=F32)
        if tiles_per_seq and a == QKV_OFF:
            ext_ref[CONV_HALO:CONV_HALO + tm, :] = acc
            tail_ref[0] = acc[tm - CONV_HALO:, :]
            qn, kn, vh = _qkv_heads(_short_conv(ext_ref, cw_ref, tm))
            acc = jnp.concatenate(qn + kn + vh, axis=1)
        o_ref[:, a:b] = acc.astype(o_ref.dtype)


def _inproj(x, nw, w, act, conv=None):
    m = x.shape[0]
    tm = min(m, 512 if act == BF16 else 256)
    assert m % tm == 0
    in_specs = [
        pl.BlockSpec((tm, D_MODEL), lambda i: (i, 0)),
        pl.BlockSpec((1, D_MODEL), lambda i: (0, 0)),
        pl.BlockSpec((D_MODEL, H_WIDTH), lambda i: (0, 0), pipeline_mode=pl.Buffered(1)),
    ]
    out_specs = [pl.BlockSpec((tm, H_WIDTH), lambda i: (i, 0))]
    out_shape = [jax.ShapeDtypeStruct((m, H_WIDTH), act)]
    args = [x, nw, w]
    scratch = []
    tiles_per_seq = 0
    if conv is not None:
        cinit, cw, seq_rows = conv
        assert seq_rows % tm == 0
        tiles_per_seq = seq_rows // tm
        in_specs += [pl.BlockSpec((1, CONV_HALO, CONV_CH), lambda i: (i // tiles_per_seq, 0, 0)),
                     pl.BlockSpec((CONV_W, CONV_CH), lambda i: (0, 0))]
        args += [cinit, cw]
        out_specs.append(pl.BlockSpec((1, CONV_HALO, CONV_CH), lambda i: (i, 0, 0)))
        out_shape.append(jax.ShapeDtypeStruct((m // tm, CONV_HALO, CONV_CH), F32))
        scratch = [pltpu.VMEM((CONV_HALO + tm, CONV_CH), F32)]
    res = pl.pallas_call(
        functools.partial(_inproj_kernel, tiles_per_seq=tiles_per_seq),
        grid=(m // tm,),
        in_specs=in_specs,
        out_specs=out_specs,
        out_shape=out_shape,
        scratch_shapes=scratch,
        compiler_params=_cparams(("arbitrary",) if conv is not None else ("parallel",)),
        name="inproj",
    )(*args)
    return (res[0], res[1]) if conv is not None else (res[0], None)


def _pool_kernel(u_ref, init_ref, pw_ref, ps_ref, o_ref, ext_ref, *, pos0, tt):
    t = pl.program_id(1)
    bb = u_ref.shape[0]

    @pl.when(t == 0)
    def _():
        ext_ref[:, 0:POOL_HALO, :] = init_ref[...]

    @pl.when(t > 0)
    def _():
        ext_ref[:, 0:POOL_HALO, :] = ext_ref[:, tt:tt + POOL_HALO, :]

    ext_ref[:, POOL_HALO:POOL_HALO + tt, :] = u_ref[...].astype(F32)

    pos = pos0 + t * tt + lax.broadcasted_iota(jnp.int32, (1, tt, 1), 1)
    for g, w in enumerate(POOL_WINDOWS):
        cs = slice(g * POOL_GROUP_DIM, (g + 1) * POOL_GROUP_DIM)
        s = ext_ref[:, :, cs]
        x = s[:, POOL_HALO:POOL_HALO + tt]
        k = 1
        while k < w:
            s = s + pltpu.roll(s, k, 1)
            k *= 2
        s = s[:, POOL_HALO:POOL_HALO + tt]
        cnt = jnp.minimum(w, pos + 1).astype(F32)
        d = (s / cnt - x).reshape(bb * tt, POOL_GROUP_DIM)
        y = jnp.dot(d.astype(BF16), pw_ref[g], preferred_element_type=F32)
        o_ref[:, :, cs] = (y * ps_ref[:, cs]).reshape(bb, tt, POOL_GROUP_DIM).astype(o_ref.dtype)


def _pool(h3, init, pw, ps, pos0, bb, tt):
    b, t, _ = h3.shape
    return pl.pallas_call(
        functools.partial(_pool_kernel, pos0=pos0, tt=tt),
        grid=(b // bb, t // tt),
        in_specs=[
            pl.BlockSpec((bb, tt, POOL_WIDTH), lambda i, j: (i, j, U_OFF // POOL_WIDTH)),
            pl.BlockSpec((bb, POOL_HALO, POOL_WIDTH), lambda i, j: (i, 0, 0)),
            pl.BlockSpec((POOL_GROUPS, POOL_GROUP_DIM, POOL_GROUP_DIM), lambda i, j: (0, 0, 0)),
            pl.BlockSpec((1, POOL_WIDTH), lambda i, j: (0, 0)),
        ],
        out_specs=pl.BlockSpec((bb, tt, POOL_WIDTH), lambda i, j: (i, j, 0)),
        out_shape=jax.ShapeDtypeStruct((b, t, POOL_WIDTH), h3.dtype),
        scratch_shapes=[pltpu.VMEM((bb, POOL_HALO + tt, POOL_WIDTH), F32)],
        compiler_params=_cparams(("parallel", "arbitrary")),
        name="pool",
    )(h3, init, pw, ps)


def _softplus(x):
    return jnp.maximum(x, 0.0) + jnp.log1p(jnp.exp(-jnp.abs(x)))


def _dot_nt(a, b, **kw):
    return lax.dot_general(a, b, (((1,), (1,)), ((), ())), preferred_element_type=F32, **kw)


def _dot_tn(a, b):
    return lax.dot_general(a, b, (((0,), (0,)), ((), ())), preferred_element_type=F32)


def _dot(a, b):
    return jnp.dot(a, b, preferred_element_type=F32)


def _delta_kernel(qkv_ref, z_ref, ba_ref, cinit_ref, s0_ref, cw_ref, alog_ref, dtb_ref, alogt_ref, dtbt_ref,
                  onw_ref, *rest, bb, tb, chunk, conv_done):
    o_ref, s_ref, ext_ref = rest[-3:]
    t = pl.program_id(1)
    rows = bb * tb
    cpb = tb // chunk
    heads = list(range(DN_HEADS))
    mm = BF16 if rows >= 16 else F32
    mc = BF16 if chunk >= 16 else F32

    @pl.when(t == 0)
    def _():
        s_ref[...] = s0_ref[...]

    if conv_done:
        y = qkv_ref[...].astype(F32).reshape(rows, CONV_CH)
        qn = [y[:, h * DN_DK:(h + 1) * DN_DK] for h in heads]
        kn = [y[:, DN_QK + h * DN_DK:DN_QK + (h + 1) * DN_DK] for h in heads]
        vh = [y[:, 2 * DN_QK + h * DN_DV:2 * DN_QK + (h + 1) * DN_DV] for h in heads]
    else:
        @pl.when(t == 0)
        def _():
            ext_ref[:, 0:CONV_HALO, :] = cinit_ref[...]

        @pl.when(t > 0)
        def _():
            ext_ref[:, 0:CONV_HALO, :] = ext_ref[:, tb:tb + CONV_HALO, :]

        ext_ref[:, CONV_HALO:CONV_HALO + tb, :] = qkv_ref[...].astype(F32)
        qn, kn, vh = _qkv_heads(_short_conv(ext_ref, cw_ref, tb).reshape(rows, CONV_CH))

    gr = min(rows, GROUP_ROWS)
    groups = rows // gr
    spg = gr // tb
    keys = [(gi, h) for gi in range(groups) for h in heads]

    def part(v, gi):
        return v[gi * gr:(gi + 1) * gr]

    bav = ba_ref[...].astype(F32).reshape(rows, 128)
    zv = z_ref[...].astype(F32).reshape(rows, DN_VW)

    row = lax.broadcasted_iota(jnp.int32, (gr, gr), 0)
    col = lax.broadcasted_iota(jnp.int32, (gr, gr), 1)
    shift = int(math.log2(chunk))
    same = (row >> shift) == (col >> shift)
    tri = same & (row >= col)
    strict = same & (row > col)
    lane = lax.broadcasted_iota(jnp.int32, (gr, 128), 1)

    gcol, bcol, grow = {}, {}, {}
    for gi in range(groups):
        bav_g = part(bav, gi)
        if gr == 128:
            sub = bav_g.T[0:2 * DN_HEADS, :]
            g_t = -jnp.exp(alogt_ref[...]) * _softplus(sub + dtbt_ref[...])
            gcum_t = jnp.dot(g_t, (same & (col >= row)).astype(F32), preferred_element_type=F32,
                             precision=lax.Precision.HIGHEST)
            live = lax.broadcasted_iota(jnp.int32, sub.shape, 0) < DN_HEADS
            packed = jnp.where(live, jax.nn.sigmoid(sub), gcum_t)
            cols = jnp.concatenate([packed, jnp.zeros((gr - 2 * DN_HEADS, gr), F32)], axis=0).T
            beta_full = gcum = cols
            for h in heads:
                grow[gi, h] = gcum_t[DN_HEADS + h:DN_HEADS + h + 1, :]
        else:
            beta_full = jax.nn.sigmoid(bav_g)
            g_full = -jnp.exp(alog_ref[...]) * _softplus(bav_g + dtb_ref[...])
            gcum = jnp.dot(tri.astype(F32), g_full, preferred_element_type=F32, precision=lax.Precision.HIGHEST)
            ones = jnp.ones((gr, 128), F32)
            for h in heads:
                grow[gi, h] = _dot_nt(ones, jnp.where(lane == DN_HEADS + h, gcum, 0.0),
                                      precision=lax.Precision.HIGHEST)
        for h in heads:
            gcol[gi, h] = jnp.sum(jnp.where(lane == DN_HEADS + h, gcum, 0.0), axis=1, keepdims=True)
            bcol[gi, h] = jnp.sum(jnp.where(lane == h, beta_full, 0.0), axis=1, keepdims=True)
    eg = {k: jnp.exp(gcol[k]) for k in keys}
    qs = {(gi, h): part(qn[h], gi) for gi, h in keys}
    ks = {(gi, h): part(kn[h], gi) for gi, h in keys}
    vs = {(gi, h): part(vh[h], gi) for gi, h in keys}

    bshift = min(shift, 4)
    decay = {k: jnp.where(tri, jnp.exp(jnp.where(tri, gcol[k] - grow[k], 0.0)), 0.0) for k in keys}
    kb = {k: ks[k].astype(mm) for k in keys}
    kk = {k: _dot_nt(kb[k], kb[k]) for k in keys}
    qk = {k: _dot_nt(qs[k].astype(mm), kb[k]) for k in keys}
    m = {k: jnp.where(strict, kk[k] * decay[k] * bcol[k], 0.0) for k in keys}
    p = {k: jnp.where((row >> bshift) == (col >> bshift), -m[k], 0.0) for k in keys}
    xp = p
    for _ in range(bshift - 1):
        xb = {k: xp[k].astype(mm) for k in keys}
        xp = {k: _dot(xb[k], xb[k]) for k in keys}
        p = {k: p[k] + xp[k] + _dot(p[k].astype(mm), xp[k].astype(mm)) for k in keys}
    for s in range(bshift, shift):
        lower = ((row >> (s + 1)) == (col >> (s + 1))) & ((row >> s) > (col >> s))
        c_blk = {k: jnp.where(lower, m[k], 0.0) for k in keys}
        pb = {k: p[k].astype(mm) for k in keys}
        a = {k: c_blk[k] + _dot(pb[k], c_blk[k].astype(mm)) for k in keys}
        p = {k: p[k] - a[k] - _dot(a[k].astype(mm), pb[k]) for k in keys}
    rhs = {k: jnp.concatenate([vs[k] * bcol[k], ks[k] * (bcol[k] * eg[k])], axis=1) for k in keys}
    uw = {k: rhs[k] + _dot(p[k].astype(mm), rhs[k].astype(mm)) for k in keys}
    qkd = {k: (qk[k] * decay[k]).astype(mm) for k in keys}
    qg = {k: qs[k] * eg[k] for k in keys}

    pairs = [(b, h) for b in range(bb) for h in heads]
    state = {bh: s_ref[bh[0], bh[1]] for bh in pairs}
    dlt = {k: [] for k in keys}
    oq = {k: [] for k in keys}
    for c in range(cpb):
        def rs(b):
            start = (b % spg) * tb + c * chunk
            return slice(start, start + chunk)
        prod = {}
        for b, h in pairs:
            k = (b // spg, h)
            lhs = jnp.concatenate([uw[k][rs(b), DN_DV:], qg[k][rs(b)]], axis=0).astype(mc)
            prod[b, h] = _dot(lhs, state[b, h].astype(mc))
        for b, h in pairs:
            k = (b // spg, h)
            r = rs(b)
            d = uw[k][r, :DN_DV] - prod[b, h][:chunk]
            glast = gcol[k][r.stop - 1:r.stop, :]
            kg = ks[k][r] * jnp.exp(glast - gcol[k][r])
            state[b, h] = state[b, h] * jnp.exp(glast) + _dot_tn(kg.astype(mc), d.astype(mc))
            dlt[k].append((r.start, d))
            oq[k].append((r.start, prod[b, h][chunk:]))
    for b, h in pairs:
        s_ref[b, h] = state[b, h]

    def stack(parts):
        parts = [v for _, v in sorted(parts, key=lambda sv: sv[0])]
        return parts[0] if len(parts) == 1 else jnp.concatenate(parts, axis=0)

    for gi, h in keys:
        hs = slice(h * DN_DV, (h + 1) * DN_DV)
        o = stack(oq[gi, h]) + _dot(qkd[gi, h], stack(dlt[gi, h]).astype(mm))
        zh = part(zv, gi)[:, hs]
        gated = _rms(o, onw_ref[...]) * (zh * jax.nn.sigmoid(zh))
        o_ref[gi * spg:(gi + 1) * spg, :, hs] = gated.reshape(spg, tb, DN_DV).astype(o_ref.dtype)


def _delta(h3, cinit, s0, s0_layer, cw, gate, onw, bb, tb, chunk, stack=None, conv_done=False):
    b, t, _ = h3.shape
    assert b % bb == 0 and t % tb == 0 and tb % chunk == 0
    state_block = (None, bb, DN_HEADS, DN_DK, DN_DV)
    in_specs = [
        pl.BlockSpec((bb, tb, CONV_CH), lambda i, j: (i, j, QKV_OFF // CONV_CH)),
        pl.BlockSpec((bb, tb, DN_VW), lambda i, j: (i, j, Z_OFF // DN_VW)),
        pl.BlockSpec((bb, tb, 128), lambda i, j: (i, j, BA_OFF // 128)),
        pl.BlockSpec((bb, CONV_HALO, CONV_CH), lambda i, j: (i, 0, 0)),
        pl.BlockSpec(state_block, lambda i, j: (s0_layer, i, 0, 0, 0)),
        pl.BlockSpec((CONV_W, CONV_CH), lambda i, j: (0, 0)),
        pl.BlockSpec((1, 128), lambda i, j: (0, 0)),
        pl.BlockSpec((1, 128), lambda i, j: (0, 0)),
        pl.BlockSpec((2 * DN_HEADS, 128), lambda i, j: (0, 0)),
        pl.BlockSpec((2 * DN_HEADS, 128), lambda i, j: (0, 0)),
        pl.BlockSpec((1, DN_DV), lambda i, j: (0, 0)),
    ]
    args = [h3, h3, h3, cinit, s0, cw, *gate, onw]
    layer, depth, prev = (0, 1, None) if stack is None else stack
    aliases = {}
    if prev is not None:
        in_specs.append(pl.BlockSpec(memory_space=pl.ANY))
        aliases = {len(args): 1}
        args.append(prev)
    o, s = pl.pallas_call(
        functools.partial(_delta_kernel, bb=bb, tb=tb, chunk=chunk, conv_done=conv_done),
        grid=(b // bb, t // tb),
        in_specs=in_specs,
        out_specs=[
            pl.BlockSpec((bb, tb, DN_VW), lambda i, j: (i, j, 0)),
            pl.BlockSpec(state_block, lambda i, j: (layer, i, 0, 0, 0)),
        ],
        out_shape=[
            jax.ShapeDtypeStruct((b, t, DN_VW), h3.dtype),
            jax.ShapeDtypeStruct((depth, b, DN_HEADS, DN_DK, DN_DV), F32),
        ],
        scratch_shapes=[pltpu.VMEM((bb, CONV_HALO + tb, CONV_CH), F32)],
        input_output_aliases=aliases,
        compiler_params=_cparams(("parallel", "arbitrary")),
        name="delta",
    )(*args)
    return o, (s[0] if stack is None else s)


def _rope(x, cos, sin):
    width = x.shape[-1]
    reps = width // cos.shape[-1]
    if reps > 1:
        cos = jnp.concatenate([cos] * reps, axis=1)
        sin = jnp.concatenate([sin] * reps, axis=1)
    lane = lax.broadcasted_iota(jnp.int32, x.shape, 1)
    first_half = (lane & (SWA_HD - 1)) < (SWA_HD // 2)
    other = jnp.where(first_half, pltpu.roll(x, width - SWA_HD // 2, 1), pltpu.roll(x, SWA_HD // 2, 1))
    return x * cos + other * sin


def _pad_heads(x, lo, fill):
    xr = pltpu.roll(x, SWA_HD, 1)
    z = jnp.full_like(x, fill)
    return [jnp.where(lo, x, z), jnp.where(lo, z, xr), jnp.where(lo, xr, z), jnp.where(lo, z, x)]


def _swa_prompt_kernel(sinks_ref, q_ref, kv_ref, meta_ref, cos_ref, sin_ref, o_ref, klast_ref, kcat_ref, vcat_ref,
                       *, pos0):
    n = pl.program_id(1)
    blk = SWA_WINDOW
    lo = lax.broadcasted_iota(jnp.int32, (blk, 128), 1) < SWA_HD

    def put(half, k, v):
        rows = slice(half * blk, (half + 1) * blk)
        for i, (a, b) in enumerate(zip(_pad_heads(k, lo, 0.0), _pad_heads(v, lo, 1.0))):
            kcat_ref[i, rows, :] = a.astype(BF16)
            vcat_ref[i, rows, :] = b.astype(BF16)

    @pl.when(n == 0)
    def _():
        put(0, meta_ref[:, :SWA_KV_WIDTH], meta_ref[:, SWA_KV_WIDTH:])

    @pl.when(n > 0)
    def _():
        kcat_ref[:, 0:blk, :] = kcat_ref[:, blk:2 * blk, :]
        vcat_ref[:, 0:blk, :] = vcat_ref[:, blk:2 * blk, :]

    cos = cos_ref[...]
    sin = sin_ref[...]
    q = _rope(q_ref[0].astype(F32), cos, sin) * (SWA_HD ** -0.5)
    kv = kv_ref[0].astype(F32)
    k = _rope(kv[:, :SWA_KV_WIDTH], cos, sin)
    klast_ref[0] = k
    put(1, k, kv[:, SWA_KV_WIDTH:])

    r = lax.broadcasted_iota(jnp.int32, (blk, 2 * blk), 0)
    j = lax.broadcasted_iota(jnp.int32, (blk, 2 * blk), 1)
    jmin = jnp.maximum(blk - pos0 - n * blk, 0)
    mask = ((j < blk) & (j > r) & (j >= jmin)) | ((j >= blk) & ((j - blk) <= r))
    heads = range(SWA_HEADS)
    qt = [q[:, i * 128:(i + 1) * 128].astype(BF16) for i in range(SWA_HEADS // 2)]
    s = [jnp.where(mask, _dot_nt(qt[h // 2], kcat_ref[2 * (h // SWA_GROUP) + h % 2]), NEG_BIG) for h in heads]
    m = [jnp.maximum(jnp.max(s[h], axis=1, keepdims=True), sinks_ref[h]) for h in heads]
    e = [jnp.exp(s[h] - m[h]).astype(BF16) for h in heads]
    sink_w = [jnp.exp(sinks_ref[h] - m[h]) for h in heads]
    for i in range(SWA_HEADS // 2):
        he, ho = 2 * i, 2 * i + 1
        g = he // SWA_GROUP
        a_e = _dot(e[he], vcat_ref[2 * g])
        a_o = _dot(e[ho], vcat_ref[2 * g + 1])
        den = (jnp.where(lo, pltpu.roll(a_e, SWA_HD, 1), pltpu.roll(a_o, SWA_HD, 1))
               + jnp.where(lo, sink_w[he], sink_w[ho]))
        o_ref[0, :, i * 128:(i + 1) * 128] = (jnp.where(lo, a_e, a_o) / den).astype(o_ref.dtype)


def _swa_prompt(h3, meta_kv, cos, sin, sinks, pos0):
    b, t, _ = h3.shape
    blk = SWA_WINDOW
    assert t % blk == 0
    return pl.pallas_call(
        functools.partial(_swa_prompt_kernel, pos0=pos0),
        grid=(b, t // blk),
        in_specs=[
            pl.BlockSpec(memory_space=pltpu.SMEM),
            pl.BlockSpec((1, blk, SWA_WIDTH), lambda i, n: (i, n, QC_OFF // SWA_WIDTH)),
            pl.BlockSpec((1, blk, 256), lambda i, n: (i, n, KVC_OFF // 256)),
            pl.BlockSpec((blk, 256), lambda i, n: (0, 0)),
            pl.BlockSpec((blk, 128), lambda i, n: (n, 0)),
            pl.BlockSpec((blk, 128), lambda i, n: (n, 0)),
        ],
        out_specs=[
            pl.BlockSpec((1, blk, SWA_WIDTH), lambda i, n: (i, n, 0)),
            pl.BlockSpec((1, blk, SWA_KV_WIDTH), lambda i, n: (i, 0, 0)),
        ],
        out_shape=[
            jax.ShapeDtypeStruct((b, t, SWA_WIDTH), h3.dtype),
            jax.ShapeDtypeStruct((b, blk, SWA_KV_WIDTH), F32),
        ],
        scratch_shapes=[pltpu.VMEM((4, 2 * blk, 128), BF16), pltpu.VMEM((4, 2 * blk, 128), BF16)],
        compiler_params=_cparams(("parallel", "arbitrary")),
        name="swa_prompt",
    )(sinks, h3, h3, meta_kv, cos, sin)


def _swa_step_kernel(sink_ref, q_ref, kv_ref, *rest, bb, t_new, window):
    if window:
        kc_ref, vc_ref, cos_ref, sin_ref = rest[:4]
        o_ref, kout_ref, vout_ref = rest[-3:]
    else:
        cos_ref, sin_ref, o_ref, knew_ref = rest
    cos = cos_ref[...]
    sin = sin_ref[...]
    rows = SWA_HEADS * t_new
    mn = BF16 if t_new >= 16 else F32
    lo = lax.broadcasted_iota(jnp.int32, (t_new, 128), 1) < SWA_HD
    tq = lax.broadcasted_iota(jnp.int32, (rows, t_new), 0) & (t_new - 1)
    mask_n = lax.broadcasted_iota(jnp.int32, (rows, t_new), 1) <= tq
    if window:
        diff = ((lax.broadcasted_iota(jnp.int32, (rows, window), 0) & (t_new - 1)) + window
                - lax.broadcasted_iota(jnp.int32, (rows, window), 1))
        mask_c = (diff >= 0) & (diff < SWA_WINDOW)
    sink = sink_ref[...]
    seqs = range(bb)

    qall, kk, vv = [], [], []
    for s in seqs:
        q = _rope(q_ref[s].astype(F32), cos, sin) * (SWA_HD ** -0.5)
        kv = kv_ref[s].astype(F32)
        k = _rope(kv[:, :SWA_KV_WIDTH], cos, sin)
        if not window:
            knew_ref[s] = k
        pieces = []
        for h in range(SWA_HEADS):
            g = h // SWA_GROUP
            tile = q[:, (h // 2) * 128:(h // 2 + 1) * 128]
            if h % 2 != g:
                tile = pltpu.roll(tile, SWA_HD, 1)
            pieces.append(jnp.where(lo, tile, 0.0) if g == 0 else jnp.where(lo, 0.0, tile))
        qall.append(jnp.concatenate(pieces, axis=0))
        kk.append(k)
        vv.append(kv[:, SWA_KV_WIDTH:])

    s_n = [jnp.where(mask_n, _dot_nt(qall[s].astype(mn), kk[s].astype(mn)), NEG_BIG) for s in seqs]
    m = [jnp.maximum(jnp.max(s_n[s], axis=1, keepdims=True), sink) for s in seqs]
    if window:
        s_c = [jnp.where(mask_c, _dot(qall[s].astype(BF16), kc_ref[s].astype(BF16)), NEG_BIG) for s in seqs]
        m = [jnp.maximum(m[s], jnp.max(s_c[s], axis=1, keepdims=True)) for s in seqs]
    e_n = [jnp.exp(s_n[s] - m[s]) for s in seqs]
    den = [jnp.sum(e_n[s], axis=1, keepdims=True) + jnp.exp(sink - m[s]) for s in seqs]
    acc = [_dot(e_n[s].astype(mn), vv[s].astype(mn)) for s in seqs]
    if window:
        e_c = [jnp.exp(s_c[s] - m[s]) for s in seqs]
        den = [den[s] + jnp.sum(e_c[s], axis=1, keepdims=True) for s in seqs]
        acc = [acc[s] + _dot_nt(e_c[s].astype(BF16), vc_ref[s].astype(BF16)) for s in seqs]

        def slide(old_t, new):
            kept = pltpu.roll(old_t, window - t_new, 1)
            fresh = jnp.concatenate([jnp.zeros((window - t_new, SWA_KV_WIDTH), F32), new], axis=0).T
            newest = lax.broadcasted_iota(jnp.int32, kept.shape, 1) >= window - t_new
            return jnp.where(newest, fresh, kept)

        for s in seqs:
            kout_ref[s] = slide(kc_ref[s], kk[s])
            vout_ref[s] = slide(vc_ref[s], vv[s])
    for s in seqs:
        a = acc[s] / den[s]
        for i in range(SWA_HEADS // 2):
            he, ho = 2 * i, 2 * i + 1
            g = he // SWA_GROUP
            a_e = a[he * t_new:(he + 1) * t_new]
            a_o = a[ho * t_new:(ho + 1) * t_new]
            if g == 1:
                a_e = pltpu.roll(a_e, SWA_HD, 1)
            else:
                a_o = pltpu.roll(a_o, SWA_HD, 1)
            o_ref[s, :, i * 128:(i + 1) * 128] = jnp.where(lo, a_e, a_o).astype(o_ref.dtype)


def _swa_step(h3, caches, cos, sin, sinks, bb):
    b, t, _ = h3.shape
    assert b % bb == 0
    sink_rows = jnp.repeat(sinks, t)[:, None]
    in_specs = [
        pl.BlockSpec((SWA_HEADS * t, 1), lambda i: (0, 0)),
        pl.BlockSpec((bb, t, SWA_WIDTH), lambda i: (i, 0, QC_OFF // SWA_WIDTH)),
        pl.BlockSpec((bb, t, 256), lambda i: (i, 0, KVC_OFF // 256)),
    ]
    args = [sink_rows, h3, h3]
    out_specs = [pl.BlockSpec((bb, t, SWA_WIDTH), lambda i: (i, 0, 0))]
    out_shape = [jax.ShapeDtypeStruct((b, t, SWA_WIDTH), h3.dtype)]
    window = 0
    aliases = {}
    if caches is None:
        out_specs.append(pl.BlockSpec((bb, t, SWA_KV_WIDTH), lambda i: (i, 0, 0)))
        out_shape.append(jax.ShapeDtypeStruct((b, t, SWA_KV_WIDTH), F32))
    else:
        k_t, v_t, layer, depth, prev_k, prev_v = caches
        window = k_t.shape[-1]
        assert window == SWA_KV_WIDTH and t <= window
        cache_block = pl.BlockSpec((None, bb, SWA_KV_WIDTH, window), lambda i: (layer, i, 0, 0))
        in_specs += [cache_block] * 2
        args += [k_t, v_t]
        out_specs += [cache_block] * 2
        out_shape += [jax.ShapeDtypeStruct((depth, b, SWA_KV_WIDTH, window), F32)] * 2
    in_specs += [pl.BlockSpec((t, 128), lambda i: (0, 0))] * 2
    args += [cos, sin]
    if caches is not None and prev_k is not None:
        in_specs += [pl.BlockSpec(memory_space=pl.ANY)] * 2
        aliases = {len(args): 1, len(args) + 1: 2}
        args += [prev_k, prev_v]
    return pl.pallas_call(
        functools.partial(_swa_step_kernel, bb=bb, t_new=t, window=window),
        grid=(b // bb,),
        in_specs=in_specs,
        out_specs=out_specs,
        out_shape=out_shape,
        input_output_aliases=aliases,
        compiler_params=_cparams(("parallel",)),
        name="swa_step",
    )(*args)


def _merge_kernel(oa_ref, ob_ref, oc_ref, ga_ref, gb_ref, gc_ref, x_ref, pa_ref, pb_ref, pc_ref, wo_ref, o_ref):
    def branch(o_r, g_r, p_r):
        return jax.nn.sigmoid(g_r[...].astype(F32)) * jnp.dot(o_r[...].astype(BF16), p_r[...], preferred_element_type=F32)

    m = branch(oa_ref, ga_ref, pa_ref) + branch(ob_ref, gb_ref, pb_ref) + branch(oc_ref, gc_ref, pc_ref)
    o_ref[...] = x_ref[...] + jnp.dot(m.astype(BF16), wo_ref[...], preferred_element_type=F32)


def _merge(oa, ob, oc, h, x, pa, pb, pc, wo):
    m = x.shape[0]
    tm = min(m, 512)
    assert m % tm == 0
    row =lambda w: pl.BlockSpec((tm, w), lambda i: (i, 0))
    gate = lambda c: pl.BlockSpec((tm, D_MODEL), lambda i: (i, c))
    full = lambda a: pl.BlockSpec(a.shape, lambda i: (0, 0))
    return pl.pallas_call(
        _merge_kernel,
        grid=(m // tm,),
        in_specs=[row(512), row(512), row(512), gate(0), gate(1), gate(2), row(D_MODEL),
                  full(pa), full(pb), full(pc), full(wo)],
        out_specs=row(D_MODEL),
        out_shape=jax.ShapeDtypeStruct((m, D_MODEL), F32),
        compiler_params=_cparams(("parallel",)),
        name="merge",
    )(oa, ob, oc, h, h, h, x, pa, pb, pc, wo)


def _mlp_kernel(x_ref, nw_ref, wu_ref, wd_ref, fnw_ref, o_ref, *, final):
    x = x_ref[...]
    xn = _rms(x, nw_ref[...]).astype(BF16)
    y = x
    for j in range(D_FF // MLP_TF):
        fs = slice(j * MLP_TF, (j + 1) * MLP_TF)
        a = jnp.maximum(jnp.dot(xn, wu_ref[:, fs], preferred_element_type=F32), 0.0)
        y = y + jnp.dot((a * a).astype(BF16), wd_ref[fs, :], preferred_element_type=F32)
    o_ref[...] = _rms(y, fnw_ref[...]) if final else y


def _mlp(x, nw, wu, wd, fnw, final):
    m = x.shape[0]
    tm = min(m, 512)
    assert m % tm == 0
    resident = lambda a: pl.BlockSpec(a.shape, lambda i: (0, 0), pipeline_mode=pl.Buffered(1))
    return pl.pallas_call(
        functools.partial(_mlp_kernel, final=final),
        grid=(m // tm,),
        in_specs=[
            pl.BlockSpec((tm, D_MODEL), lambda i: (i, 0)),
            pl.BlockSpec((1, D_MODEL), lambda i: (0, 0)),
            resident(wu),
            resident(wd),
            pl.BlockSpec((1, D_MODEL), lambda i: (0, 0)),
        ],
        out_specs=pl.BlockSpec((tm, D_MODEL), lambda i: (i, 0)),
        out_shape=jax.ShapeDtypeStruct((m, D_MODEL), F32),
        compiler_params=_cparams(("parallel",)),
        name="mlp",
    )(x, nw, wu, wd, fnw)


def _rope_tables(pos0, t):
    half = SWA_HD // 2
    inv = ROPE_THETA ** (-jnp.arange(half, dtype=F32) / half)
    ang = (pos0 + jnp.arange(t)).astype(F32)[:, None] * inv[None, :]
    cos = jnp.cos(ang)
    sin = jnp.sin(ang)
    return jnp.tile(cos, (1, 4)), jnp.tile(jnp.concatenate([-sin, sin], axis=1), (1, 2))


def _prep_layer(l, norm1_w, w_in, pool_w, pool_scale, dn_conv_w, dn_a_log, dn_dt_bias, dn_onorm_w, swa_sinks,
                proj_a, proj_b, proj_c, w_out, norm2_w, w_up, w_down):
    w = w_in[l]
    segments = ((3336, 6408), (512, 2048), (0, 512), (2048, 2560), (2568, 3080), (3080, 3336), (2560, 2568))
    w_perm = jnp.concatenate(
        [w[:, a:b].astype(BF16) for a, b in segments]
        + [jnp.zeros((D_MODEL, H_WIDTH - BA_OFF - 2 * DN_HEADS), BF16)], axis=1)
    lane_pad = lambda v: jnp.zeros((1, 128), F32).at[0, DN_HEADS:2 * DN_HEADS].set(v.astype(F32))
    row_pad = lambda v: jnp.zeros((2 * DN_HEADS, 128), F32).at[DN_HEADS:].set(
        jnp.broadcast_to(v.astype(F32)[:, None], (DN_HEADS, 128)))
    return dict(
        norm1=norm1_w[l][None].astype(F32), w_in=w_perm,
        pool_w=pool_w[l].astype(BF16), pool_scale=pool_scale[l][None].astype(F32),
        conv_w=dn_conv_w[l].astype(F32),
        gate=(lane_pad(dn_a_log[l]), lane_pad(dn_dt_bias[l]), row_pad(dn_a_log[l]), row_pad(dn_dt_bias[l])),
        onw=dn_onorm_w[l][None].astype(F32), sinks=swa_sinks[l].astype(F32),
        pa=proj_a[l].astype(BF16), pb=proj_b[l].astype(BF16), pc=proj_c[l].astype(BF16),
        wo=w_out[l].astype(BF16), norm2=norm2_w[l][None].astype(F32),
        wu=w_up[l].astype(BF16), wd=w_down[l].astype(BF16))


def _group_step(x3, p, fnw, final, pos0, pool_init, conv_init, s0, kind, swa_extra, pool_tiles, delta_tiles,
                swa_bb=1, s_stack=None, act=F32, fuse_conv=False):
    b, t, _ = x3.shape
    x2 = x3.reshape(b * t, D_MODEL)
    h, conv_tails = _inproj(x2, p['norm1'], p['w_in'], act,
                            conv=(conv_init, p['conv_w'], t) if fuse_conv else None)
    h3 = h.reshape(b, t, H_WIDTH)
    o_a = _pool(h3, pool_init, p['pool_w'], p['pool_scale'], pos0, *pool_tiles)
    o_b, s_new = _delta(h3, conv_init, s0[0], s0[1], p['conv_w'], p['gate'], p['onw'], *delta_tiles,
                        stack=s_stack, conv_done=fuse_conv)
    cos, sin = _rope_tables(pos0, t)
    if kind == 'prompt':
        o_c, k_rot = _swa_prompt(h3, swa_extra, cos, sin, p['sinks'], pos0)
    else:
        o_c, *k_rot = _swa_step(h3, swa_extra, cos, sin, p['sinks'], swa_bb)
    m = b * t
    h1 = _merge(o_a.reshape(m, 512), o_b.reshape(m, 512), o_c.reshape(m, 512), h, x2,
                p['pa'], p['pb'], p['pc'], p['wo'])
    out = _mlp(h1, p['norm2'], p['wu'], p['wd'], fnw, final)
    return out.reshape(b, t, D_MODEL), h3, s_new, k_rot, conv_tails


def kernel(x_prompt, x_sample, state_pool, state_conv, state_delta, cache_swa_k, cache_swa_v, meta_tokens, norm1_w, w_in, pool_w, pool_scale, dn_conv_w, dn_a_log, dn_dt_bias, dn_onorm_w, swa_sinks, proj_a, proj_b, proj_c, w_out, norm2_w, w_up, w_down, final_norm_w):
    depth = w_in.shape[0]
    bp, tp, _ = x_prompt.shape
    bs, ts, _ = x_sample.shape
    past = cache_swa_k.shape[2]
    fnw = final_norm_w[None].astype(F32)

    xm = meta_tokens[None].astype(F32)
    xp = x_prompt
    xs = x_sample
    outs = {k: [] for k in ('pool_p', 'conv_p', 'k_p', 'v_p', 'pool_s', 'conv_s')}
    delta_p = delta_s = None
    k_s = v_s = None
    to_t = lambda c: c.transpose(0, 1, 3, 4, 2).reshape(depth, bs, SWA_KV_WIDTH, past)
    cache_k_t, cache_v_t = to_t(cache_swa_k), to_t(cache_swa_v)
    for l in range(depth):
        p = _prep_layer(l, norm1_w, w_in, pool_w, pool_scale, dn_conv_w, dn_a_log, dn_dt_bias, dn_onorm_w,
                        swa_sinks, proj_a, proj_b, proj_c, w_out, norm2_w, w_up, w_down)
        final = l == depth - 1

        xm, hm, s_m, k_m, _ = _group_step(
            xm, p, fnw, final, 0,
            jnp.zeros((1, POOL_HALO, POOL_WIDTH), F32), jnp.zeros((1, CONV_HALO, CONV_CH), F32),
            (jnp.zeros((1, 1, DN_HEADS, DN_DK, DN_DV), F32), 0), 'step', None,
            (1, N_META), (1, N_META, N_META))

        meta_kv = jnp.zeros((SWA_WINDOW, 256), F32)
        meta_kv = meta_kv.at[SWA_WINDOW - N_META:, :SWA_KV_WIDTH].set(k_m[0][0])
        meta_kv = meta_kv.at[SWA_WINDOW - N_META:, SWA_KV_WIDTH:].set(hm[0, :, KVC_OFF + SWA_KV_WIDTH:KVC_OFF + 256])
        xp, hp, s_p, k_p, tails_p = _group_step(
            xp, p, fnw, final, N_META,
            jnp.broadcast_to(hm[:, :, U_OFF:U_OFF + POOL_WIDTH], (bp, POOL_HALO, POOL_WIDTH)),
            jnp.broadcast_to(hm[:, N_META - CONV_HALO:, QKV_OFF:QKV_OFF + CONV_CH], (bp, CONV_HALO, CONV_CH)),
            (jnp.broadcast_to(s_m[None], (1, bp, DN_HEADS, DN_DK, DN_DV)), 0), 'prompt', meta_kv,
            (1, 512), (4, 2 * DN_CHUNK, DN_CHUNK), s_stack=(l, depth, delta_p), act=BF16, fuse_conv=True)
        delta_p = s_p
        outs['pool_p'].append(hp[:, tp - POOL_BUF:, U_OFF:U_OFF + POOL_WIDTH].astype(F32))
        if tails_p is None:
            outs['conv_p'].append(hp[:, tp - (CONV_W - 1):, QKV_OFF:QKV_OFF + CONV_CH].astype(F32))
        else:
            outs['conv_p'].append(tails_p.reshape(bp, -1, CONV_HALO, CONV_CH)[:, -1, CONV_HALO - (CONV_W - 1):])
        outs['k_p'].append(k_p.reshape(bp, SWA_WINDOW, SWA_KV_HEADS, SWA_HD))
        outs['v_p'].append(hp[:, tp - SWA_WINDOW:, KVC_OFF + SWA_KV_WIDTH:KVC_OFF + 256].astype(F32)
                           .reshape(bp, SWA_WINDOW, SWA_KV_HEADS, SWA_HD))

        xs, hs, s_s, (k_s, v_s), _ = _group_step(
            xs, p, fnw, final, PAST_LEN,
            jnp.concatenate([jnp.zeros((bs, POOL_HALO - POOL_BUF, POOL_WIDTH), F32), state_pool[l]], axis=1),
            jnp.concatenate([jnp.zeros((bs, CONV_HALO - (CONV_W - 1), CONV_CH), F32), state_conv[l]], axis=1),
            (state_delta, l), 'step', (cache_k_t, cache_v_t, l, depth, k_s, v_s),
            (bs, ts), (16, ts, ts), 16, s_stack=(l, depth, delta_s))
        delta_s = s_s
        outs['pool_s'].append(jnp.concatenate([state_pool[l], hs[:, :, U_OFF:U_OFF + POOL_WIDTH]], axis=1)[:, -POOL_BUF:])
        outs['conv_s'].append(jnp.concatenate([state_conv[l], hs[:, :, QKV_OFF:QKV_OFF + CONV_CH]], axis=1)[:, -(CONV_W - 1):])

    st = {k: jnp.stack(v) for k, v in outs.items()}
    from_t = lambda c: c.reshape(depth, bs, SWA_KV_HEADS, SWA_HD, past).transpose(0, 1, 4, 2, 3)
    return (xp, xs, st['pool_p'], st['conv_p'], delta_p, st['k_p'], st['v_p'],
            st['pool_s'], st['conv_s'], delta_s, from_t(k_s), from_t(v_s))
```

```python
import functools
import math

import jax
import jax.numpy as jnp
from jax import lax
from jax.experimental import pallas as pl
from jax.experimental.pallas import tpu as pltpu

F32 = jnp.float32
BF16 = jnp.bfloat16

D_MODEL = 1024
N_META = 16
EPS = 1e-6
POOL_GROUPS = 4
POOL_GROUP_DIM = 128
POOL_WIDTH = 512
POOL_WINDOWS = (2, 4, 8, 16)
POOL_BUF = 15
POOL_HALO = 16
DN_HEADS = 4
DN_DK = 128
DN_DV = 128
DN_QK = 512
DN_VW = 512
CONV_W = 4
CONV_CH = 1536
CONV_HALO = 8
DN_CHUNK = 64
GROUP_ROWS = 128
SWA_HEADS = 8
SWA_KV_HEADS = 2
SWA_GROUP = 4
SWA_HD = 64
SWA_WIDTH = 512
SWA_KV_WIDTH = 128
SWA_WINDOW = 128
ROPE_THETA = 10000.0
D_FF = 4096
MLP_TF = 1024
PAST_LEN = 16384

G_OFF = 0
QKV_OFF = 3072
U_OFF = 4608
Z_OFF = 5120
QC_OFF = 5632
KVC_OFF = 6144
BA_OFF = 6400
H_WIDTH = 6528
IN_CHUNKS = ((0, 1536), (1536, 3072), (3072, 4608), (4608, 6144), (6144, 6528))

VMEM_LIMIT = 56 * 1024 * 1024
NEG_BIG = -1e30


def _cparams(sem):
    return pltpu.CompilerParams(dimension_semantics=sem, vmem_limit_bytes=VMEM_LIMIT)


def _rms(x, w):
    return x * lax.rsqrt(jnp.mean(x * x, axis=-1, keepdims=True) + EPS) * w


def _l2n(v):
    return v * lax.rsqrt(jnp.sum(v * v, axis=-1, keepdims=True) + EPS)


def _short_conv(ext_ref, cw_ref, rows):
    base = CONV_HALO - (CONV_W - 1)
    y = ext_ref[..., base:base + rows, :] * cw_ref[0:1, :]
    for j in range(1, CONV_W):
        y = y + ext_ref[..., base + j:base + j + rows, :] * cw_ref[j:j + 1, :]
    return y * jax.nn.sigmoid(y)


def _qkv_heads(y):
    heads = range(DN_HEADS)
    qn = [_l2n(y[:, h * DN_DK:(h + 1) * DN_DK]) * (DN_DK ** -0.5) for h in heads]
    kn = [_l2n(y[:, DN_QK + h * DN_DK:DN_QK + (h + 1) * DN_DK]) for h in heads]
    vh = [y[:, 2 * DN_QK + h * DN_DV:2 * DN_QK + (h + 1) * DN_DV] for h in heads]
    return qn, kn, vh


def _inproj_kernel(x_ref, nw_ref, w_ref, *rest, tiles_per_seq):
    if tiles_per_seq:
        cinit_ref, cw_ref, o_ref, tail_ref, ext_ref = rest
    else:
        (o_ref,) = rest
    tm = x_ref.shape[0]
    if tiles_per_seq:
        first = pl.program_id(0) % tiles_per_seq == 0

        @pl.when(first)
        def _():
            ext_ref[0:CONV_HALO, :] = cinit_ref[0]

        @pl.when(jnp.logical_not(first))
        def _():
            ext_ref[0:CONV_HALO, :] = ext_ref[tm:tm + CONV_HALO, :]

    xn = _rms(x_ref[...], nw_ref[...]).astype(BF16)
    for a, b in sorted(IN_CHUNKS, key=lambda ab: ab[0] != QKV_OFF):
        acc = jnp.dot(xn, w_ref[:, a:b], preferred_element_type=F32)
        if tiles_per_seq and a == QKV_OFF:
            ext_ref[CONV_HALO:CONV_HALO + tm, :] = acc
            tail_ref[0] = acc[tm - CONV_HALO:, :]
            qn, kn, vh = _qkv_heads(_short_conv(ext_ref, cw_ref, tm))
            acc = jnp.concatenate(qn + kn + vh, axis=1)
        o_ref[:, a:b] = acc.astype(o_ref.dtype)


def _inproj(x, nw, w, act, conv=None):
    m = x.shape[0]
    tm = min(m, 512 if act == BF16 else 256)
    assert m % tm == 0
    in_specs = [
        pl.BlockSpec((tm, D_MODEL), lambda i: (i, 0)),
        pl.BlockSpec((1, D_MODEL), lambda i: (0, 0)),
        pl.BlockSpec((D_MODEL, H_WIDTH), lambda i: (0, 0), pipeline_mode=pl.Buffered(1)),
    ]
    out_specs = [pl.BlockSpec((tm, H_WIDTH), lambda i: (i, 0))]
    out_shape = [jax.ShapeDtypeStruct((m, H_WIDTH), act)]
    args = [x, nw, w]
    scratch = []
    tiles_per_seq = 0
    if conv is not None:
        cinit, cw, seq_rows = conv
        assert seq_rows % tm == 0
        tiles_per_seq = seq_rows // tm
        in_specs += [pl.BlockSpec((1, CONV_HALO, CONV_CH), lambda i: (i // tiles_per_seq, 0, 0)),
                     pl.BlockSpec((CONV_W, CONV_CH), lambda i: (0, 0))]
        args += [cinit, cw]
        out_specs.append(pl.BlockSpec((1, CONV_HALO, CONV_CH), lambda i: (i, 0, 0)))
        out_shape.append(jax.ShapeDtypeStruct((m // tm, CONV_HALO, CONV_CH), F32))
        scratch = [pltpu.VMEM((CONV_HALO + tm, CONV_CH), F32)]
    res = pl.pallas_call(
        functools.partial(_inproj_kernel, tiles_per_seq=tiles_per_seq),
        grid=(m // tm,),
        in_specs=in_specs,
        out_specs=out_specs,
        out_shape=out_shape,
        scratch_shapes=scratch,
        compiler_params=_cparams(("arbitrary",) if conv is not None else ("parallel",)),
        name="inproj",
    )(*args)
    return (res[0], res[1]) if conv is not None else (res[0], None)


def _pool_kernel(u_ref, init_ref, pw_ref, ps_ref, o_ref, ext_ref, *, pos0, tt):
    t = pl.program_id(1)
    bb = u_ref.shape[0]

    @pl.when(t == 0)
    def _():
        ext_ref[:, 0:POOL_HALO, :] = init_ref[...]

    @pl.when(t > 0)
    def _():
        ext_ref[:, 0:POOL_HALO, :] = ext_ref[:, tt:tt + POOL_HALO, :]

    ext_ref[:, POOL_HALO:POOL_HALO + tt, :] = u_ref[...].astype(F32)

    pos = pos0 + t * tt + lax.broadcasted_iota(jnp.int32, (1, tt, 1), 1)
    for g, w in enumerate(POOL_WINDOWS):
        cs = slice(g * POOL_GROUP_DIM, (g + 1) * POOL_GROUP_DIM)
        s = ext_ref[:, :, cs]
        x = s[:, POOL_HALO:POOL_HALO + tt]
        k = 1
        while k < w:
            s = s + pltpu.roll(s, k, 1)
            k *= 2
        s = s[:, POOL_HALO:POOL_HALO + tt]
        cnt = jnp.minimum(w, pos + 1).astype(F32)
        d = (s / cnt - x).reshape(bb * tt, POOL_GROUP_DIM)
        y = jnp.dot(d.astype(BF16), pw_ref[g], preferred_element_type=F32)
        o_ref[:, :, cs] = (y * ps_ref[:, cs]).reshape(bb, tt, POOL_GROUP_DIM).astype(o_ref.dtype)


def _pool(h3, init, pw, ps, pos0, bb, tt):
    b, t, _ = h3.shape
    return pl.pallas_call(
        functools.partial(_pool_kernel, pos0=pos0, tt=tt),
        grid=(b // bb, t // tt),
        in_specs=[
            pl.BlockSpec((bb, tt, POOL_WIDTH), lambda i, j: (i, j, U_OFF // POOL_WIDTH)),
            pl.BlockSpec((bb, POOL_HALO, POOL_WIDTH), lambda i, j: (i, 0, 0)),
            pl.BlockSpec((POOL_GROUPS, POOL_GROUP_DIM, POOL_GROUP_DIM), lambda i, j: (0, 0, 0)),
            pl.BlockSpec((1, POOL_WIDTH), lambda i, j: (0, 0)),
        ],
        out_specs=pl.BlockSpec((bb, tt, POOL_WIDTH), lambda i, j: (i, j, 0)),
        out_shape=jax.ShapeDtypeStruct((b, t, POOL_WIDTH), h3.dtype),
        scratch_shapes=[pltpu.VMEM((bb, POOL_HALO + tt, POOL_WIDTH), F32)],
        compiler_params=_cparams(("parallel", "arbitrary")),
        name="pool",
    )(h3, init, pw, ps)


def _softplus(x):
    return jnp.maximum(x, 0.0) + jnp.log1p(jnp.exp(-jnp.abs(x)))


def _dot_nt(a, b, **kw):
    return lax.dot_general(a, b, (((1,), (1,)), ((), ())), preferred_element_type=F32, **kw)


def _dot_tn(a, b):
    return lax.dot_general(a, b, (((0,), (0,)), ((), ())), preferred_element_type=F32)


def _dot(a, b):
    return jnp.dot(a, b, preferred_element_type=F32)


def _delta_kernel(qkv_ref, z_ref, ba_ref, cinit_ref, s0_ref, cw_ref, alog_ref, dtb_ref, alogt_ref, dtbt_ref,
                  onw_ref, *rest, bb, tb, chunk, conv_done):
    o_ref, s_ref, ext_ref = rest[-3:]
    t = pl.program_id(1)
    rows = bb * tb
    cpb = tb // chunk
    heads = list(range(DN_HEADS))
    mm = BF16 if rows >= 16 else F32
    mc = BF16 if chunk >= 16 else F32

    @pl.when(t == 0)
    def _():
        s_ref[...] = s0_ref[...]

    if conv_done:
        y = qkv_ref[...].astype(F32).reshape(rows, CONV_CH)
        qn = [y[:, h * DN_DK:(h + 1) * DN_DK] for h in heads]
        kn = [y[:, DN_QK + h * DN_DK:DN_QK + (h + 1) * DN_DK] for h in heads]
        vh = [y[:, 2 * DN_QK + h * DN_DV:2 * DN_QK + (h + 1) * DN_DV] for h in heads]
    else:
        @pl.when(t == 0)
        def _():
            ext_ref[:, 0:CONV_HALO, :] = cinit_ref[...]

        @pl.when(t > 0)
        def _():
            ext_ref[:, 0:CONV_HALO, :] = ext_ref[:, tb:tb + CONV_HALO, :]

        ext_ref[:, CONV_HALO:CONV_HALO + tb, :] = qkv_ref[...].astype(F32)
        qn, kn, vh = _qkv_heads(_short_conv(ext_ref, cw_ref, tb).reshape(rows, CONV_CH))

    gr = min(rows, GROUP_ROWS)
    groups = rows // gr
    spg = gr // tb
    keys = [(gi, h) for gi in range(groups) for h in heads]

    def part(v, gi):
        return v[gi * gr:(gi + 1) * gr]

    bav = ba_ref[...].astype(F32).reshape(rows, 128)
    zv = z_ref[...].astype(F32).reshape(rows, DN_VW)

    row = lax.broadcasted_iota(jnp.int32, (gr, gr), 0)
    col = lax.broadcasted_iota(jnp.int32, (gr, gr), 1)
    shift = int(math.log2(chunk))
    same = (row >> shift) == (col >> shift)
    tri = same & (row >= col)
    strict = same & (row > col)
    lane = lax.broadcasted_iota(jnp.int32, (gr, 128), 1)

    gcol, bcol, grow = {}, {}, {}
    for gi in range(groups):
        bav_g = part(bav, gi)
        if gr == 128:
            sub = bav_g.T[0:2 * DN_HEADS, :]
            g_t = -jnp.exp(alogt_ref[...]) * _softplus(sub + dtbt_ref[...])
            gcum_t = jnp.dot(g_t, (same & (col >= row)).astype(F32), preferred_element_type=F32,
                             precision=lax.Precision.HIGHEST)
            live = lax.broadcasted_iota(jnp.int32, sub.shape, 0) < DN_HEADS
            packed = jnp.where(live, jax.nn.sigmoid(sub), gcum_t)
            cols = jnp.concatenate([packed, jnp.zeros((gr - 2 * DN_HEADS, gr), F32)], axis=0).T
            beta_full = gcum = cols
            for h in heads:
                grow[gi, h] = gcum_t[DN_HEADS + h:DN_HEADS + h + 1, :]
        else:
            beta_full = jax.nn.sigmoid(bav_g)
            g_full = -jnp.exp(alog_ref[...]) * _softplus(bav_g + dtb_ref[...])
            gcum = jnp.dot(tri.astype(F32), g_full, preferred_element_type=F32, precision=lax.Precision.HIGHEST)
            ones = jnp.ones((gr, 128), F32)
            for h in heads:
                grow[gi, h] = _dot_nt(ones, jnp.where(lane == DN_HEADS + h, gcum, 0.0),
                                      precision=lax.Precision.HIGHEST)
        for h in heads:
            gcol[gi, h] = jnp.sum(jnp.where(lane == DN_HEADS + h, gcum, 0.0), axis=1, keepdims=True)
            bcol[gi, h] = jnp.sum(jnp.where(lane == h, beta_full, 0.0), axis=1, keepdims=True)
    eg = {k: jnp.exp(gcol[k]) for k in keys}
    qs = {(gi, h): part(qn[h], gi) for gi, h in keys}
    ks = {(gi, h): part(kn[h], gi) for gi, h in keys}
    vs = {(gi, h): part(vh[h], gi) for gi, h in keys}

    bshift = min(shift, 4)
    decay = {k: jnp.where(tri, jnp.exp(jnp.where(tri, gcol[k] - grow[k], 0.0)), 0.0) for k in keys}
    kb = {k: ks[k].astype(mm) for k in keys}
    kk = {k: _dot_nt(kb[k], kb[k]) for k in keys}
    qk = {k: _dot_nt(qs[k].astype(mm), kb[k]) for k in keys}
    m = {k: jnp.where(strict, kk[k] * decay[k] * bcol[k], 0.0) for k in keys}
    p = {k: jnp.where((row >> bshift) == (col >> bshift), -m[k], 0.0) for k in keys}
    xp = p
    for _ in range(bshift - 1):
        xb = {k: xp[k].astype(mm) for k in keys}
        xp = {k: _dot(xb[k], xb[k]) for k in keys}
        p = {k: p[k] + xp[k] + _dot(p[k].astype(mm), xp[k].astype(mm)) for k in keys}
    for s in range(bshift, shift):
        lower = ((row >> (s + 1)) == (col >> (s + 1))) & ((row >> s) > (col >> s))
        c_blk = {k: jnp.where(lower, m[k], 0.0) for k in keys}
        pb = {k: p[k].astype(mm) for k in keys}
        a = {k: c_blk[k] + _dot(pb[k], c_blk[k].astype(mm)) for k in keys}
        p = {k: p[k] - a[k] - _dot(a[k].astype(mm), pb[k]) for k in keys}
    rhs = {k: jnp.concatenate([vs[k] * bcol[k], ks[k] * (bcol[k] * eg[k])], axis=1) for k in keys}
    uw = {k: rhs[k] + _dot(p[k].astype(mm), rhs[k].astype(mm)) for k in keys}
    qkd = {k: (qk[k] * decay[k]).astype(mm) for k in keys}
    qg = {k: qs[k] * eg[k] for k in keys}

    pairs = [(b, h) for b in range(bb) for h in heads]
    state = {bh: s_ref[bh[0], bh[1]] for bh in pairs}
    dlt = {k: [] for k in keys}
    oq = {k: [] for k in keys}
    for c in range(cpb):
        def rs(b):
            start = (b % spg) * tb + c * chunk
            return slice(start, start + chunk)
        prod = {}
        for b, h in pairs:
            k = (b // spg, h)
            lhs = jnp.concatenate([uw[k][rs(b), DN_DV:], qg[k][rs(b)]], axis=0).astype(mc)
            prod[b, h] = _dot(lhs, state[b, h].astype(mc))
        for b, h in pairs:
            k = (b // spg, h)
            r = rs(b)
            d = uw[k][r, :DN_DV] - prod[b, h][:chunk]
            glast = gcol[k][r.stop - 1:r.stop, :]
            kg = ks[k][r] * jnp.exp(glast - gcol[k][r])
            state[b, h] = state[b, h] * jnp.exp(glast) + _dot_tn(kg.astype(mc), d.astype(mc))
            dlt[k].append((r.start, d))
            oq[k].append((r.start, prod[b, h][chunk:]))
    for b, h in pairs:
        s_ref[b, h] = state[b, h]

    def stack(parts):
        parts = [v for _, v in sorted(parts, key=lambda sv: sv[0])]
        return parts[0] if len(parts) == 1 else jnp.concatenate(parts, axis=0)

    for gi, h in keys:
        hs = slice(h * DN_DV, (h + 1) * DN_DV)
        o = stack(oq[gi, h]) + _dot(qkd[gi, h], stack(dlt[gi, h]).astype(mm))
        zh = part(zv, gi)[:, hs]
        gated = _rms(o, onw_ref[...]) * (zh * jax.nn.sigmoid(zh))
        o_ref[gi * spg:(gi + 1) * spg, :, hs] = gated.reshape(spg, tb, DN_DV).astype(o_ref.dtype)


def _delta(h3, cinit, s0, s0_layer, cw, gate, onw, bb, tb, chunk, stack=None, conv_done=False):
    b, t, _ = h3.shape
    assert b % bb == 0 and t % tb == 0 and tb % chunk == 0
    state_block = (None, bb, DN_HEADS, DN_DK, DN_DV)
    in_specs = [
        pl.BlockSpec((bb, tb, CONV_CH), lambda i, j: (i, j, QKV_OFF // CONV_CH)),
        pl.BlockSpec((bb, tb, DN_VW), lambda i, j: (i, j, Z_OFF // DN_VW)),
        pl.BlockSpec((bb, tb, 128), lambda i, j: (i, j, BA_OFF // 128)),
        pl.BlockSpec((bb, CONV_HALO, CONV_CH), lambda i, j: (i, 0, 0)),
        pl.BlockSpec(state_block, lambda i, j: (s0_layer, i, 0, 0, 0)),
        pl.BlockSpec((CONV_W, CONV_CH), lambda i, j: (0, 0)),
        pl.BlockSpec((1, 128), lambda i, j: (0, 0)),
        pl.BlockSpec((1, 128), lambda i, j: (0, 0)),
        pl.BlockSpec((2 * DN_HEADS, 128), lambda i, j: (0, 0)),
        pl.BlockSpec((2 * DN_HEADS, 128), lambda i, j: (0, 0)),
        pl.BlockSpec((1, DN_DV), lambda i, j: (0, 0)),
    ]
    args = [h3, h3, h3, cinit, s0, cw, *gate, onw]
    layer, depth, prev = (0, 1, None) if stack is None else stack
    aliases = {}
    if prev is not None:
        in_specs.append(pl.BlockSpec(memory_space=pl.ANY))
        aliases = {len(args): 1}
        args.append(prev)
    o, s = pl.pallas_call(
        functools.partial(_delta_kernel, bb=bb, tb=tb, chunk=chunk, conv_done=conv_done),
        grid=(b // bb, t // tb),
        in_specs=in_specs,
        out_specs=[
            pl.BlockSpec((bb, tb, DN_VW), lambda i, j: (i, j, 0)),
            pl.BlockSpec(state_block, lambda i, j: (layer, i, 0, 0, 0)),
        ],
        out_shape=[
            jax.ShapeDtypeStruct((b, t, DN_VW), h3.dtype),
            jax.ShapeDtypeStruct((depth, b, DN_HEADS, DN_DK, DN_DV), F32),
        ],
        scratch_shapes=[pltpu.VMEM((bb, CONV_HALO + tb, CONV_CH), F32)],
        input_output_aliases=aliases,
        compiler_params=_cparams(("parallel", "arbitrary")),
        name="delta",
    )(*args)
    return o, (s[0] if stack is None else s)


def _rope(x, cos, sin):
    width = x.shape[-1]
    reps = width // cos.shape[-1]
    if reps > 1:
        cos = jnp.concatenate([cos] * reps, axis=1)
        sin = jnp.concatenate([sin] * reps, axis=1)
    lane = lax.broadcasted_iota(jnp.int32, x.shape, 1)
    first_half = (lane & (SWA_HD - 1)) < (SWA_HD // 2)
    other = jnp.where(first_half, pltpu.roll(x, width - SWA_HD // 2, 1), pltpu.roll(x, SWA_HD // 2, 1))
    return x * cos + other * sin


def _pad_heads(x, lo, fill):
    xr = pltpu.roll(x, SWA_HD, 1)
    z = jnp.full_like(x, fill)
    return [jnp.where(lo, x, z), jnp.where(lo, z, xr), jnp.where(lo, xr, z), jnp.where(lo, z, x)]


def _swa_prompt_kernel(sinks_ref, q_ref, kv_ref, meta_ref, cos_ref, sin_ref, o_ref, klast_ref, kcat_ref, vcat_ref,
                       *, pos0):
    n = pl.program_id(1)
    bb = q_ref.shape[0]
    blk = SWA_WINDOW
    lo = lax.broadcasted_iota(jnp.int32, (blk, 128), 1) < SWA_HD

    def put(b, half, k, v):
        rows = slice(half * blk, (half + 1) * blk)
        for i, (a, c) in enumerate(zip(_pad_heads(k, lo, 0.0), _pad_heads(v, lo, 1.0))):
            kcat_ref[b, i, rows, :] = a.astype(BF16)
            vcat_ref[b, i, rows, :] = c.astype(BF16)

    @pl.when(n == 0)
    def _():
        for b in range(bb):
            put(b, 0, meta_ref[:, :SWA_KV_WIDTH], meta_ref[:, SWA_KV_WIDTH:])

    @pl.when(n > 0)
    def _():
        kcat_ref[:, :, 0:blk, :] = kcat_ref[:, :, blk:2 * blk, :]
        vcat_ref[:, :, 0:blk, :] = vcat_ref[:, :, blk:2 * blk, :]

    cos = cos_ref[...]
    sin = sin_ref[...]
    qt = {}
    for b in range(bb):
        q = _rope(q_ref[b].astype(F32), cos, sin) * (SWA_HD ** -0.5)
        kv = kv_ref[b].astype(F32)
        k = _rope(kv[:, :SWA_KV_WIDTH], cos, sin)
        klast_ref[b] = k
        put(b, 1, k, kv[:, SWA_KV_WIDTH:])
        for i in range(SWA_HEADS // 2):
            qt[b, i] = q[:, i * 128:(i + 1) * 128].astype(BF16)

    r = lax.broadcasted_iota(jnp.int32, (blk, 2 * blk), 0)
    j = lax.broadcasted_iota(jnp.int32, (blk, 2 * blk), 1)
    jmin = jnp.maximum(blk - pos0 - n * blk, 0)
    mask = ((j < blk) & (j > r) & (j >= jmin)) | ((j >= blk) & ((j - blk) <= r))
    keys = [(b, h) for b in range(bb) for h in range(SWA_HEADS)]
    s = {(b, h): jnp.where(mask, _dot_nt(qt[b, h // 2], kcat_ref[b, 2 * (h // SWA_GROUP) + h % 2]), NEG_BIG)
         for b, h in keys}
    m = {(b, h): jnp.maximum(jnp.max(s[b, h], axis=1, keepdims=True), sinks_ref[h]) for b, h in keys}
    e = {k: jnp.exp(s[k] - m[k]).astype(BF16) for k in keys}
    sink_w = {(b, h): jnp.exp(sinks_ref[h] - m[b, h]) for b, h in keys}
    for b in range(bb):
        for i in range(SWA_HEADS // 2):
            he, ho = 2 * i, 2 * i + 1
            g = he // SWA_GROUP
            a_e = _dot(e[b, he], vcat_ref[b, 2 * g])
            a_o = _dot(e[b, ho], vcat_ref[b, 2 * g + 1])
            den = (jnp.where(lo, pltpu.roll(a_e, SWA_HD, 1), pltpu.roll(a_o, SWA_HD, 1))
                   + jnp.where(lo, sink_w[b, he], sink_w[b, ho]))
            o_ref[b, :, i * 128:(i + 1) * 128] = (jnp.where(lo, a_e, a_o) / den).astype(o_ref.dtype)


def _swa_prompt(h3, meta_kv, cos, sin, sinks, pos0, bb):
    b, t, _ = h3.shape
    blk = SWA_WINDOW
    assert t % blk == 0 and b % bb == 0
    return pl.pallas_call(
        functools.partial(_swa_prompt_kernel, pos0=pos0),
        grid=(b // bb, t // blk),
        in_specs=[
            pl.BlockSpec(memory_space=pltpu.SMEM),
            pl.BlockSpec((bb, blk, SWA_WIDTH), lambda i, n: (i, n, QC_OFF // SWA_WIDTH)),
            pl.BlockSpec((bb, blk, 256), lambda i, n: (i, n, KVC_OFF // 256)),
            pl.BlockSpec((blk, 256), lambda i, n: (0, 0)),
            pl.BlockSpec((blk, 128), lambda i, n: (n, 0)),
            pl.BlockSpec((blk, 128), lambda i, n: (n, 0)),
        ],
        out_specs=[
            pl.BlockSpec((bb, blk, SWA_WIDTH), lambda i, n: (i, n, 0)),
            pl.BlockSpec((bb, blk, SWA_KV_WIDTH), lambda i, n: (i, 0, 0)),
        ],
        out_shape=[
            jax.ShapeDtypeStruct((b, t, SWA_WIDTH), h3.dtype),
            jax.ShapeDtypeStruct((b, blk, SWA_KV_WIDTH), F32),
        ],
        scratch_shapes=[pltpu.VMEM((bb, 4, 2 * blk, 128), BF16), pltpu.VMEM((bb, 4, 2 * blk, 128), BF16)],
        compiler_params=_cparams(("parallel", "arbitrary")),
        name="swa_prompt",
    )(sinks, h3, h3, meta_kv, cos, sin)


def _swa_step_kernel(sink_ref, q_ref, kv_ref, *rest, bb, t_new, window):
    if window:
        kc_ref, vc_ref, cos_ref, sin_ref = rest[:4]
        o_ref, kout_ref, vout_ref = rest[-3:]
    else:
        cos_ref, sin_ref, o_ref, knew_ref = rest
    cos = cos_ref[...]
    sin = sin_ref[...]
    rows = SWA_HEADS * t_new
    mn = BF16 if t_new >= 16 else F32
    lo = lax.broadcasted_iota(jnp.int32, (t_new, 128), 1) < SWA_HD
    tq = lax.broadcasted_iota(jnp.int32, (rows, t_new), 0) & (t_new - 1)
    mask_n = lax.broadcasted_iota(jnp.int32, (rows, t_new), 1) <= tq
    if window:
        diff = ((lax.broadcasted_iota(jnp.int32, (rows, window), 0) & (t_new - 1)) + window
                - lax.broadcasted_iota(jnp.int32, (rows, window), 1))
        mask_c = (diff >= 0) & (diff < SWA_WINDOW)
    sink = sink_ref[...]
    seqs = range(bb)

    qall, kk, vv = [], [], []
    for s in seqs:
        q = _rope(q_ref[s].astype(F32), cos, sin) * (SWA_HD ** -0.5)
        kv = kv_ref[s].astype(F32)
        k = _rope(kv[:, :SWA_KV_WIDTH], cos, sin)
        if not window:
            knew_ref[s] = k
        pieces = []
        for h in range(SWA_HEADS):
            g = h // SWA_GROUP
            tile = q[:, (h // 2) * 128:(h // 2 + 1) * 128]
            if h % 2 != g:
                tile = pltpu.roll(tile, SWA_HD, 1)
            pieces.append(jnp.where(lo, tile, 0.0) if g == 0 else jnp.where(lo, 0.0, tile))
        qall.append(jnp.concatenate(pieces, axis=0))
        kk.append(k)
        vv.append(kv[:, SWA_KV_WIDTH:])

    s_n = [jnp.where(mask_n, _dot_nt(qall[s].astype(mn), kk[s].astype(mn)), NEG_BIG) for s in seqs]
    m = [jnp.maximum(jnp.max(s_n[s], axis=1, keepdims=True), sink) for s in seqs]
    if window:
        s_c = [jnp.where(mask_c, _dot(qall[s].astype(BF16), kc_ref[s].astype(BF16)), NEG_BIG) for s in seqs]
        m = [jnp.maximum(m[s], jnp.max(s_c[s], axis=1, keepdims=True)) for s in seqs]
    e_n = [jnp.exp(s_n[s] - m[s]) for s in seqs]
    den = [jnp.sum(e_n[s], axis=1, keepdims=True) + jnp.exp(sink - m[s]) for s in seqs]
    acc = [_dot(e_n[s].astype(mn), vv[s].astype(mn)) for s in seqs]
    if window:
        e_c = [jnp.exp(s_c[s] - m[s]) for s in seqs]
        den = [den[s] + jnp.sum(e_c[s], axis=1, keepdims=True) for s in seqs]
        acc = [acc[s] + _dot_nt(e_c[s].astype(BF16), vc_ref[s].astype(BF16)) for s in seqs]

        def slide(old_t, new):
            kept = pltpu.roll(old_t, window - t_new, 1)
            fresh = jnp.concatenate([jnp.zeros((window - t_new, SWA_KV_WIDTH), F32), new], axis=0).T
            newest = lax.broadcasted_iota(jnp.int32, kept.shape, 1) >= window - t_new
            return jnp.where(newest, fresh, kept)

        for s in seqs:
            kout_ref[s] = slide(kc_ref[s], kk[s])
            vout_ref[s] = slide(vc_ref[s], vv[s])
    for s in seqs:
        a = acc[s] / den[s]
        for i in range(SWA_HEADS // 2):
            he, ho = 2 * i, 2 * i + 1
            g = he // SWA_GROUP
            a_e = a[he * t_new:(he + 1) * t_new]
            a_o = a[ho * t_new:(ho + 1) * t_new]
            if g == 1:
                a_e = pltpu.roll(a_e, SWA_HD, 1)
            else:
                a_o = pltpu.roll(a_o, SWA_HD, 1)
            o_ref[s, :, i * 128:(i + 1) * 128] = jnp.where(lo, a_e, a_o).astype(o_ref.dtype)


def _swa_step(h3, caches, cos, sin, sinks, bb):
    b, t, _ = h3.shape
    assert b % bb == 0
    sink_rows = jnp.repeat(sinks, t)[:, None]
    in_specs = [
        pl.BlockSpec((SWA_HEADS * t, 1), lambda i: (0, 0)),
        pl.BlockSpec((bb, t, SWA_WIDTH), lambda i: (i, 0, QC_OFF // SWA_WIDTH)),
        pl.BlockSpec((bb, t, 256), lambda i: (i, 0, KVC_OFF // 256)),
    ]
    args = [sink_rows, h3, h3]
    out_specs = [pl.BlockSpec((bb, t, SWA_WIDTH), lambda i: (i, 0, 0))]
    out_shape = [jax.ShapeDtypeStruct((b, t, SWA_WIDTH), h3.dtype)]
    window = 0
    aliases = {}
    if caches is None:
        out_specs.append(pl.BlockSpec((bb, t, SWA_KV_WIDTH), lambda i: (i, 0, 0)))
        out_shape.append(jax.ShapeDtypeStruct((b, t, SWA_KV_WIDTH), F32))
    else:
        k_t, v_t, layer, depth, prev_k, prev_v = caches
        window = k_t.shape[-1]
        assert window == SWA_KV_WIDTH and t <= window
        cache_block = pl.BlockSpec((None, bb, SWA_KV_WIDTH, window), lambda i: (layer, i, 0, 0))
        in_specs += [cache_block] * 2
        args += [k_t, v_t]
        out_specs += [cache_block] * 2
        out_shape += [jax.ShapeDtypeStruct((depth, b, SWA_KV_WIDTH, window), F32)] * 2
    in_specs += [pl.BlockSpec((t, 128), lambda i: (0, 0))] * 2
    args += [cos, sin]
    if caches is not None and prev_k is not None:
        in_specs += [pl.BlockSpec(memory_space=pl.ANY)] * 2
        aliases = {len(args): 1, len(args) + 1: 2}
        args += [prev_k, prev_v]
    return pl.pallas_call(
        functools.partial(_swa_step_kernel, bb=bb, t_new=t, window=window),
        grid=(b // bb,),
        in_specs=in_specs,
        out_specs=out_specs,
        out_shape=out_shape,
        input_output_aliases=aliases,
        compiler_params=_cparams(("parallel",)),
        name="swa_step",
    )(*args)


def _merge_kernel(oa_ref, ob_ref, oc_ref, ga_ref, gb_ref, gc_ref, x_ref, pa_ref, pb_ref, pc_ref, wo_ref, o_ref):
    def branch(o_r, g_r, p_r):
        return jax.nn.sigmoid(g_r[...].astype(F32)) * jnp.dot(o_r[...].astype(BF16), p_r[...], preferred_element_type=F32)

    m = branch(oa_ref, ga_ref, pa_ref) + branch(ob_ref, gb_ref, pb_ref) + branch(oc_ref, gc_ref, pc_ref)
    o_ref[...] = x_ref[...] + jnp.dot(m.astype(BF16), wo_ref[...], preferred_element_type=F32)


def _merge(oa, ob, oc, h, x, pa, pb, pc, wo):
    m = x.shape[0]
    tm = min(m, 512)
    assert m % tm == 0
    row =lambda w: pl.BlockSpec((tm, w), lambda i: (i, 0))
    gate = lambda c: pl.BlockSpec((tm, D_MODEL), lambda i: (i, c))
    full = lambda a: pl.BlockSpec(a.shape, lambda i: (0, 0))
    return pl.pallas_call(
        _merge_kernel,
        grid=(m // tm,),
        in_specs=[row(512), row(512), row(512), gate(0), gate(1), gate(2), row(D_MODEL),
                  full(pa), full(pb), full(pc), full(wo)],
        out_specs=row(D_MODEL),
        out_shape=jax.ShapeDtypeStruct((m, D_MODEL), F32),
        compiler_params=_cparams(("parallel",)),
        name="merge",
    )(oa, ob, oc, h, h, h, x, pa, pb, pc, wo)


def _mlp_kernel(x_ref, nw_ref, wu_ref, wd_ref, fnw_ref, o_ref, *, final):
    x = x_ref[...]
    xn = _rms(x, nw_ref[...]).astype(BF16)
    y = x
    for j in range(D_FF // MLP_TF):
        fs = slice(j * MLP_TF, (j + 1) * MLP_TF)
        a = jnp.maximum(jnp.dot(xn, wu_ref[:, fs], preferred_element_type=F32), 0.0)
        y = y + jnp.dot((a * a).astype(BF16), wd_ref[fs, :], preferred_element_type=F32)
    o_ref[...] = _rms(y, fnw_ref[...]) if final else y


def _mlp(x, nw, wu, wd, fnw, final):
    m = x.shape[0]
    tm = min(m, 512)
    assert m % tm == 0
    resident = lambda a: pl.BlockSpec(a.shape, lambda i: (0, 0), pipeline_mode=pl.Buffered(1))
    return pl.pallas_call(
        functools.partial(_mlp_kernel, final=final),
        grid=(m // tm,),
        in_specs=[
            pl.BlockSpec((tm, D_MODEL), lambda i: (i, 0)),
            pl.BlockSpec((1, D_MODEL), lambda i: (0, 0)),
            resident(wu),
            resident(wd),
            pl.BlockSpec((1, D_MODEL), lambda i: (0, 0)),
        ],
        out_specs=pl.BlockSpec((tm, D_MODEL), lambda i: (i, 0)),
        out_shape=jax.ShapeDtypeStruct((m, D_MODEL), F32),
        compiler_params=_cparams(("parallel",)),
        name="mlp",
    )(x, nw, wu, wd, fnw)


def _rope_tables(pos0, t):
    half = SWA_HD // 2
    inv = ROPE_THETA ** (-jnp.arange(half, dtype=F32) / half)
    ang = (pos0 + jnp.arange(t)).astype(F32)[:, None] * inv[None, :]
    cos = jnp.cos(ang)
    sin = jnp.sin(ang)
    return jnp.tile(cos, (1, 4)), jnp.tile(jnp.concatenate([-sin, sin], axis=1), (1, 2))


def _prep_layer(l, norm1_w, w_in, pool_w, pool_scale, dn_conv_w, dn_a_log, dn_dt_bias, dn_onorm_w, swa_sinks,
                proj_a, proj_b, proj_c, w_out, norm2_w, w_up, w_down):
    w = w_in[l]
    segments = ((3336, 6408), (512, 2048), (0, 512), (2048, 2560), (2568, 3080), (3080, 3336), (2560, 2568))
    w_perm = jnp.concatenate(
        [w[:, a:b].astype(BF16) for a, b in segments]
        + [jnp.zeros((D_MODEL, H_WIDTH - BA_OFF - 2 * DN_HEADS), BF16)], axis=1)
    lane_pad = lambda v: jnp.zeros((1, 128), F32).at[0, DN_HEADS:2 * DN_HEADS].set(v.astype(F32))
    row_pad = lambda v: jnp.zeros((2 * DN_HEADS, 128), F32).at[DN_HEADS:].set(
        jnp.broadcast_to(v.astype(F32)[:, None], (DN_HEADS, 128)))
    return dict(
        norm1=norm1_w[l][None].astype(F32), w_in=w_perm,
        pool_w=pool_w[l].astype(BF16), pool_scale=pool_scale[l][None].astype(F32),
        conv_w=dn_conv_w[l].astype(F32),
        gate=(lane_pad(dn_a_log[l]), lane_pad(dn_dt_bias[l]), row_pad(dn_a_log[l]), row_pad(dn_dt_bias[l])),
        onw=dn_onorm_w[l][None].astype(F32), sinks=swa_sinks[l].astype(F32),
        pa=proj_a[l].astype(BF16), pb=proj_b[l].astype(BF16), pc=proj_c[l].astype(BF16),
        wo=w_out[l].astype(BF16), norm2=norm2_w[l][None].astype(F32),
        wu=w_up[l].astype(BF16), wd=w_down[l].astype(BF16))


def _group_step(x3, p, fnw, final, pos0, pool_init, conv_init, s0, kind, swa_extra, pool_tiles, delta_tiles,
                swa_bb=1, s_stack=None, act=F32, fuse_conv=False):
    b, t, _ = x3.shape
    x2 = x3.reshape(b * t, D_MODEL)
    h, conv_tails = _inproj(x2, p['norm1'], p['w_in'], act,
                            conv=(conv_init, p['conv_w'], t) if fuse_conv else None)
    h3 = h.reshape(b, t, H_WIDTH)
    o_a = _pool(h3, pool_init, p['pool_w'], p['pool_scale'], pos0, *pool_tiles)
    o_b, s_new = _delta(h3, conv_init, s0[0], s0[1], p['conv_w'], p['gate'], p['onw'], *delta_tiles,
                        stack=s_stack, conv_done=fuse_conv)
    cos, sin = _rope_tables(pos0, t)
    if kind == 'prompt':
        o_c, k_rot = _swa_prompt(h3, swa_extra, cos, sin, p['sinks'], pos0, swa_bb)
    else:
        o_c, *k_rot = _swa_step(h3, swa_extra, cos, sin, p['sinks'], swa_bb)
    m = b * t
    h1 = _merge(o_a.reshape(m, 512), o_b.reshape(m, 512), o_c.reshape(m, 512), h, x2,
                p['pa'], p['pb'], p['pc'], p['wo'])
    out = _mlp(h1, p['norm2'], p['wu'], p['wd'], fnw, final)
    return out.reshape(b, t, D_MODEL), h3, s_new, k_rot, conv_tails


def kernel(x_prompt, x_sample, state_pool, state_conv, state_delta, cache_swa_k, cache_swa_v, meta_tokens, norm1_w, w_in, pool_w, pool_scale, dn_conv_w, dn_a_log, dn_dt_bias, dn_onorm_w, swa_sinks, proj_a, proj_b, proj_c, w_out, norm2_w, w_up, w_down, final_norm_w):
    depth = w_in.shape[0]
    bp, tp, _ = x_prompt.shape
    bs, ts, _ = x_sample.shape
    past = cache_swa_k.shape[2]
    fnw = final_norm_w[None].astype(F32)

    xm = meta_tokens[None].astype(F32)
    xp = x_prompt
    xs = x_sample
    outs = {k: [] for k in ('pool_p', 'conv_p', 'k_p', 'v_p', 'pool_s', 'conv_s')}
    delta_p = delta_s = None
    k_s = v_s = None
    to_t = lambda c: c.transpose(0, 1, 3, 4, 2).reshape(depth, bs, SWA_KV_WIDTH, past)
    cache_k_t, cache_v_t = to_t(cache_swa_k), to_t(cache_swa_v)
    for l in range(depth):
        p = _prep_layer(l, norm1_w, w_in, pool_w, pool_scale, dn_conv_w, dn_a_log, dn_dt_bias, dn_onorm_w,
                        swa_sinks, proj_a, proj_b, proj_c, w_out, norm2_w, w_up, w_down)
        final = l == depth - 1

        xm, hm, s_m, k_m, _ = _group_step(
            xm, p, fnw, final, 0,
            jnp.zeros((1, POOL_HALO, POOL_WIDTH), F32), jnp.zeros((1, CONV_HALO, CONV_CH), F32),
            (jnp.zeros((1, 1, DN_HEADS, DN_DK, DN_DV), F32), 0), 'step', None,
            (1, N_META), (1, N_META, N_META))

        meta_kv = jnp.zeros((SWA_WINDOW, 256), F32)
        meta_kv = meta_kv.at[SWA_WINDOW - N_META:, :SWA_KV_WIDTH].set(k_m[0][0])
        meta_kv = meta_kv.at[SWA_WINDOW - N_META:, SWA_KV_WIDTH:].set(hm[0, :, KVC_OFF + SWA_KV_WIDTH:KVC_OFF + 256])
        xp, hp, s_p, k_p, tails_p = _group_step(
            xp, p, fnw, final, N_META,
            jnp.broadcast_to(hm[:, :, U_OFF:U_OFF + POOL_WIDTH], (bp, POOL_HALO, POOL_WIDTH)),
            jnp.broadcast_to(hm[:, N_META - CONV_HALO:, QKV_OFF:QKV_OFF + CONV_CH], (bp, CONV_HALO, CONV_CH)),
            (jnp.broadcast_to(s_m[None], (1, bp, DN_HEADS, DN_DK, DN_DV)), 0), 'prompt', meta_kv,
            (1, 512), (4, 2 * DN_CHUNK, DN_CHUNK), swa_bb=bp, s_stack=(l, depth, delta_p), act=BF16, fuse_conv=True)
        delta_p = s_p
        outs['pool_p'].append(hp[:, tp - POOL_BUF:, U_OFF:U_OFF + POOL_WIDTH].astype(F32))
        if tails_p is None:
            outs['conv_p'].append(hp[:, tp - (CONV_W - 1):, QKV_OFF:QKV_OFF + CONV_CH].astype(F32))
        else:
            outs['conv_p'].append(tails_p.reshape(bp, -1, CONV_HALO, CONV_CH)[:, -1, CONV_HALO - (CONV_W - 1):])
        outs['k_p'].append(k_p.reshape(bp, SWA_WINDOW, SWA_KV_HEADS, SWA_HD))
        outs['v_p'].append(hp[:, tp - SWA_WINDOW:, KVC_OFF + SWA_KV_WIDTH:KVC_OFF + 256].astype(F32)
                           .reshape(bp, SWA_WINDOW, SWA_KV_HEADS, SWA_HD))

        xs, hs, s_s, (k_s, v_s), _ = _group_step(
            xs, p, fnw, final, PAST_LEN,
            jnp.concatenate([jnp.zeros((bs, POOL_HALO - POOL_BUF, POOL_WIDTH), F32), state_pool[l]], axis=1),
            jnp.concatenate([jnp.zeros((bs, CONV_HALO - (CONV_W - 1), CONV_CH), F32), state_conv[l]], axis=1),
            (state_delta, l), 'step', (cache_k_t, cache_v_t, l, depth, k_s, v_s),
            (bs, ts), (16, ts, ts), 16, s_stack=(l, depth, delta_s))
        delta_s = s_s
        outs['pool_s'].append(jnp.concatenate([state_pool[l], hs[:, :, U_OFF:U_OFF + POOL_WIDTH]], axis=1)[:, -POOL_BUF:])
        outs['conv_s'].append(jnp.concatenate([state_conv[l], hs[:, :, QKV_OFF:QKV_OFF + CONV_CH]], axis=1)[:, -(CONV_W - 1):])

    st = {k: jnp.stack(v) for k, v in outs.items()}
    from_t = lambda c: c.reshape(depth, bs, SWA_KV_HEADS, SWA_HD, past).transpose(0, 1, 4, 2, 3)
    return (xp, xs, st['pool_p'], st['conv_p'], delta_p, st['k_p'], st['v_p'],
            st['pool_s'], st['conv_s'], delta_s, from_t(k_s), from_t(v_s))
```

```python
import functools
import math

import jax
import jax.numpy as jnp
from jax import lax
from jax.experimental import pallas as pl
from jax.experimental.pallas import tpu as pltpu

F32 = jnp.float32
BF16 = jnp.bfloat16

D_MODEL = 1024
N_META = 16
EPS = 1e-6
POOL_GROUPS = 4
POOL_GROUP_DIM = 128
POOL_WIDTH = 512
POOL_WINDOWS = (2, 4, 8, 16)
POOL_BUF = 15
POOL_HALO = 16
DN_HEADS = 4
DN_DK = 128
DN_DV = 128
DN_QK = 512
DN_VW = 512
CONV_W = 4
CONV_CH = 1536
CONV_HALO = 8
DN_CHUNK = 64
GROUP_ROWS = 128
SWA_HEADS = 8
SWA_KV_HEADS = 2
SWA_GROUP = 4
SWA_HD = 64
SWA_WIDTH = 512
SWA_KV_WIDTH = 128
SWA_WINDOW = 128
ROPE_THETA = 10000.0
D_FF = 4096
MLP_TF = 1024
PAST_LEN = 16384

G_OFF = 0
QKV_OFF = 3072
U_OFF = 4608
Z_OFF = 5120
QC_OFF = 5632
KVC_OFF = 6144
BA_OFF = 6400
H_WIDTH = 6528
IN_CHUNKS = ((0, 1536), (1536, 3072), (3072, 4608), (4608, 6144), (6144, 6528))

VMEM_LIMIT = 56 * 1024 * 1024
NEG_BIG = -1e30


def _cparams(sem):
    return pltpu.CompilerParams(dimension_semantics=sem, vmem_limit_bytes=VMEM_LIMIT)


def _rms(x, w):
    return x * lax.rsqrt(jnp.mean(x * x, axis=-1, keepdims=True) + EPS) * w


def _l2n(v):
    return v * lax.rsqrt(jnp.sum(v * v, axis=-1, keepdims=True) + EPS)


def _short_conv(ext_ref, cw_ref, rows):
    base = CONV_HALO - (CONV_W - 1)
    y = ext_ref[..., base:base + rows, :] * cw_ref[0:1, :]
    for j in range(1, CONV_W):
        y = y + ext_ref[..., base + j:base + j + rows, :] * cw_ref[j:j + 1, :]
    return y * jax.nn.sigmoid(y)


def _qkv_heads(y):
    heads = range(DN_HEADS)
    qn = [_l2n(y[:, h * DN_DK:(h + 1) * DN_DK]) * (DN_DK ** -0.5) for h in heads]
    kn = [_l2n(y[:, DN_QK + h * DN_DK:DN_QK + (h + 1) * DN_DK]) for h in heads]
    vh = [y[:, 2 * DN_QK + h * DN_DV:2 * DN_QK + (h + 1) * DN_DV] for h in heads]
    return qn, kn, vh


def _inproj_kernel(x_ref, nw_ref, w_ref, *rest, tiles_per_seq):
    if tiles_per_seq:
        cinit_ref, cw_ref, o_ref, tail_ref, ext_ref = rest
    else:
        (o_ref,) = rest
    tm = x_ref.shape[0]
    if tiles_per_seq:
        first = pl.program_id(0) % tiles_per_seq == 0

        @pl.when(first)
        def _():
            ext_ref[0:CONV_HALO, :] = cinit_ref[0]

        @pl.when(jnp.logical_not(first))
        def _():
            ext_ref[0:CONV_HALO, :] = ext_ref[tm:tm + CONV_HALO, :]

    xn = _rms(x_ref[...], nw_ref[...]).astype(BF16)
    for a, b in sorted(IN_CHUNKS, key=lambda ab: ab[0] != QKV_OFF):
        acc = jnp.dot(xn, w_ref[:, a:b], preferred_element_type=F32)
        if tiles_per_seq and a == QKV_OFF:
            ext_ref[CONV_HALO:CONV_HALO + tm, :] = acc
            tail_ref[0] = acc[tm - CONV_HALO:, :]
            qn, kn, vh = _qkv_heads(_short_conv(ext_ref, cw_ref, tm))
            acc = jnp.concatenate(qn + kn + vh, axis=1)
        o_ref[:, a:b] = acc.astype(o_ref.dtype)


def _inproj(x, nw, w, act, conv=None):
    m = x.shape[0]
    tm = min(m, 512 if act == BF16 else 256)
    assert m % tm == 0
    in_specs = [
        pl.BlockSpec((tm, D_MODEL), lambda i: (i, 0)),
        pl.BlockSpec((1, D_MODEL), lambda i: (0, 0)),
        pl.BlockSpec((D_MODEL, H_WIDTH), lambda i: (0, 0), pipeline_mode=pl.Buffered(1)),
    ]
    out_specs = [pl.BlockSpec((tm, H_WIDTH), lambda i: (i, 0))]
    out_shape = [jax.ShapeDtypeStruct((m, H_WIDTH), act)]
    args = [x, nw, w]
    scratch = []
    tiles_per_seq = 0
    if conv is not None:
        cinit, cw, seq_rows = conv
        assert seq_rows % tm == 0
        tiles_per_seq = seq_rows // tm
        in_specs += [pl.BlockSpec((1, CONV_HALO, CONV_CH), lambda i: (i // tiles_per_seq, 0, 0)),
                     pl.BlockSpec((CONV_W, CONV_CH), lambda i: (0, 0))]
        args += [cinit, cw]
        out_specs.append(pl.BlockSpec((1, CONV_HALO, CONV_CH), lambda i: (i, 0, 0)))
        out_shape.append(jax.ShapeDtypeStruct((m // tm, CONV_HALO, CONV_CH), F32))
        scratch = [pltpu.VMEM((CONV_HALO + tm, CONV_CH), F32)]
    res = pl.pallas_call(
        functools.partial(_inproj_kernel, tiles_per_seq=tiles_per_seq),
        grid=(m // tm,),
        in_specs=in_specs,
        out_specs=out_specs,
        out_shape=out_shape,
        scratch_shapes=scratch,
        compiler_params=_cparams(("arbitrary",) if conv is not None else ("parallel",)),
        name="inproj",
    )(*args)
    return (res[0], res[1]) if conv is not None else (res[0], None)


def _pool_kernel(u_ref, init_ref, pw_ref, ps_ref, o_ref, ext_ref, *, pos0, tt):
    t = pl.program_id(1)
    bb = u_ref.shape[0]

    @pl.when(t == 0)
    def _():
        ext_ref[:, 0:POOL_HALO, :] = init_ref[...]

    @pl.when(t > 0)
    def _():
        ext_ref[:, 0:POOL_HALO, :] = ext_ref[:, tt:tt + POOL_HALO, :]

    ext_ref[:, POOL_HALO:POOL_HALO + tt, :] = u_ref[...].astype(F32)

    pos = pos0 + t * tt + lax.broadcasted_iota(jnp.int32, (1, tt, 1), 1)
    for g, w in enumerate(POOL_WINDOWS):
        cs = slice(g * POOL_GROUP_DIM, (g + 1) * POOL_GROUP_DIM)
        s = ext_ref[:, :, cs]
        x = s[:, POOL_HALO:POOL_HALO + tt]
        k = 1
        while k < w:
            s = s + pltpu.roll(s, k, 1)
            k *= 2
        s = s[:, POOL_HALO:POOL_HALO + tt]
        cnt = jnp.minimum(w, pos + 1).astype(F32)
        d = (s / cnt - x).reshape(bb * tt, POOL_GROUP_DIM)
        y = jnp.dot(d.astype(BF16), pw_ref[g], preferred_element_type=F32)
        o_ref[:, :, cs] = (y * ps_ref[:, cs]).reshape(bb, tt, POOL_GROUP_DIM).astype(o_ref.dtype)


def _pool(h3, init, pw, ps, pos0, bb, tt):
    b, t, _ = h3.shape
    return pl.pallas_call(
        functools.partial(_pool_kernel, pos0=pos0, tt=tt),
        grid=(b // bb, t // tt),
        in_specs=[
            pl.BlockSpec((bb, tt, POOL_WIDTH), lambda i, j: (i, j, U_OFF // POOL_WIDTH)),
            pl.BlockSpec((bb, POOL_HALO, POOL_WIDTH), lambda i, j: (i, 0, 0)),
            pl.BlockSpec((POOL_GROUPS, POOL_GROUP_DIM, POOL_GROUP_DIM), lambda i, j: (0, 0, 0)),
            pl.BlockSpec((1, POOL_WIDTH), lambda i, j: (0, 0)),
        ],
        out_specs=pl.BlockSpec((bb, tt, POOL_WIDTH), lambda i, j: (i, j, 0)),
        out_shape=jax.ShapeDtypeStruct((b, t, POOL_WIDTH), h3.dtype),
        scratch_shapes=[pltpu.VMEM((bb, POOL_HALO + tt, POOL_WIDTH), F32)],
        compiler_params=_cparams(("parallel", "arbitrary")),
        name="pool",
    )(h3, init, pw, ps)


def _softplus(x):
    return jnp.maximum(x, 0.0) + jnp.log1p(jnp.exp(-jnp.abs(x)))


def _dot_nt(a, b, **kw):
    return lax.dot_general(a, b, (((1,), (1,)), ((), ())), preferred_element_type=F32, **kw)


def _dot_tn(a, b):
    return lax.dot_general(a, b, (((0,), (0,)), ((), ())), preferred_element_type=F32)


def _dot(a, b):
    return jnp.dot(a, b, preferred_element_type=F32)


def _delta_kernel(qkv_ref, z_ref, ba_ref, cinit_ref, s0_ref, cw_ref, alog_ref, dtb_ref, alogt_ref, dtbt_ref,
                  onw_ref, *rest, bb, tb, chunk, conv_done):
    o_ref, s_ref, ext_ref = rest[-3:]
    t = pl.program_id(1)
    rows = bb * tb
    cpb = tb // chunk
    heads = list(range(DN_HEADS))
    mm = BF16 if rows >= 16 else F32
    mc = BF16 if chunk >= 16 else F32

    @pl.when(t == 0)
    def _():
        s_ref[...] = s0_ref[...]

    if conv_done:
        y = qkv_ref[...].astype(F32).reshape(rows, CONV_CH)
        qn = [y[:, h * DN_DK:(h + 1) * DN_DK] for h in heads]
        kn = [y[:, DN_QK + h * DN_DK:DN_QK + (h + 1) * DN_DK] for h in heads]
        vh = [y[:, 2 * DN_QK + h * DN_DV:2 * DN_QK + (h + 1) * DN_DV] for h in heads]
    else:
        @pl.when(t == 0)
        def _():
            ext_ref[:, 0:CONV_HALO, :] = cinit_ref[...]

        @pl.when(t > 0)
        def _():
            ext_ref[:, 0:CONV_HALO, :] = ext_ref[:, tb:tb + CONV_HALO, :]

        ext_ref[:, CONV_HALO:CONV_HALO + tb, :] = qkv_ref[...].astype(F32)
        qn, kn, vh = _qkv_heads(_short_conv(ext_ref, cw_ref, tb).reshape(rows, CONV_CH))

    gr = min(rows, GROUP_ROWS)
    groups = rows // gr
    spg = gr // tb
    keys = [(gi, h) for gi in range(groups) for h in heads]

    def part(v, gi):
        return v[gi * gr:(gi + 1) * gr]

    bav = ba_ref[...].astype(F32).reshape(rows, 128)
    zv = z_ref[...].astype(F32).reshape(rows, DN_VW)

    row = lax.broadcasted_iota(jnp.int32, (gr, gr), 0)
    col = lax.broadcasted_iota(jnp.int32, (gr, gr), 1)
    shift = int(math.log2(chunk))
    same = (row >> shift) == (col >> shift)
    tri = same & (row >= col)
    strict = same & (row > col)
    lane = lax.broadcasted_iota(jnp.int32, (gr, 128), 1)

    gcol, bcol, grow = {}, {}, {}
    for gi in range(groups):
        bav_g = part(bav, gi)
        if gr == 128:
            sub = bav_g.T[0:2 * DN_HEADS, :]
            g_t = -jnp.exp(alogt_ref[...]) * _softplus(sub + dtbt_ref[...])
            gcum_t = jnp.dot(g_t, (same & (col >= row)).astype(F32), preferred_element_type=F32,
                             precision=lax.Precision.HIGHEST)
            live = lax.broadcasted_iota(jnp.int32, sub.shape, 0) < DN_HEADS
            packed = jnp.where(live, jax.nn.sigmoid(sub), gcum_t)
            cols = jnp.concatenate([packed, jnp.zeros((gr - 2 * DN_HEADS, gr), F32)], axis=0).T
            beta_full = gcum = cols
            for h in heads:
                grow[gi, h] = gcum_t[DN_HEADS + h:DN_HEADS + h + 1, :]
        else:
            beta_full = jax.nn.sigmoid(bav_g)
            g_full = -jnp.exp(alog_ref[...]) * _softplus(bav_g + dtb_ref[...])
            gcum = jnp.dot(tri.astype(F32), g_full, preferred_element_type=F32, precision=lax.Precision.HIGHEST)
            ones = jnp.ones((gr, 128), F32)
            for h in heads:
                grow[gi, h] = _dot_nt(ones, jnp.where(lane == DN_HEADS + h, gcum, 0.0),
                                      precision=lax.Precision.HIGHEST)
        for h in heads:
            gcol[gi, h] = jnp.sum(jnp.where(lane == DN_HEADS + h, gcum, 0.0), axis=1, keepdims=True)
            bcol[gi, h] = jnp.sum(jnp.where(lane == h, beta_full, 0.0), axis=1, keepdims=True)
    eg = {k: jnp.exp(gcol[k]) for k in keys}
    qs = {(gi, h): part(qn[h], gi) for gi, h in keys}
    ks = {(gi, h): part(kn[h], gi) for gi, h in keys}
    vs = {(gi, h): part(vh[h], gi) for gi, h in keys}

    bshift = min(shift, 4)
    decay = {k: jnp.where(tri, jnp.exp(jnp.where(tri, gcol[k] - grow[k], 0.0)), 0.0) for k in keys}
    kb = {k: ks[k].astype(mm) for k in keys}
    kk = {k: _dot_nt(kb[k], kb[k]) for k in keys}
    qk = {k: _dot_nt(qs[k].astype(mm), kb[k]) for k in keys}
    m = {k: jnp.where(strict, kk[k] * decay[k] * bcol[k], 0.0) for k in keys}
    p = {k: jnp.where((row >> bshift) == (col >> bshift), -m[k], 0.0) for k in keys}
    xp = p
    for _ in range(bshift - 1):
        xb = {k: xp[k].astype(mm) for k in keys}
        xp = {k: _dot(xb[k], xb[k]) for k in keys}
        p = {k: p[k] + xp[k] + _dot(p[k].astype(mm), xp[k].astype(mm)) for k in keys}
    for s in range(bshift, shift):
        lower = ((row >> (s + 1)) == (col >> (s + 1))) & ((row >> s) > (col >> s))
        c_blk = {k: jnp.where(lower, m[k], 0.0) for k in keys}
        pb = {k: p[k].astype(mm) for k in keys}
        a = {k: c_blk[k] + _dot(pb[k], c_blk[k].astype(mm)) for k in keys}
        p = {k: p[k] - a[k] - _dot(a[k].astype(mm), pb[k]) for k in keys}
    rhs = {k: jnp.concatenate([vs[k] * bcol[k], ks[k] * (bcol[k] * eg[k])], axis=1) for k in keys}
    uw = {k: rhs[k] + _dot(p[k].astype(mm), rhs[k].astype(mm)) for k in keys}
    qkd = {k: (qk[k] * decay[k]).astype(mm) for k in keys}
    qg = {k: qs[k] * eg[k] for k in keys}

    pairs = [(b, h) for b in range(bb) for h in heads]
    state = {bh: s_ref[bh[0], bh[1]] for bh in pairs}
    dlt = {k: [] for k in keys}
    oq = {k: [] for k in keys}
    for c in range(cpb):
        def rs(b):
            start = (b % spg) * tb + c * chunk
            return slice(start, start + chunk)
        prod = {}
        for b, h in pairs:
            k = (b // spg, h)
            lhs = jnp.concatenate([uw[k][rs(b), DN_DV:], qg[k][rs(b)]], axis=0).astype(mc)
            prod[b, h] = _dot(lhs, state[b, h].astype(mc))
        for b, h in pairs:
            k = (b // spg, h)
            r = rs(b)
            d = uw[k][r, :DN_DV] - prod[b, h][:chunk]
            glast = gcol[k][r.stop - 1:r.stop, :]
            kg = ks[k][r] * jnp.exp(glast - gcol[k][r])
            state[b, h] = state[b, h] * jnp.exp(glast) + _dot_tn(kg.astype(mc), d.astype(mc))
            dlt[k].append((r.start, d))
            oq[k].append((r.start, prod[b, h][chunk:]))
    for b, h in pairs:
        s_ref[b, h] = state[b, h]

    def stack(parts):
        parts = [v for _, v in sorted(parts, key=lambda sv: sv[0])]
        return parts[0] if len(parts) == 1 else jnp.concatenate(parts, axis=0)

    for gi, h in keys:
        hs = slice(h * DN_DV, (h + 1) * DN_DV)
        o = stack(oq[gi, h]) + _dot(qkd[gi, h], stack(dlt[gi, h]).astype(mm))
        zh = part(zv, gi)[:, hs]
        gated = _rms(o, onw_ref[...]) * (zh * jax.nn.sigmoid(zh))
        o_ref[gi * spg:(gi + 1) * spg, :, hs] = gated.reshape(spg, tb, DN_DV).astype(o_ref.dtype)


def _delta(h3, cinit, s0, s0_layer, cw, gate, onw, bb, tb, chunk, stack=None, conv_done=False):
    b, t, _ = h3.shape
    assert b % bb == 0 and t % tb == 0 and tb % chunk == 0
    state_block = (None, bb, DN_HEADS, DN_DK, DN_DV)
    in_specs = [
        pl.BlockSpec((bb, tb, CONV_CH), lambda i, j: (i, j, QKV_OFF // CONV_CH)),
        pl.BlockSpec((bb, tb, DN_VW), lambda i, j: (i, j, Z_OFF // DN_VW)),
        pl.BlockSpec((bb, tb, 128), lambda i, j: (i, j, BA_OFF // 128)),
        pl.BlockSpec((bb, CONV_HALO, CONV_CH), lambda i, j: (i, 0, 0)),
        pl.BlockSpec(state_block, lambda i, j: (s0_layer, i, 0, 0, 0)),
        pl.BlockSpec((CONV_W, CONV_CH), lambda i, j: (0, 0)),
        pl.BlockSpec((1, 128), lambda i, j: (0, 0)),
        pl.BlockSpec((1, 128), lambda i, j: (0, 0)),
        pl.BlockSpec((2 * DN_HEADS, 128), lambda i, j: (0, 0)),
        pl.BlockSpec((2 * DN_HEADS, 128), lambda i, j: (0, 0)),
        pl.BlockSpec((1, DN_DV), lambda i, j: (0, 0)),
    ]
    args = [h3, h3, h3, cinit, s0, cw, *gate, onw]
    layer, depth, prev = (0, 1, None) if stack is None else stack
    aliases = {}
    if prev is not None:
        in_specs.append(pl.BlockSpec(memory_space=pl.ANY))
        aliases = {len(args): 1}
        args.append(prev)
    o, s = pl.pallas_call(
        functools.partial(_delta_kernel, bb=bb, tb=tb, chunk=chunk, conv_done=conv_done),
        grid=(b // bb, t // tb),
        in_specs=in_specs,
        out_specs=[
            pl.BlockSpec((bb, tb, DN_VW), lambda i, j: (i, j, 0)),
            pl.BlockSpec(state_block, lambda i, j: (layer, i, 0, 0, 0)),
        ],
        out_shape=[
            jax.ShapeDtypeStruct((b, t, DN_VW), h3.dtype),
            jax.ShapeDtypeStruct((depth, b, DN_HEADS, DN_DK, DN_DV), F32),
        ],
        scratch_shapes=[pltpu.VMEM((bb, CONV_HALO + tb, CONV_CH), F32)],
        input_output_aliases=aliases,
        compiler_params=_cparams(("parallel", "arbitrary")),
        name="delta",
    )(*args)
    return o, (s[0] if stack is None else s)


def _rope(x, cos, sin):
    width = x.shape[-1]
    reps = width // cos.shape[-1]
    if reps > 1:
        cos = jnp.concatenate([cos] * reps, axis=1)
        sin = jnp.concatenate([sin] * reps, axis=1)
    lane = lax.broadcasted_iota(jnp.int32, x.shape, 1)
    first_half = (lane & (SWA_HD - 1)) < (SWA_HD // 2)
    other = jnp.where(first_half, pltpu.roll(x, width - SWA_HD // 2, 1), pltpu.roll(x, SWA_HD // 2, 1))
    return x * cos + other * sin


def _pad_heads(x, lo, fill):
    xr = pltpu.roll(x, SWA_HD, 1)
    z = jnp.full_like(x, fill)
    return [jnp.where(lo, x, z), jnp.where(lo, z, xr), jnp.where(lo, xr, z), jnp.where(lo, z, x)]


def _swa_prompt_kernel(sinks_ref, q_ref, kv_ref, meta_ref, cos_ref, sin_ref, o_ref, klast_ref, kcat_ref, vcat_ref,
                       *, pos0):
    n = pl.program_id(1)
    bb = q_ref.shape[0]
    blk = SWA_WINDOW
    lo = lax.broadcasted_iota(jnp.int32, (blk, 128), 1) < SWA_HD

    def put(b, half, k, v):
        rows = slice(half * blk, (half + 1) * blk)
        for i, (a, c) in enumerate(zip(_pad_heads(k, lo, 0.0), _pad_heads(v, lo, 1.0))):
            kcat_ref[b, i, rows, :] = a.astype(BF16)
            vcat_ref[b, i, rows, :] = c.astype(BF16)

    @pl.when(n == 0)
    def _():
        for b in range(bb):
            put(b, 0, meta_ref[:, :SWA_KV_WIDTH], meta_ref[:, SWA_KV_WIDTH:])

    @pl.when(n > 0)
    def _():
        kcat_ref[:, :, 0:blk, :] = kcat_ref[:, :, blk:2 * blk, :]
        vcat_ref[:, :, 0:blk, :] = vcat_ref[:, :, blk:2 * blk, :]

    cos = cos_ref[...]
    sin = sin_ref[...]
    qt = {}
    for b in range(bb):
        q = _rope(q_ref[b].astype(F32), cos, sin) * (SWA_HD ** -0.5)
        kv = kv_ref[b].astype(F32)
        k = _rope(kv[:, :SWA_KV_WIDTH], cos, sin)
        klast_ref[b] = k
        put(b, 1, k, kv[:, SWA_KV_WIDTH:])
        for i in range(SWA_HEADS // 2):
            qt[b, i] = q[:, i * 128:(i + 1) * 128].astype(BF16)

    r = lax.broadcasted_iota(jnp.int32, (blk, 2 * blk), 0)
    j = lax.broadcasted_iota(jnp.int32, (blk, 2 * blk), 1)
    jmin = jnp.maximum(blk - pos0 - n * blk, 0)
    mask = ((j < blk) & (j > r) & (j >= jmin)) | ((j >= blk) & ((j - blk) <= r))
    keys = [(b, h) for b in range(bb) for h in range(SWA_HEADS)]
    s = {(b, h): jnp.where(mask, _dot_nt(qt[b, h // 2], kcat_ref[b, 2 * (h // SWA_GROUP) + h % 2]), NEG_BIG)
         for b, h in keys}
    m = {(b, h): jnp.maximum(jnp.max(s[b, h], axis=1, keepdims=True), sinks_ref[h]) for b, h in keys}
    e = {k: jnp.exp(s[k] - m[k]).astype(BF16) for k in keys}
    sink_w = {(b, h): jnp.exp(sinks_ref[h] - m[b, h]) for b, h in keys}
    for b in range(bb):
        for i in range(SWA_HEADS // 2):
            he, ho = 2 * i, 2 * i + 1
            g = he // SWA_GROUP
            a_e = _dot(e[b, he], vcat_ref[b, 2 * g])
            a_o = _dot(e[b, ho], vcat_ref[b, 2 * g + 1])
            den = (jnp.where(lo, pltpu.roll(a_e, SWA_HD, 1), pltpu.roll(a_o, SWA_HD, 1))
                   + jnp.where(lo, sink_w[b, he], sink_w[b, ho]))
            o_ref[b, :, i * 128:(i + 1) * 128] = (jnp.where(lo, a_e, a_o) / den).astype(o_ref.dtype)


def _swa_prompt(h3, meta_kv, cos, sin, sinks, pos0, bb):
    b, t, _ = h3.shape
    blk = SWA_WINDOW
    assert t % blk == 0 and b % bb == 0
    return pl.pallas_call(
        functools.partial(_swa_prompt_kernel, pos0=pos0),
        grid=(b // bb, t // blk),
        in_specs=[
            pl.BlockSpec(memory_space=pltpu.SMEM),
            pl.BlockSpec((bb, blk, SWA_WIDTH), lambda i, n: (i, n, QC_OFF // SWA_WIDTH)),
            pl.BlockSpec((bb, blk, 256), lambda i, n: (i, n, KVC_OFF // 256)),
            pl.BlockSpec((blk, 256), lambda i, n: (0, 0)),
            pl.BlockSpec((blk, 128), lambda i, n: (n, 0)),
            pl.BlockSpec((blk, 128), lambda i, n: (n, 0)),
        ],
        out_specs=[
            pl.BlockSpec((bb, blk, SWA_WIDTH), lambda i, n: (i, n, 0)),
            pl.BlockSpec((bb, blk, SWA_KV_WIDTH), lambda i, n: (i, 0, 0)),
        ],
        out_shape=[
            jax.ShapeDtypeStruct((b, t, SWA_WIDTH), h3.dtype),
            jax.ShapeDtypeStruct((b, blk, SWA_KV_WIDTH), F32),
        ],
        scratch_shapes=[pltpu.VMEM((bb, 4, 2 * blk, 128), BF16), pltpu.VMEM((bb, 4, 2 * blk, 128), BF16)],
        compiler_params=_cparams(("parallel", "arbitrary")),
        name="swa_prompt",
    )(sinks, h3, h3, meta_kv, cos, sin)


def _swa_step_kernel(sink_ref, q_ref, kv_ref, *rest, bb, t_new, window):
    if window:
        kc_ref, vc_ref, cos_ref, sin_ref = rest[:4]
        o_ref, kout_ref, vout_ref = rest[-3:]
    else:
        cos_ref, sin_ref, o_ref, knew_ref = rest
    cos = cos_ref[...]
    sin = sin_ref[...]
    rows = SWA_HEADS * t_new
    mn = BF16 if t_new >= 16 else F32
    lo = lax.broadcasted_iota(jnp.int32, (t_new, 128), 1) < SWA_HD
    tq = lax.broadcasted_iota(jnp.int32, (rows, t_new), 0) & (t_new - 1)
    mask_n = lax.broadcasted_iota(jnp.int32, (rows, t_new), 1) <= tq
    if window:
        diff = ((lax.broadcasted_iota(jnp.int32, (rows, window), 0) & (t_new - 1)) + window
                - lax.broadcasted_iota(jnp.int32, (rows, window), 1))
        mask_c = (diff >= 0) & (diff < SWA_WINDOW)
    sink = sink_ref[...]
    seqs = range(bb)

    qall, kk, vv = [], [], []
    for s in seqs:
        q = _rope(q_ref[s].astype(F32), cos, sin) * (SWA_HD ** -0.5)
        kv = kv_ref[s].astype(F32)
        k = _rope(kv[:, :SWA_KV_WIDTH], cos, sin)
        if not window:
            knew_ref[s] = k
        pieces = []
        for h in range(SWA_HEADS):
            g = h // SWA_GROUP
            tile = q[:, (h // 2) * 128:(h // 2 + 1) * 128]
            if h % 2 != g:
                tile = pltpu.roll(tile, SWA_HD, 1)
            pieces.append(jnp.where(lo, tile, 0.0) if g == 0 else jnp.where(lo, 0.0, tile))
        qall.append(jnp.concatenate(pieces, axis=0))
        kk.append(k)
        vv.append(kv[:, SWA_KV_WIDTH:])

    s_n = [jnp.where(mask_n, _dot_nt(qall[s].astype(mn), kk[s].astype(mn)), NEG_BIG) for s in seqs]
    m = [jnp.maximum(jnp.max(s_n[s], axis=1, keepdims=True), sink) for s in seqs]
    if window:
        s_c = [jnp.where(mask_c, _dot(qall[s].astype(BF16), kc_ref[s].astype(BF16)), NEG_BIG) for s in seqs]
        m = [jnp.maximum(m[s], jnp.max(s_c[s], axis=1, keepdims=True)) for s in seqs]
    e_n = [jnp.exp(s_n[s] - m[s]) for s in seqs]
    den = [jnp.sum(e_n[s], axis=1, keepdims=True) + jnp.exp(sink - m[s]) for s in seqs]
    acc = [_dot(e_n[s].astype(mn), vv[s].astype(mn)) for s in seqs]
    if window:
        e_c = [jnp.exp(s_c[s] - m[s]) for s in seqs]
        den = [den[s] + jnp.sum(e_c[s], axis=1, keepdims=True) for s in seqs]
        acc = [acc[s] + _dot_nt(e_c[s].astype(BF16), vc_ref[s].astype(BF16)) for s in seqs]

        def slide(old_t, new):
            kept = pltpu.roll(old_t, window - t_new, 1)
            fresh = jnp.concatenate([jnp.zeros((window - t_new, SWA_KV_WIDTH), F32), new], axis=0).T
            newest = lax.broadcasted_iota(jnp.int32, kept.shape, 1) >= window - t_new
            return jnp.where(newest, fresh, kept)

        for s in seqs:
            kout_ref[s] = slide(kc_ref[s], kk[s])
            vout_ref[s] = slide(vc_ref[s], vv[s])
    for s in seqs:
        a = acc[s] / den[s]
        for i in range(SWA_HEADS // 2):
            he, ho = 2 * i, 2 * i + 1
            g = he // SWA_GROUP
            a_e = a[he * t_new:(he + 1) * t_new]
            a_o = a[ho * t_new:(ho + 1) * t_new]
            if g == 1:
                a_e = pltpu.roll(a_e, SWA_HD, 1)
            else:
                a_o = pltpu.roll(a_o, SWA_HD, 1)
            o_ref[s, :, i * 128:(i + 1) * 128] = jnp.where(lo, a_e, a_o).astype(o_ref.dtype)


def _swa_step(h3, caches, cos, sin, sinks, bb):
    b, t, _ = h3.shape
    assert b % bb == 0
    sink_rows = jnp.repeat(sinks, t)[:, None]
    in_specs = [
        pl.BlockSpec((SWA_HEADS * t, 1), lambda i: (0, 0)),
        pl.BlockSpec((bb, t, SWA_WIDTH), lambda i: (i, 0, QC_OFF // SWA_WIDTH)),
        pl.BlockSpec((bb, t, 256), lambda i: (i, 0, KVC_OFF // 256)),
    ]
    args = [sink_rows, h3, h3]
    out_specs = [pl.BlockSpec((bb, t, SWA_WIDTH), lambda i: (i, 0, 0))]
    out_shape = [jax.ShapeDtypeStruct((b, t, SWA_WIDTH), h3.dtype)]
    window = 0
    aliases = {}
    if caches is None:
        out_specs.append(pl.BlockSpec((bb, t, SWA_KV_WIDTH), lambda i: (i, 0, 0)))
        out_shape.append(jax.ShapeDtypeStruct((b, t, SWA_KV_WIDTH), F32))
    else:
        k_t, v_t, layer, depth, prev_k, prev_v = caches
        window = k_t.shape[-1]
        assert window == SWA_KV_WIDTH and t <= window
        cache_block = pl.BlockSpec((None, bb, SWA_KV_WIDTH, window), lambda i: (layer, i, 0, 0))
        in_specs += [cache_block] * 2
        args += [k_t, v_t]
        out_specs += [cache_block] * 2
        out_shape += [jax.ShapeDtypeStruct((depth, b, SWA_KV_WIDTH, window), F32)] * 2
    in_specs += [pl.BlockSpec((t, 128), lambda i: (0, 0))] * 2
    args += [cos, sin]
    if caches is not None and prev_k is not None:
        in_specs += [pl.BlockSpec(memory_space=pl.ANY)] * 2
        aliases = {len(args): 1, len(args) + 1: 2}
        args += [prev_k, prev_v]
    return pl.pallas_call(
        functools.partial(_swa_step_kernel, bb=bb, t_new=t, window=window),
        grid=(b // bb,),
        in_specs=in_specs,
        out_specs=out_specs,
        out_shape=out_shape,
        input_output_aliases=aliases,
        compiler_params=_cparams(("parallel",)),
        name="swa_step",
    )(*args)


def _merge_mlp_kernel(oa_ref, ob_ref, oc_ref, ga_ref, gb_ref, gc_ref, x_ref, pa_ref, pb_ref, pc_ref, wo_ref,
                      nw_ref, wu_ref, wd_ref, fnw_ref, o_ref, *, final):
    def branch(o_r, g_r, p_r):
        return jax.nn.sigmoid(g_r[...].astype(F32)) * jnp.dot(o_r[...].astype(BF16), p_r[...],
                                                              preferred_element_type=F32)

    m = branch(oa_ref, ga_ref, pa_ref) + branch(ob_ref, gb_ref, pb_ref) + branch(oc_ref, gc_ref, pc_ref)
    h1 = x_ref[...] + jnp.dot(m.astype(BF16), wo_ref[...], preferred_element_type=F32)
    xn = _rms(h1, nw_ref[...]).astype(BF16)
    y = h1
    for j in range(D_FF // MLP_TF):
        fs = slice(j * MLP_TF, (j + 1) * MLP_TF)
        a = jnp.maximum(jnp.dot(xn, wu_ref[:, fs], preferred_element_type=F32), 0.0)
        y = y + jnp.dot((a * a).astype(BF16), wd_ref[fs, :], preferred_element_type=F32)
    o_ref[...] = _rms(y, fnw_ref[...]) if final else y


def _merge_mlp(oa, ob, oc, h, x, pa, pb, pc, wo, nw, wu, wd, fnw, final):
    m = x.shape[0]
    tm = min(m, 512)
    assert m % tm == 0
    row = lambda w: pl.BlockSpec((tm, w), lambda i: (i, 0))
    gate = lambda c: pl.BlockSpec((tm, D_MODEL), lambda i: (i, c))
    resident = lambda a: pl.BlockSpec(a.shape, lambda i: (0, 0), pipeline_mode=pl.Buffered(1))
    return pl.pallas_call(
        functools.partial(_merge_mlp_kernel, final=final),
        grid=(m // tm,),
        in_specs=[row(512), row(512), row(512), gate(0), gate(1), gate(2), row(D_MODEL),
                  resident(pa), resident(pb), resident(pc), resident(wo),
                  resident(nw), resident(wu), resident(wd), resident(fnw)],
        out_specs=row(D_MODEL),
        out_shape=jax.ShapeDtypeStruct((m, D_MODEL), F32),
        compiler_params=_cparams(("parallel",)),
        name="merge_mlp",
    )(oa, ob, oc, h, h, h, x, pa, pb, pc, wo, nw, wu, wd, fnw)


def _rope_tables(pos0, t):
    half = SWA_HD // 2
    inv = ROPE_THETA ** (-jnp.arange(half, dtype=F32) / half)
    ang = (pos0 + jnp.arange(t)).astype(F32)[:, None] * inv[None, :]
    cos = jnp.cos(ang)
    sin = jnp.sin(ang)
    return jnp.tile(cos, (1, 4)), jnp.tile(jnp.concatenate([-sin, sin], axis=1), (1, 2))


def _prep_layer(l, norm1_w, w_in, pool_w, pool_scale, dn_conv_w, dn_a_log, dn_dt_bias, dn_onorm_w, swa_sinks,
                proj_a, proj_b, proj_c, w_out, norm2_w, w_up, w_down):
    w = w_in[l]
    segments = ((3336, 6408), (512, 2048), (0, 512), (2048, 2560), (2568, 3080), (3080, 3336), (2560, 2568))
    w_perm = jnp.concatenate(
        [w[:, a:b].astype(BF16) for a, b in segments]
        + [jnp.zeros((D_MODEL, H_WIDTH - BA_OFF - 2 * DN_HEADS), BF16)], axis=1)
    lane_pad = lambda v: jnp.zeros((1, 128), F32).at[0, DN_HEADS:2 * DN_HEADS].set(v.astype(F32))
    row_pad = lambda v: jnp.zeros((2 * DN_HEADS, 128), F32).at[DN_HEADS:].set(
        jnp.broadcast_to(v.astype(F32)[:, None], (DN_HEADS, 128)))
    return dict(
        norm1=norm1_w[l][None].astype(F32), w_in=w_perm,
        pool_w=pool_w[l].astype(BF16), pool_scale=pool_scale[l][None].astype(F32),
        conv_w=dn_conv_w[l].astype(F32),
        gate=(lane_pad(dn_a_log[l]), lane_pad(dn_dt_bias[l]), row_pad(dn_a_log[l]), row_pad(dn_dt_bias[l])),
        onw=dn_onorm_w[l][None].astype(F32), sinks=swa_sinks[l].astype(F32),
        pa=proj_a[l].astype(BF16), pb=proj_b[l].astype(BF16), pc=proj_c[l].astype(BF16),
        wo=w_out[l].astype(BF16), norm2=norm2_w[l][None].astype(F32),
        wu=w_up[l].astype(BF16), wd=w_down[l].astype(BF16))


def _group_step(x3, p, fnw, final, pos0, pool_init, conv_init, s0, kind, swa_extra, pool_tiles, delta_tiles,
                swa_bb=1, s_stack=None, act=F32, fuse_conv=False):
    b, t, _ = x3.shape
    x2 = x3.reshape(b * t, D_MODEL)
    h, conv_tails = _inproj(x2, p['norm1'], p['w_in'], act,
                            conv=(conv_init, p['conv_w'], t) if fuse_conv else None)
    h3 = h.reshape(b, t, H_WIDTH)
    o_a = _pool(h3, pool_init, p['pool_w'], p['pool_scale'], pos0, *pool_tiles)
    o_b, s_new = _delta(h3, conv_init, s0[0], s0[1], p['conv_w'], p['gate'], p['onw'], *delta_tiles,
                        stack=s_stack, conv_done=fuse_conv)
    cos, sin = _rope_tables(pos0, t)
    if kind == 'prompt':
        o_c, k_rot = _swa_prompt(h3, swa_extra, cos, sin, p['sinks'], pos0, swa_bb)
    else:
        o_c, *k_rot = _swa_step(h3, swa_extra, cos, sin, p['sinks'], swa_bb)
    m = b * t
    out = _merge_mlp(o_a.reshape(m, 512), o_b.reshape(m, 512), o_c.reshape(m, 512), h, x2,
                     p['pa'], p['pb'], p['pc'], p['wo'], p['norm2'], p['wu'], p['wd'], fnw, final)
    return out.reshape(b, t, D_MODEL), h3, s_new, k_rot, conv_tails


def kernel(x_prompt, x_sample, state_pool, state_conv, state_delta, cache_swa_k, cache_swa_v, meta_tokens, norm1_w, w_in, pool_w, pool_scale, dn_conv_w, dn_a_log, dn_dt_bias, dn_onorm_w, swa_sinks, proj_a, proj_b, proj_c, w_out, norm2_w, w_up, w_down, final_norm_w):
    depth = w_in.shape[0]
    bp, tp, _ = x_prompt.shape
    bs, ts, _ = x_sample.shape
    past = cache_swa_k.shape[2]
    fnw = final_norm_w[None].astype(F32)

    xm = meta_tokens[None].astype(F32)
    xp = x_prompt
    xs = x_sample
    outs = {k: [] for k in ('pool_p', 'conv_p', 'k_p', 'v_p', 'pool_s', 'conv_s')}
    delta_p = delta_s = None
    k_s = v_s = None
    to_t = lambda c: c.transpose(0, 1, 3, 4, 2).reshape(depth, bs, SWA_KV_WIDTH, past)
    cache_k_t, cache_v_t = to_t(cache_swa_k), to_t(cache_swa_v)
    for l in range(depth):
        p = _prep_layer(l, norm1_w, w_in, pool_w, pool_scale, dn_conv_w, dn_a_log, dn_dt_bias, dn_onorm_w,
                        swa_sinks, proj_a, proj_b, proj_c, w_out, norm2_w, w_up, w_down)
        final = l == depth - 1

        xm, hm, s_m, k_m, _ = _group_step(
            xm, p, fnw, final, 0,
            jnp.zeros((1, POOL_HALO, POOL_WIDTH), F32), jnp.zeros((1, CONV_HALO, CONV_CH), F32),
            (jnp.zeros((1, 1, DN_HEADS, DN_DK, DN_DV), F32), 0), 'step', None,
            (1, N_META), (1, N_META, N_META))

        meta_kv = jnp.zeros((SWA_WINDOW, 256), F32)
        meta_kv = meta_kv.at[SWA_WINDOW - N_META:, :SWA_KV_WIDTH].set(k_m[0][0])
        meta_kv = meta_kv.at[SWA_WINDOW - N_META:, SWA_KV_WIDTH:].set(hm[0, :, KVC_OFF + SWA_KV_WIDTH:KVC_OFF + 256])
        xp, hp, s_p, k_p, tails_p = _group_step(
            xp, p, fnw, final, N_META,
            jnp.broadcast_to(hm[:, :, U_OFF:U_OFF + POOL_WIDTH], (bp, POOL_HALO, POOL_WIDTH)),
            jnp.broadcast_to(hm[:, N_META - CONV_HALO:, QKV_OFF:QKV_OFF + CONV_CH], (bp, CONV_HALO, CONV_CH)),
            (jnp.broadcast_to(s_m[None], (1, bp, DN_HEADS, DN_DK, DN_DV)), 0), 'prompt', meta_kv,
            (1, 512), (4, 2 * DN_CHUNK, DN_CHUNK), swa_bb=bp, s_stack=(l, depth, delta_p), act=BF16, fuse_conv=True)
        delta_p = s_p
        outs['pool_p'].append(hp[:, tp - POOL_BUF:, U_OFF:U_OFF + POOL_WIDTH].astype(F32))
        if tails_p is None:
            outs['conv_p'].append(hp[:, tp - (CONV_W - 1):, QKV_OFF:QKV_OFF + CONV_CH].astype(F32))
        else:
            outs['conv_p'].append(tails_p.reshape(bp, -1, CONV_HALO, CONV_CH)[:, -1, CONV_HALO - (CONV_W - 1):])
        outs['k_p'].append(k_p.reshape(bp, SWA_WINDOW, SWA_KV_HEADS, SWA_HD))
        outs['v_p'].append(hp[:, tp - SWA_WINDOW:, KVC_OFF + SWA_KV_WIDTH:KVC_OFF + 256].astype(F32)
                           .reshape(bp, SWA_WINDOW, SWA_KV_HEADS, SWA_HD))

        xs, hs, s_s, (k_s, v_s), _ = _group_step(
            xs, p, fnw, final, PAST_LEN,
            jnp.concatenate([jnp.zeros((bs, POOL_HALO - POOL_BUF, POOL_WIDTH), F32), state_pool[l]], axis=1),
            jnp.concatenate([jnp.zeros((bs, CONV_HALO - (CONV_W - 1), CONV_CH), F32), state_conv[l]], axis=1),
            (state_delta, l), 'step', (cache_k_t, cache_v_t, l, depth, k_s, v_s),
            (bs, ts), (16, ts, ts), 16, s_stack=(l, depth, delta_s))
        delta_s = s_s
        outs['pool_s'].append(jnp.concatenate([state_pool[l], hs[:, :, U_OFF:U_OFF + POOL_WIDTH]], axis=1)[:, -POOL_BUF:])
        outs['conv_s'].append(jnp.concatenate([state_conv[l], hs[:, :, QKV_OFF:QKV_OFF + CONV_CH]], axis=1)[:, -(CONV_W - 1):])

    st = {k: jnp.stack(v) for k, v in outs.items()}
    from_t = lambda c: c.reshape(depth, bs, SWA_KV_HEADS, SWA_HD, past).transpose(0, 1, 4, 2, 3)
    return (xp, xs, st['pool_p'], st['conv_p'], delta_p, st['k_p'], st['v_p'],
            st['pool_s'], st['conv_s'], delta_s, from_t(k_s), from_t(v_s))
```

```python
import functools
import math

import jax
import jax.numpy as jnp
from jax import lax
from jax.experimental import pallas as pl
from jax.experimental.pallas import tpu as pltpu

F32 = jnp.float32
BF16 = jnp.bfloat16

D_MODEL = 1024
N_META = 16
EPS = 1e-6
POOL_GROUPS = 4
POOL_GROUP_DIM = 128
POOL_WIDTH = 512
POOL_WINDOWS = (2, 4, 8, 16)
POOL_BUF = 15
POOL_HALO = 16
DN_HEADS = 4
DN_DK = 128
DN_DV = 128
DN_QK = 512
DN_VW = 512
CONV_W = 4
CONV_CH = 1536
CONV_HALO = 8
DN_CHUNK = 64
GROUP_ROWS = 128
SWA_HEADS = 8
SWA_KV_HEADS = 2
SWA_GROUP = 4
SWA_HD = 64
SWA_WIDTH = 512
SWA_KV_WIDTH = 128
SWA_WINDOW = 128
ROPE_THETA = 10000.0
D_FF = 4096
MLP_TF = 1024
PAST_LEN = 16384

G_OFF = 0
QKV_OFF = 3072
U_OFF = 4608
Z_OFF = 5120
QC_OFF = 5632
KVC_OFF = 6144
BA_OFF = 6400
H_WIDTH = 6528
IN_CHUNKS = ((0, 1536), (1536, 3072), (3072, 4608), (4608, 6144), (6144, 6528))

VMEM_LIMIT = 56 * 1024 * 1024
NEG_BIG = -1e30


def _cparams(sem):
    return pltpu.CompilerParams(dimension_semantics=sem, vmem_limit_bytes=VMEM_LIMIT)


def _rms(x, w):
    return x * lax.rsqrt(jnp.mean(x * x, axis=-1, keepdims=True) + EPS) * w


def _l2n(v):
    return v * lax.rsqrt(jnp.sum(v * v, axis=-1, keepdims=True) + EPS)


def _short_conv(ext_ref, cw_ref, rows):
    e = ext_ref[...]
    axis = e.ndim - 2
    y = e * cw_ref[CONV_W - 1:CONV_W, :]
    for j in range(CONV_W - 1):
        y = y + pltpu.roll(e, CONV_W - 1 - j, axis) * cw_ref[j:j + 1, :]
    y = y[..., CONV_HALO:CONV_HALO + rows, :]
    return y * jax.nn.sigmoid(y)


def _qkv_heads(y):
    heads = range(DN_HEADS)
    qn = [_l2n(y[:, h * DN_DK:(h + 1) * DN_DK]) * (DN_DK ** -0.5) for h in heads]
    kn = [_l2n(y[:, DN_QK + h * DN_DK:DN_QK + (h + 1) * DN_DK]) for h in heads]
    vh = [y[:, 2 * DN_QK + h * DN_DV:2 * DN_QK + (h + 1) * DN_DV] for h in heads]
    return qn, kn, vh


def _inproj_kernel(x_ref, nw_ref, w_ref, *rest, tiles_per_seq):
    if tiles_per_seq:
        cinit_ref, cw_ref, o_ref, tail_ref, ext_ref = rest
    else:
        (o_ref,) = rest
    tm = x_ref.shape[0]
    if tiles_per_seq:
        first = pl.program_id(0) % tiles_per_seq == 0

        @pl.when(first)
        def _():
            ext_ref[0:CONV_HALO, :] = cinit_ref[0]

        @pl.when(jnp.logical_not(first))
        def _():
            ext_ref[0:CONV_HALO, :] = ext_ref[tm:tm + CONV_HALO, :]

    xn = _rms(x_ref[...], nw_ref[...]).astype(BF16)
    for a, b in sorted(IN_CHUNKS, key=lambda ab: ab[0] != QKV_OFF):
        acc = jnp.dot(xn, w_ref[:, a:b], preferred_element_type=F32)
        if tiles_per_seq and a == QKV_OFF:
            ext_ref[CONV_HALO:CONV_HALO + tm, :] = acc
            tail_ref[0] = acc[tm - CONV_HALO:, :]
            qn, kn, vh = _qkv_heads(_short_conv(ext_ref, cw_ref, tm))
            acc = jnp.concatenate(qn + kn + vh, axis=1)
        o_ref[:, a:b] = acc.astype(o_ref.dtype)


def _inproj(x, nw, w, act, conv=None):
    m = x.shape[0]
    tm = min(m, 512 if act == BF16 else 256)
    assert m % tm == 0
    in_specs = [
        pl.BlockSpec((tm, D_MODEL), lambda i: (i, 0)),
        pl.BlockSpec((1, D_MODEL), lambda i: (0, 0)),
        pl.BlockSpec((D_MODEL, H_WIDTH), lambda i: (0, 0), pipeline_mode=pl.Buffered(1)),
    ]
    out_specs = [pl.BlockSpec((tm, H_WIDTH), lambda i: (i, 0))]
    out_shape = [jax.ShapeDtypeStruct((m, H_WIDTH), act)]
    args = [x, nw, w]
    scratch = []
    tiles_per_seq = 0
    if conv is not None:
        cinit, cw, seq_rows = conv
        assert seq_rows % tm == 0
        tiles_per_seq = seq_rows // tm
        in_specs += [pl.BlockSpec((1, CONV_HALO, CONV_CH), lambda i: (i // tiles_per_seq, 0, 0)),
                     pl.BlockSpec((CONV_W, CONV_CH), lambda i: (0, 0))]
        args += [cinit, cw]
        out_specs.append(pl.BlockSpec((1, CONV_HALO, CONV_CH), lambda i: (i, 0, 0)))
        out_shape.append(jax.ShapeDtypeStruct((m // tm, CONV_HALO, CONV_CH), F32))
        scratch = [pltpu.VMEM((CONV_HALO + tm, CONV_CH), F32)]
    res = pl.pallas_call(
        functools.partial(_inproj_kernel, tiles_per_seq=tiles_per_seq),
        grid=(m // tm,),
        in_specs=in_specs,
        out_specs=out_specs,
        out_shape=out_shape,
        scratch_shapes=scratch,
        compiler_params=_cparams(("arbitrary",) if conv is not None else ("parallel",)),
        name="inproj",
    )(*args)
    return (res[0], res[1]) if conv is not None else (res[0], None)


def _pool_kernel(u_ref, init_ref, pw_ref, ps_ref, o_ref, ext_ref, *, pos0, tt):
    t = pl.program_id(1)
    bb = u_ref.shape[0]

    @pl.when(t == 0)
    def _():
        ext_ref[:, 0:POOL_HALO, :] = init_ref[...]

    @pl.when(t > 0)
    def _():
        ext_ref[:, 0:POOL_HALO, :] = ext_ref[:, tt:tt + POOL_HALO, :]

    ext_ref[:, POOL_HALO:POOL_HALO + tt, :] = u_ref[...].astype(F32)

    pos = pos0 + t * tt + lax.broadcasted_iota(jnp.int32, (1, tt, 1), 1)
    for g, w in enumerate(POOL_WINDOWS):
        cs = slice(g * POOL_GROUP_DIM, (g + 1) * POOL_GROUP_DIM)
        s = ext_ref[:, :, cs]
        x = s[:, POOL_HALO:POOL_HALO + tt]
        k = 1
        while k < w:
            s = s + pltpu.roll(s, k, 1)
            k *= 2
        s = s[:, POOL_HALO:POOL_HALO + tt]
        cnt = jnp.minimum(w, pos + 1).astype(F32)
        d = (s / cnt - x).reshape(bb * tt, POOL_GROUP_DIM)
        y = jnp.dot(d.astype(BF16), pw_ref[g], preferred_element_type=F32)
        o_ref[:, :, cs] = (y * ps_ref[:, cs]).reshape(bb, tt, POOL_GROUP_DIM).astype(o_ref.dtype)


def _pool(h3, init, pw, ps, pos0, bb, tt):
    b, t, _ = h3.shape
    return pl.pallas_call(
        functools.partial(_pool_kernel, pos0=pos0, tt=tt),
        grid=(b // bb, t // tt),
        in_specs=[
            pl.BlockSpec((bb, tt, POOL_WIDTH), lambda i, j: (i, j, U_OFF // POOL_WIDTH)),
            pl.BlockSpec((bb, POOL_HALO, POOL_WIDTH), lambda i, j: (i, 0, 0)),
            pl.BlockSpec((POOL_GROUPS, POOL_GROUP_DIM, POOL_GROUP_DIM), lambda i, j: (0, 0, 0)),
            pl.BlockSpec((1, POOL_WIDTH), lambda i, j: (0, 0)),
        ],
        out_specs=pl.BlockSpec((bb, tt, POOL_WIDTH), lambda i, j: (i, j, 0)),
        out_shape=jax.ShapeDtypeStruct((b, t, POOL_WIDTH), h3.dtype),
        scratch_shapes=[pltpu.VMEM((bb, POOL_HALO + tt, POOL_WIDTH), F32)],
        compiler_params=_cparams(("parallel", "arbitrary")),
        name="pool",
    )(h3, init, pw, ps)


def _softplus(x):
    return jnp.maximum(x, 0.0) + jnp.log1p(jnp.exp(-jnp.abs(x)))


def _dot_nt(a, b, **kw):
    return lax.dot_general(a, b, (((1,), (1,)), ((), ())), preferred_element_type=F32, **kw)


def _dot_tn(a, b):
    return lax.dot_general(a, b, (((0,), (0,)), ((), ())), preferred_element_type=F32)


def _dot(a, b):
    return jnp.dot(a, b, preferred_element_type=F32)


def _delta_kernel(qkv_ref, z_ref, ba_ref, cinit_ref, s0_ref, cw_ref, alog_ref, dtb_ref, alogt_ref, dtbt_ref,
                  onw_ref, *rest, bb, tb, chunk, conv_done):
    o_ref, s_ref, ext_ref = rest[-3:]
    t = pl.program_id(1)
    rows = bb * tb
    cpb = tb // chunk
    heads = list(range(DN_HEADS))
    mm = BF16 if rows >= 16 else F32
    mc = BF16 if chunk >= 16 else F32

    @pl.when(t == 0)
    def _():
        s_ref[...] = s0_ref[...]

    if conv_done:
        y = qkv_ref[...].astype(F32).reshape(rows, CONV_CH)
        qn = [y[:, h * DN_DK:(h + 1) * DN_DK] for h in heads]
        kn = [y[:, DN_QK + h * DN_DK:DN_QK + (h + 1) * DN_DK] for h in heads]
        vh = [y[:, 2 * DN_QK + h * DN_DV:2 * DN_QK + (h + 1) * DN_DV] for h in heads]
    else:
        @pl.when(t == 0)
        def _():
            ext_ref[:, 0:CONV_HALO, :] = cinit_ref[...]

        @pl.when(t > 0)
        def _():
            ext_ref[:, 0:CONV_HALO, :] = ext_ref[:, tb:tb + CONV_HALO, :]

        ext_ref[:, CONV_HALO:CONV_HALO + tb, :] = qkv_ref[...].astype(F32)
        qn, kn, vh = _qkv_heads(_short_conv(ext_ref, cw_ref, tb).reshape(rows, CONV_CH))

    gr = min(rows, GROUP_ROWS)
    groups = rows // gr
    spg = gr // tb
    keys = [(gi, h) for gi in range(groups) for h in heads]

    def part(v, gi):
        return v[gi * gr:(gi + 1) * gr]

    bav = ba_ref[...].astype(F32).reshape(rows, 128)
    zv = z_ref[...].astype(F32).reshape(rows, DN_VW)

    row = lax.broadcasted_iota(jnp.int32, (gr, gr), 0)
    col = lax.broadcasted_iota(jnp.int32, (gr, gr), 1)
    shift = int(math.log2(chunk))
    same = (row >> shift) == (col >> shift)
    tri = same & (row >= col)
    strict = same & (row > col)
    lane = lax.broadcasted_iota(jnp.int32, (gr, 128), 1)

    gcol, bcol, grow = {}, {}, {}
    for gi in range(groups):
        bav_g = part(bav, gi)
        if gr == 128:
            sub = bav_g.T[0:2 * DN_HEADS, :]
            g_t = -jnp.exp(alogt_ref[...]) * _softplus(sub + dtbt_ref[...])
            gcum_t = jnp.dot(g_t, (same & (col >= row)).astype(F32), preferred_element_type=F32,
                             precision=lax.Precision.HIGHEST)
            live = lax.broadcasted_iota(jnp.int32, sub.shape, 0) < DN_HEADS
            packed = jnp.where(live, jax.nn.sigmoid(sub), gcum_t)
            cols = jnp.concatenate([packed, jnp.zeros((gr - 2 * DN_HEADS, gr), F32)], axis=0).T
            beta_full = gcum = cols
            for h in heads:
                grow[gi, h] = gcum_t[DN_HEADS + h:DN_HEADS + h + 1, :]
        else:
            beta_full = jax.nn.sigmoid(bav_g)
            g_full = -jnp.exp(alog_ref[...]) * _softplus(bav_g + dtb_ref[...])
            gcum = jnp.dot(tri.astype(F32), g_full, preferred_element_type=F32, precision=lax.Precision.HIGHEST)
            ones = jnp.ones((gr, 128), F32)
            for h in heads:
                grow[gi, h] = _dot_nt(ones, jnp.where(lane == DN_HEADS + h, gcum, 0.0),
                                      precision=lax.Precision.HIGHEST)
        for h in heads:
            gcol[gi, h] = jnp.sum(jnp.where(lane == DN_HEADS + h, gcum, 0.0), axis=1, keepdims=True)
            bcol[gi, h] = jnp.sum(jnp.where(lane == h, beta_full, 0.0), axis=1, keepdims=True)
    eg = {k: jnp.exp(gcol[k]) for k in keys}
    qs = {(gi, h): part(qn[h], gi) for gi, h in keys}
    ks = {(gi, h): part(kn[h], gi) for gi, h in keys}
    vs = {(gi, h): part(vh[h], gi) for gi, h in keys}

    bshift = min(shift, 4)
    decay = {k: jnp.where(tri, jnp.exp(jnp.where(tri, gcol[k] - grow[k], 0.0)), 0.0) for k in keys}
    kb = {k: ks[k].astype(mm) for k in keys}
    kk = {k: _dot_nt(kb[k], kb[k]) for k in keys}
    qk = {k: _dot_nt(qs[k].astype(mm), kb[k]) for k in keys}
    m = {k: jnp.where(strict, kk[k] * decay[k] * bcol[k], 0.0) for k in keys}
    p = {k: jnp.where((row >> bshift) == (col >> bshift), -m[k], 0.0) for k in keys}
    xp = p
    for _ in range(bshift - 1):
        xb = {k: xp[k].astype(mm) for k in keys}
        xp = {k: _dot(xb[k], xb[k]) for k in keys}
        p = {k: p[k] + xp[k] + _dot(p[k].astype(mm), xp[k].astype(mm)) for k in keys}
    for s in range(bshift, shift):
        lower = ((row >> (s + 1)) == (col >> (s + 1))) & ((row >> s) > (col >> s))
        c_blk = {k: jnp.where(lower, m[k], 0.0) for k in keys}
        pb = {k: p[k].astype(mm) for k in keys}
        a = {k: c_blk[k] + _dot(pb[k], c_blk[k].astype(mm)) for k in keys}
        p = {k: p[k] - a[k] - _dot(a[k].astype(mm), pb[k]) for k in keys}
    rhs = {k: jnp.concatenate([vs[k] * bcol[k], ks[k] * (bcol[k] * eg[k])], axis=1) for k in keys}
    uw = {k: rhs[k] + _dot(p[k].astype(mm), rhs[k].astype(mm)) for k in keys}
    qkd = {k: (qk[k] * decay[k]).astype(mm) for k in keys}
    qg = {k: qs[k] * eg[k] for k in keys}

    pairs = [(b, h) for b in range(bb) for h in heads]
    state = {bh: s_ref[bh[0], bh[1]] for bh in pairs}
    dlt = {k: [] for k in keys}
    oq = {k: [] for k in keys}
    for c in range(cpb):
        def rs(b):
            start = (b % spg) * tb + c * chunk
            return slice(start, start + chunk)
        prod = {}
        for b, h in pairs:
            k = (b // spg, h)
            lhs = jnp.concatenate([uw[k][rs(b), DN_DV:], qg[k][rs(b)]], axis=0).astype(mc)
            prod[b, h] = _dot(lhs, state[b, h].astype(mc))
        for b, h in pairs:
            k = (b // spg, h)
            r = rs(b)
            d = uw[k][r, :DN_DV] - prod[b, h][:chunk]
            glast = gcol[k][r.stop - 1:r.stop, :]
            kg = ks[k][r] * jnp.exp(glast - gcol[k][r])
            state[b, h] = state[b, h] * jnp.exp(glast) + _dot_tn(kg.astype(mc), d.astype(mc))
            dlt[k].append((r.start, d))
            oq[k].append((r.start, prod[b, h][chunk:]))
    for b, h in pairs:
        s_ref[b, h] = state[b, h]

    def stack(parts):
        parts = [v for _, v in sorted(parts, key=lambda sv: sv[0])]
        return parts[0] if len(parts) == 1 else jnp.concatenate(parts, axis=0)

    for gi, h in keys:
        hs = slice(h * DN_DV, (h + 1) * DN_DV)
        o = stack(oq[gi, h]) + _dot(qkd[gi, h], stack(dlt[gi, h]).astype(mm))
        zh = part(zv, gi)[:, hs]
        gated = _rms(o, onw_ref[...]) * (zh * jax.nn.sigmoid(zh))
        o_ref[gi * spg:(gi + 1) * spg, :, hs] = gated.reshape(spg, tb, DN_DV).astype(o_ref.dtype)


def _delta(h3, cinit, s0, s0_layer, cw, gate, onw, bb, tb, chunk, stack=None, conv_done=False):
    b, t, _ = h3.shape
    assert b % bb == 0 and t % tb == 0 and tb % chunk == 0
    state_block = (None, bb, DN_HEADS, DN_DK, DN_DV)
    in_specs = [
        pl.BlockSpec((bb, tb, CONV_CH), lambda i, j: (i, j, QKV_OFF // CONV_CH)),
        pl.BlockSpec((bb, tb, DN_VW), lambda i, j: (i, j, Z_OFF // DN_VW)),
        pl.BlockSpec((bb, tb, 128), lambda i, j: (i, j, BA_OFF // 128)),
        pl.BlockSpec((bb, CONV_HALO, CONV_CH), lambda i, j: (i, 0, 0)),
        pl.BlockSpec(state_block, lambda i, j: (s0_layer, i, 0, 0, 0)),
        pl.BlockSpec((CONV_W, CONV_CH), lambda i, j: (0, 0)),
        pl.BlockSpec((1, 128), lambda i, j: (0, 0)),
        pl.BlockSpec((1, 128), lambda i, j: (0, 0)),
        pl.BlockSpec((2 * DN_HEADS, 128), lambda i, j: (0, 0)),
        pl.BlockSpec((2 * DN_HEADS, 128), lambda i, j: (0, 0)),
        pl.BlockSpec((1, DN_DV), lambda i, j: (0, 0)),
    ]
    args = [h3, h3, h3, cinit, s0, cw, *gate, onw]
    layer, depth, prev = (0, 1, None) if stack is None else stack
    aliases = {}
    if prev is not None:
        in_specs.append(pl.BlockSpec(memory_space=pl.ANY))
        aliases = {len(args): 1}
        args.append(prev)
    o, s = pl.pallas_call(
        functools.partial(_delta_kernel, bb=bb, tb=tb, chunk=chunk, conv_done=conv_done),
        grid=(b // bb, t // tb),
        in_specs=in_specs,
        out_specs=[
            pl.BlockSpec((bb, tb, DN_VW), lambda i, j: (i, j, 0)),
            pl.BlockSpec(state_block, lambda i, j: (layer, i, 0, 0, 0)),
        ],
        out_shape=[
            jax.ShapeDtypeStruct((b, t, DN_VW), h3.dtype),
            jax.ShapeDtypeStruct((depth, b, DN_HEADS, DN_DK, DN_DV), F32),
        ],
        scratch_shapes=[pltpu.VMEM((bb, CONV_HALO + tb, CONV_CH), F32)],
        input_output_aliases=aliases,
        compiler_params=_cparams(("parallel", "arbitrary")),
        name="delta",
    )(*args)
    return o, (s[0] if stack is None else s)


def _rope(x, cos, sin):
    width = x.shape[-1]
    reps = width // cos.shape[-1]
    if reps > 1:
        cos = jnp.concatenate([cos] * reps, axis=1)
        sin = jnp.concatenate([sin] * reps, axis=1)
    lane = lax.broadcasted_iota(jnp.int32, x.shape, 1)
    first_half = (lane & (SWA_HD - 1)) < (SWA_HD // 2)
    other = jnp.where(first_half, pltpu.roll(x, width - SWA_HD // 2, 1), pltpu.roll(x, SWA_HD // 2, 1))
    return x * cos + other * sin


def _pad_heads(x, lo, fill):
    xr = pltpu.roll(x, SWA_HD, 1)
    z = jnp.full_like(x, fill)
    return [jnp.where(lo, x, z), jnp.where(lo, z, xr), jnp.where(lo, xr, z), jnp.where(lo, z, x)]


def _swa_prompt_kernel(sinks_ref, q_ref, kv_ref, meta_ref, cos_ref, sin_ref, o_ref, klast_ref, kcat_ref, vcat_ref,
                       *, pos0):
    n = pl.program_id(1)
    bb = q_ref.shape[0]
    blk = SWA_WINDOW
    lo = lax.broadcasted_iota(jnp.int32, (blk, 128), 1) < SWA_HD

    def put(b, half, k, v):
        rows = slice(half * blk, (half + 1) * blk)
        for i, (a, c) in enumerate(zip(_pad_heads(k, lo, 0.0), _pad_heads(v, lo, 1.0))):
            kcat_ref[b, i, rows, :] = a.astype(BF16)
            vcat_ref[b, i, rows, :] = c.astype(BF16)

    @pl.when(n == 0)
    def _():
        for b in range(bb):
            put(b, 0, meta_ref[:, :SWA_KV_WIDTH], meta_ref[:, SWA_KV_WIDTH:])

    @pl.when(n > 0)
    def _():
        kcat_ref[:, :, 0:blk, :] = kcat_ref[:, :, blk:2 * blk, :]
        vcat_ref[:, :, 0:blk, :] = vcat_ref[:, :, blk:2 * blk, :]

    cos = cos_ref[...]
    sin = sin_ref[...]
    qt = {}
    for b in range(bb):
        q = _rope(q_ref[b].astype(F32), cos, sin) * (SWA_HD ** -0.5)
        kv = kv_ref[b].astype(F32)
        k = _rope(kv[:, :SWA_KV_WIDTH], cos, sin)
        klast_ref[b] = k
        put(b, 1, k, kv[:, SWA_KV_WIDTH:])
        for i in range(SWA_HEADS // 2):
            qt[b, i] = q[:, i * 128:(i + 1) * 128].astype(BF16)

    r = lax.broadcasted_iota(jnp.int32, (blk, 2 * blk), 0)
    j = lax.broadcasted_iota(jnp.int32, (blk, 2 * blk), 1)
    jmin = jnp.maximum(blk - pos0 - n * blk, 0)
    mask = ((j < blk) & (j > r) & (j >= jmin)) | ((j >= blk) & ((j - blk) <= r))
    keys = [(b, h) for b in range(bb) for h in range(SWA_HEADS)]
    s = {(b, h): jnp.where(mask, _dot_nt(qt[b, h // 2], kcat_ref[b, 2 * (h // SWA_GROUP) + h % 2]), NEG_BIG)
         for b, h in keys}
    m = {(b, h): jnp.maximum(jnp.max(s[b, h], axis=1, keepdims=True), sinks_ref[h]) for b, h in keys}
    e = {k: jnp.exp(s[k] - m[k]).astype(BF16) for k in keys}
    sink_w = {(b, h): jnp.exp(sinks_ref[h] - m[b, h]) for b, h in keys}
    for b in range(bb):
        for i in range(SWA_HEADS // 2):
            he, ho = 2 * i, 2 * i + 1
            g = he // SWA_GROUP
            a_e = _dot(e[b, he], vcat_ref[b, 2 * g])
            a_o = _dot(e[b, ho], vcat_ref[b, 2 * g + 1])
            den = (jnp.where(lo, pltpu.roll(a_e, SWA_HD, 1), pltpu.roll(a_o, SWA_HD, 1))
                   + jnp.where(lo, sink_w[b, he], sink_w[b, ho]))
            o_ref[b, :, i * 128:(i + 1) * 128] = (jnp.where(lo, a_e, a_o) / den).astype(o_ref.dtype)


def _swa_prompt(h3, meta_kv, cos, sin, sinks, pos0, bb):
    b, t, _ = h3.shape
    blk = SWA_WINDOW
    assert t % blk == 0 and b % bb == 0
    return pl.pallas_call(
        functools.partial(_swa_prompt_kernel, pos0=pos0),
        grid=(b // bb, t // blk),
        in_specs=[
            pl.BlockSpec(memory_space=pltpu.SMEM),
            pl.BlockSpec((bb, blk, SWA_WIDTH), lambda i, n: (i, n, QC_OFF // SWA_WIDTH)),
            pl.BlockSpec((bb, blk, 256), lambda i, n: (i, n, KVC_OFF // 256)),
            pl.BlockSpec((blk, 256), lambda i, n: (0, 0)),
            pl.BlockSpec((blk, 128), lambda i, n: (n, 0)),
            pl.BlockSpec((blk, 128), lambda i, n: (n, 0)),
        ],
        out_specs=[
            pl.BlockSpec((bb, blk, SWA_WIDTH), lambda i, n: (i, n, 0)),
            pl.BlockSpec((bb, blk, SWA_KV_WIDTH), lambda i, n: (i, 0, 0)),
        ],
        out_shape=[
            jax.ShapeDtypeStruct((b, t, SWA_WIDTH), h3.dtype),
            jax.ShapeDtypeStruct((b, blk, SWA_KV_WIDTH), F32),
        ],
        scratch_shapes=[pltpu.VMEM((bb, 4, 2 * blk, 128), BF16), pltpu.VMEM((bb, 4, 2 * blk, 128), BF16)],
        compiler_params=_cparams(("parallel", "arbitrary")),
        name="swa_prompt",
    )(sinks, h3, h3, meta_kv, cos, sin)


def _swa_step_kernel(sink_ref, q_ref, kv_ref, *rest, bb, t_new, window):
    if window:
        kc_ref, vc_ref, cos_ref, sin_ref = rest[:4]
        o_ref, kout_ref, vout_ref = rest[-3:]
    else:
        cos_ref, sin_ref, o_ref, knew_ref = rest
    cos = cos_ref[...]
    sin = sin_ref[...]
    rows = SWA_HEADS * t_new
    mn = BF16 if t_new >= 16 else F32
    lo = lax.broadcasted_iota(jnp.int32, (t_new, 128), 1) < SWA_HD
    tq = lax.broadcasted_iota(jnp.int32, (rows, t_new), 0) & (t_new - 1)
    mask_n = lax.broadcasted_iota(jnp.int32, (rows, t_new), 1) <= tq
    if window:
        diff = ((lax.broadcasted_iota(jnp.int32, (rows, window), 0) & (t_new - 1)) + window
                - lax.broadcasted_iota(jnp.int32, (rows, window), 1))
        mask_c = (diff >= 0) & (diff < SWA_WINDOW)
    sink = sink_ref[...]
    seqs = range(bb)

    qall, kk, vv = [], [], []
    for s in seqs:
        q = _rope(q_ref[s].astype(F32), cos, sin) * (SWA_HD ** -0.5)
        kv = kv_ref[s].astype(F32)
        k = _rope(kv[:, :SWA_KV_WIDTH], cos, sin)
        if not window:
            knew_ref[s] = k
        pieces = []
        for h in range(SWA_HEADS):
            g = h // SWA_GROUP
            tile = q[:, (h // 2) * 128:(h // 2 + 1) * 128]
            if h % 2 != g:
                tile = pltpu.roll(tile, SWA_HD, 1)
            pieces.append(jnp.where(lo, tile, 0.0) if g == 0 else jnp.where(lo, 0.0, tile))
        qall.append(jnp.concatenate(pieces, axis=0))
        kk.append(k)
        vv.append(kv[:, SWA_KV_WIDTH:])

    s_n = [jnp.where(mask_n, _dot_nt(qall[s].astype(mn), kk[s].astype(mn)), NEG_BIG) for s in seqs]
    m = [jnp.maximum(jnp.max(s_n[s], axis=1, keepdims=True), sink) for s in seqs]
    if window:
        s_c = [jnp.where(mask_c, _dot(qall[s].astype(BF16), kc_ref[s].astype(BF16)), NEG_BIG) for s in seqs]
        m = [jnp.maximum(m[s], jnp.max(s_c[s], axis=1, keepdims=True)) for s in seqs]
    e_n = [jnp.exp(s_n[s] - m[s]) for s in seqs]
    den = [jnp.sum(e_n[s], axis=1, keepdims=True) + jnp.exp(sink - m[s]) for s in seqs]
    acc = [_dot(e_n[s].astype(mn), vv[s].astype(mn)) for s in seqs]
    if window:
        e_c = [jnp.exp(s_c[s] - m[s]) for s in seqs]
        den = [den[s] + jnp.sum(e_c[s], axis=1, keepdims=True) for s in seqs]
        acc = [acc[s] + _dot_nt(e_c[s].astype(BF16), vc_ref[s].astype(BF16)) for s in seqs]

        def slide(old_t, new):
            kept = pltpu.roll(old_t, window - t_new, 1)
            fresh = jnp.concatenate([jnp.zeros((window - t_new, SWA_KV_WIDTH), F32), new], axis=0).T
            newest = lax.broadcasted_iota(jnp.int32, kept.shape, 1) >= window - t_new
            return jnp.where(newest, fresh, kept)

        for s in seqs:
            kout_ref[s] = slide(kc_ref[s], kk[s])
            vout_ref[s] = slide(vc_ref[s], vv[s])
    for s in seqs:
        a = acc[s] / den[s]
        for i in range(SWA_HEADS // 2):
            he, ho = 2 * i, 2 * i + 1
            g = he // SWA_GROUP
            a_e = a[he * t_new:(he + 1) * t_new]
            a_o = a[ho * t_new:(ho + 1) * t_new]
            if g == 1:
                a_e = pltpu.roll(a_e, SWA_HD, 1)
            else:
                a_o = pltpu.roll(a_o, SWA_HD, 1)
            o_ref[s, :, i * 128:(i + 1) * 128] = jnp.where(lo, a_e, a_o).astype(o_ref.dtype)


def _swa_step(h3, caches, cos, sin, sinks, bb):
    b, t, _ = h3.shape
    assert b % bb == 0
    sink_rows = jnp.repeat(sinks, t)[:, None]
    in_specs = [
        pl.BlockSpec((SWA_HEADS * t, 1), lambda i: (0, 0)),
        pl.BlockSpec((bb, t, SWA_WIDTH), lambda i: (i, 0, QC_OFF // SWA_WIDTH)),
        pl.BlockSpec((bb, t, 256), lambda i: (i, 0, KVC_OFF // 256)),
    ]
    args = [sink_rows, h3, h3]
    out_specs = [pl.BlockSpec((bb, t, SWA_WIDTH), lambda i: (i, 0, 0))]
    out_shape = [jax.ShapeDtypeStruct((b, t, SWA_WIDTH), h3.dtype)]
    window = 0
    aliases = {}
    if caches is None:
        out_specs.append(pl.BlockSpec((bb, t, SWA_KV_WIDTH), lambda i: (i, 0, 0)))
        out_shape.append(jax.ShapeDtypeStruct((b, t, SWA_KV_WIDTH), F32))
    else:
        k_t, v_t, layer, depth, prev_k, prev_v = caches
        window = k_t.shape[-1]
        assert window == SWA_KV_WIDTH and t <= window
        cache_block = pl.BlockSpec((None, bb, SWA_KV_WIDTH, window), lambda i: (layer, i, 0, 0))
        in_specs += [cache_block] * 2
        args += [k_t, v_t]
        out_specs += [cache_block] * 2
        out_shape += [jax.ShapeDtypeStruct((depth, b, SWA_KV_WIDTH, window), F32)] * 2
    in_specs += [pl.BlockSpec((t, 128), lambda i: (0, 0))] * 2
    args += [cos, sin]
    if caches is not None and prev_k is not None:
        in_specs += [pl.BlockSpec(memory_space=pl.ANY)] * 2
        aliases = {len(args): 1, len(args) + 1: 2}
        args += [prev_k, prev_v]
    return pl.pallas_call(
        functools.partial(_swa_step_kernel, bb=bb, t_new=t, window=window),
        grid=(b // bb,),
        in_specs=in_specs,
        out_specs=out_specs,
        out_shape=out_shape,
        input_output_aliases=aliases,
        compiler_params=_cparams(("parallel",)),
        name="swa_step",
    )(*args)


def _merge_mlp_kernel(oa_ref, ob_ref, oc_ref, ga_ref, gb_ref, gc_ref, x_ref, pa_ref, pb_ref, pc_ref, wo_ref,
                      nw_ref, wu_ref, wd_ref, fnw_ref, o_ref, *, final):
    def branch(o_r, g_r, p_r):
        return jax.nn.sigmoid(g_r[...].astype(F32)) * jnp.dot(o_r[...].astype(BF16), p_r[...],
                                                              preferred_element_type=F32)

    m = branch(oa_ref, ga_ref, pa_ref) + branch(ob_ref, gb_ref, pb_ref) + branch(oc_ref, gc_ref, pc_ref)
    h1 = x_ref[...] + jnp.dot(m.astype(BF16), wo_ref[...], preferred_element_type=F32)
    xn = _rms(h1, nw_ref[...]).astype(BF16)
    y = h1
    for j in range(D_FF // MLP_TF):
        fs = slice(j * MLP_TF, (j + 1) * MLP_TF)
        a = jnp.maximum(jnp.dot(xn, wu_ref[:, fs], preferred_element_type=F32), 0.0)
        y = y + jnp.dot((a * a).astype(BF16), wd_ref[fs, :], preferred_element_type=F32)
    o_ref[...] = _rms(y, fnw_ref[...]) if final else y


def _merge_mlp(oa, ob, oc, h, x, pa, pb, pc, wo, nw, wu, wd, fnw, final):
    m = x.shape[0]
    tm = min(m, 512)
    assert m % tm == 0
    row = lambda w: pl.BlockSpec((tm, w), lambda i: (i, 0))
    gate = lambda c: pl.BlockSpec((tm, D_MODEL), lambda i: (i, c))
    resident = lambda a: pl.BlockSpec(a.shape, lambda i: (0, 0), pipeline_mode=pl.Buffered(1))
    return pl.pallas_call(
        functools.partial(_merge_mlp_kernel, final=final),
        grid=(m // tm,),
        in_specs=[row(512), row(512), row(512), gate(0), gate(1), gate(2), row(D_MODEL),
                  resident(pa), resident(pb), resident(pc), resident(wo),
                  resident(nw), resident(wu), resident(wd), resident(fnw)],
        out_specs=row(D_MODEL),
        out_shape=jax.ShapeDtypeStruct((m, D_MODEL), F32),
        compiler_params=_cparams(("parallel",)),
        name="merge_mlp",
    )(oa, ob, oc, h, h, h, x, pa, pb, pc, wo, nw, wu, wd, fnw)


def _rope_tables(pos0, t):
    half = SWA_HD // 2
    inv = ROPE_THETA ** (-jnp.arange(half, dtype=F32) / half)
    ang = (pos0 + jnp.arange(t)).astype(F32)[:, None] * inv[None, :]
    cos = jnp.cos(ang)
    sin = jnp.sin(ang)
    return jnp.tile(cos, (1, 4)), jnp.tile(jnp.concatenate([-sin, sin], axis=1), (1, 2))


def _prep_layer(l, norm1_w, w_in, pool_w, pool_scale, dn_conv_w, dn_a_log, dn_dt_bias, dn_onorm_w, swa_sinks,
                proj_a, proj_b, proj_c, w_out, norm2_w, w_up, w_down):
    w = w_in[l]
    segments = ((3336, 6408), (512, 2048), (0, 512), (2048, 2560), (2568, 3080), (3080, 3336), (2560, 2568))
    w_perm = jnp.concatenate(
        [w[:, a:b].astype(BF16) for a, b in segments]
        + [jnp.zeros((D_MODEL, H_WIDTH - BA_OFF - 2 * DN_HEADS), BF16)], axis=1)
    lane_pad = lambda v: jnp.zeros((1, 128), F32).at[0, DN_HEADS:2 * DN_HEADS].set(v.astype(F32))
    row_pad = lambda v: jnp.zeros((2 * DN_HEADS, 128), F32).at[DN_HEADS:].set(
        jnp.broadcast_to(v.astype(F32)[:, None], (DN_HEADS, 128)))
    return dict(
        norm1=norm1_w[l][None].astype(F32), w_in=w_perm,
        pool_w=pool_w[l].astype(BF16), pool_scale=pool_scale[l][None].astype(F32),
        conv_w=dn_conv_w[l].astype(F32),
        gate=(lane_pad(dn_a_log[l]), lane_pad(dn_dt_bias[l]), row_pad(dn_a_log[l]), row_pad(dn_dt_bias[l])),
        onw=dn_onorm_w[l][None].astype(F32), sinks=swa_sinks[l].astype(F32),
        pa=proj_a[l].astype(BF16), pb=proj_b[l].astype(BF16), pc=proj_c[l].astype(BF16),
        wo=w_out[l].astype(BF16), norm2=norm2_w[l][None].astype(F32),
        wu=w_up[l].astype(BF16), wd=w_down[l].astype(BF16))


def _group_step(x3, p, fnw, final, pos0, pool_init, conv_init, s0, kind, swa_extra, pool_tiles, delta_tiles,
                swa_bb=1, s_stack=None, act=F32, fuse_conv=False):
    b, t, _ = x3.shape
    x2 = x3.reshape(b * t, D_MODEL)
    h, conv_tails = _inproj(x2, p['norm1'], p['w_in'], act,
                            conv=(conv_init, p['conv_w'], t) if fuse_conv else None)
    h3 = h.reshape(b, t, H_WIDTH)
    o_a = _pool(h3, pool_init, p['pool_w'], p['pool_scale'], pos0, *pool_tiles)
    o_b, s_new = _delta(h3, conv_init, s0[0], s0[1], p['conv_w'], p['gate'], p['onw'], *delta_tiles,
                        stack=s_stack, conv_done=fuse_conv)
    cos, sin = _rope_tables(pos0, t)
    if kind == 'prompt':
        o_c, k_rot = _swa_prompt(h3, swa_extra, cos, sin, p['sinks'], pos0, swa_bb)
    else:
        o_c, *k_rot = _swa_step(h3, swa_extra, cos, sin, p['sinks'], swa_bb)
    m = b * t
    out = _merge_mlp(o_a.reshape(m, 512), o_b.reshape(m, 512), o_c.reshape(m, 512), h, x2,
                     p['pa'], p['pb'], p['pc'], p['wo'], p['norm2'], p['wu'], p['wd'], fnw, final)
    return out.reshape(b, t, D_MODEL), h3, s_new, k_rot, conv_tails


def kernel(x_prompt, x_sample, state_pool, state_conv, state_delta, cache_swa_k, cache_swa_v, meta_tokens, norm1_w, w_in, pool_w, pool_scale, dn_conv_w, dn_a_log, dn_dt_bias, dn_onorm_w, swa_sinks, proj_a, proj_b, proj_c, w_out, norm2_w, w_up, w_down, final_norm_w):
    depth = w_in.shape[0]
    bp, tp, _ = x_prompt.shape
    bs, ts, _ = x_sample.shape
    past = cache_swa_k.shape[2]
    fnw = final_norm_w[None].astype(F32)

    xm = meta_tokens[None].astype(F32)
    xp = x_prompt
    xs = x_sample
    outs = {k: [] for k in ('pool_p', 'conv_p', 'k_p', 'v_p', 'pool_s', 'conv_s')}
    delta_p = delta_s = None
    k_s = v_s = None
    to_t = lambda c: c.transpose(0, 1, 3, 4, 2).reshape(depth, bs, SWA_KV_WIDTH, past)
    cache_k_t, cache_v_t = to_t(cache_swa_k), to_t(cache_swa_v)
    for l in range(depth):
        p = _prep_layer(l, norm1_w, w_in, pool_w, pool_scale, dn_conv_w, dn_a_log, dn_dt_bias, dn_onorm_w,
                        swa_sinks, proj_a, proj_b, proj_c, w_out, norm2_w, w_up, w_down)
        final = l == depth - 1

        xm, hm, s_m, k_m, _ = _group_step(
            xm, p, fnw, final, 0,
            jnp.zeros((1, POOL_HALO, POOL_WIDTH), F32), jnp.zeros((1, CONV_HALO, CONV_CH), F32),
            (jnp.zeros((1, 1, DN_HEADS, DN_DK, DN_DV), F32), 0), 'step', None,
            (1, N_META), (1, N_META, N_META))

        meta_kv = jnp.zeros((SWA_WINDOW, 256), F32)
        meta_kv = meta_kv.at[SWA_WINDOW - N_META:, :SWA_KV_WIDTH].set(k_m[0][0])
        meta_kv = meta_kv.at[SWA_WINDOW - N_META:, SWA_KV_WIDTH:].set(hm[0, :, KVC_OFF + SWA_KV_WIDTH:KVC_OFF + 256])
        xp, hp, s_p, k_p, tails_p = _group_step(
            xp, p, fnw, final, N_META,
            jnp.broadcast_to(hm[:, :, U_OFF:U_OFF + POOL_WIDTH], (bp, POOL_HALO, POOL_WIDTH)),
            jnp.broadcast_to(hm[:, N_META - CONV_HALO:, QKV_OFF:QKV_OFF + CONV_CH], (bp, CONV_HALO, CONV_CH)),
            (jnp.broadcast_to(s_m[None], (1, bp, DN_HEADS, DN_DK, DN_DV)), 0), 'prompt', meta_kv,
            (1, 512), (4, 2 * DN_CHUNK, DN_CHUNK), swa_bb=bp, s_stack=(l, depth, delta_p), act=BF16, fuse_conv=True)
        delta_p = s_p
        outs['pool_p'].append(hp[:, tp - POOL_BUF:, U_OFF:U_OFF + POOL_WIDTH].astype(F32))
        if tails_p is None:
            outs['conv_p'].append(hp[:, tp - (CONV_W - 1):, QKV_OFF:QKV_OFF + CONV_CH].astype(F32))
        else:
            outs['conv_p'].append(tails_p.reshape(bp, -1, CONV_HALO, CONV_CH)[:, -1, CONV_HALO - (CONV_W - 1):])
        outs['k_p'].append(k_p.reshape(bp, SWA_WINDOW, SWA_KV_HEADS, SWA_HD))
        outs['v_p'].append(hp[:, tp - SWA_WINDOW:, KVC_OFF + SWA_KV_WIDTH:KVC_OFF + 256].astype(F32)
                           .reshape(bp, SWA_WINDOW, SWA_KV_HEADS, SWA_HD))

        xs, hs, s_s, (k_s, v_s), _ = _group_step(
            xs, p, fnw, final, PAST_LEN,
            jnp.concatenate([jnp.zeros((bs, POOL_HALO - POOL_BUF, POOL_WIDTH), F32), state_pool[l]], axis=1),
            jnp.concatenate([jnp.zeros((bs, CONV_HALO - (CONV_W - 1), CONV_CH), F32), state_conv[l]], axis=1),
            (state_delta, l), 'step', (cache_k_t, cache_v_t, l, depth, k_s, v_s),
            (bs, ts), (16, ts, ts), 16, s_stack=(l, depth, delta_s))
        delta_s = s_s
        outs['pool_s'].append(jnp.concatenate([state_pool[l], hs[:, :, U_OFF:U_OFF + POOL_WIDTH]], axis=1)[:, -POOL_BUF:])
        outs['conv_s'].append(jnp.concatenate([state_conv[l], hs[:, :, QKV_OFF:QKV_OFF + CONV_CH]], axis=1)[:, -(CONV_W - 1):])

    st = {k: jnp.stack(v) for k, v in outs.items()}
    from_t = lambda c: c.reshape(depth, bs, SWA_KV_HEADS, SWA_HD, past).transpose(0, 1, 4, 2, 3)
    return (xp, xs, st['pool_p'], st['conv_p'], delta_p, st['k_p'], st['v_p'],
            st['pool_s'], st['conv_s'], delta_s, from_t(k_s), from_t(v_s))
```
